```python
import jax, jax.numpy as jnp
from jax import lax
import numpy as np

D_MODEL = 4096
BATCH = 4
SEQ = 2048
DEPTH = 1
DEC_BATCH = 128
DEC_SEQ = 1
PAST_LEN = 16384
PAGE_SIZE = 128

N_META = 16
D_MIX = D_MODEL
D_GLA_V = D_MIX // 2
D_CONV = D_MIX - D_GLA_V
D_GLA_K = D_GLA_V // 2
N_GLA_HEADS = 4
DK = D_GLA_K // N_GLA_HEADS
DV = D_GLA_V // N_GLA_HEADS
GATE_RANK = 16
GATE_TAU = 16.0
GLA_CHUNK = 64
CONV_WIDTH = 3
D_FF = -(-8 * D_MODEL // (3 * 256)) * 256
ALPHA = (2 * DEPTH) ** 0.25
BETA = (8 * DEPTH) ** -0.25
LN_EPS = 1e-5
RMS_EPS = 1e-6
SPLIT_SIZES = [D_GLA_K, D_GLA_K, D_GLA_V, D_GLA_V, GATE_RANK, D_CONV, D_CONV, D_CONV]
SPLIT_IDX = np.cumsum(SPLIT_SIZES)[:-1].tolist()
D_IN = int(sum(SPLIT_SIZES))

kernel_name = "hymba_gla_shortconv_deepnorm_step"


def layer_norm(x, g, b):
    xf = x.astype(jnp.float32)
    mu = jnp.mean(xf, -1, keepdims=True)
    var = jnp.mean(jnp.square(xf - mu), -1, keepdims=True)
    return ((xf - mu) * lax.rsqrt(var + LN_EPS) * g.astype(jnp.float32) + b.astype(jnp.float32)).astype(x.dtype)


def gla_chunk(q, k, v, log_a, S0):
    L = q.shape[2]
    b = jnp.cumsum(log_a, axis=2)
    mask = jnp.tril(jnp.ones((L, L), dtype=bool))
    diff = b[:, :, :, None, :] - b[:, :, None, :, :]
    decay = jnp.exp(jnp.where(mask[None, None, :, :, None], diff, -jnp.inf))
    attn = jnp.einsum('bhid,bhjd,bhijd->bhij', q, k, decay)
    o = jnp.einsum('bhij,bhjv->bhiv', attn, v) + jnp.einsum('bhid,bhdv->bhiv', q * jnp.exp(b), S0)
    b_last = b[:, :, -1:, :]
    S = jnp.exp(b_last[:, :, 0, :])[..., None] * S0 + jnp.einsum('bhjd,bhjv->bhdv', k * jnp.exp(b_last - b), v)
    return o, S


def gla_scan(q, k, v, log_a, S0):
    B, H, L, _ = q.shape
    C = min(GLA_CHUNK, L)
    n = -(-L // C)
    pad = n * C - L

    def prep(t):
        t = jnp.pad(t, ((0, 0), (0, 0), (0, pad), (0, 0)))
        return t.reshape(B, H, n, C, t.shape[-1]).transpose(2, 0, 1, 3, 4)

    def step(S, inp):
        qc, kc, vc, ac = inp
        o, S = gla_chunk(qc, kc, vc, ac, S)
        return S, o

    S, o = lax.scan(step, S0, (prep(q), prep(k), prep(v), prep(log_a)))
    o = o.transpose(1, 2, 0, 3, 4).reshape(B, H, n * C, DV)[:, :, :L]
    return o, S


def mixer(h, S0, buf, n_lead, w_in, w_a2, b_a, gla_norm_g, conv_w, w_out):
    B, T, _ = h.shape
    f32 = jnp.float32
    proj = jnp.einsum('btd,de->bte', h, w_in)
    q, k, v, g, a_lr, cb, cc, ch = jnp.split(proj, SPLIT_IDX, axis=-1)

    def heads(t, d):
        return t.reshape(B, T, N_GLA_HEADS, d).transpose(0, 2, 1, 3).astype(f32)

    qh = heads(q, DK) * (DK ** -0.5)
    kh = heads(k, DK)
    vh = heads(v, DV)
    z = jnp.einsum('btr,rk->btk', a_lr, w_a2) + b_a
    log_a = heads(jax.nn.log_sigmoid(z.astype(f32)) / GATE_TAU, DK)
    S0 = S0.astype(f32)
    if n_lead > 0:
        o_lead, S = gla_chunk(qh[:, :, :n_lead], kh[:, :, :n_lead], vh[:, :, :n_lead], log_a[:, :, :n_lead], S0)
        o_rest, S = gla_scan(qh[:, :, n_lead:], kh[:, :, n_lead:], vh[:, :, n_lead:], log_a[:, :, n_lead:], S)
        o = jnp.concatenate([o_lead, o_rest], axis=2)
    else:
        o, S = gla_scan(qh, kh, vh, log_a, S0)
    o = o * lax.rsqrt(jnp.mean(o * o, -1, keepdims=True) + RMS_EPS) * gla_norm_g.astype(f32)
    o = o.transpose(0, 2, 1, 3).reshape(B, T, D_GLA_V).astype(h.dtype) * jax.nn.silu(g)

    u = cc * ch
    u_full = jnp.concatenate([buf.astype(u.dtype), u], axis=1)
    zc = conv_w[0] * u_full[:, 0:T]
    for i in range(1, CONV_WIDTH):
        zc = zc + conv_w[i] * u_full[:, i:i + T]
    yc = cb * zc
    new_buf = u_full[:, T:]

    y = jnp.einsum('btm,md->btd', jnp.concatenate([o, yc], axis=-1), w_out)
    return y, S, new_buf


def swiglu(h, w_gate, w_up, w_down):
    a = jnp.einsum('btd,df->btf', h, w_gate)
    u = jnp.einsum('btd,df->btf', h, w_up)
    return jnp.einsum('btf,fd->btd', jax.nn.silu(a) * u, w_down)


def setup_inputs(seed: int = 0) -> dict:
    key = jax.random.key(seed)
    ks = jax.random.split(key, 24)
    nrm = lambda k, s: jax.random.normal(k, s, dtype=jnp.float32)
    return {
        "x_prompt": nrm(ks[0], (BATCH, SEQ, D_MODEL)),
        "x_sample": nrm(ks[1], (DEC_BATCH, DEC_SEQ, D_MODEL)),
        "state_gla": nrm(ks[2], (DEPTH, DEC_BATCH, N_GLA_HEADS, DK, DV)),
        "state_conv": nrm(ks[3], (DEPTH, DEC_BATCH, CONV_WIDTH - 1, D_CONV)),
        "meta_tokens": nrm(ks[4], (N_META, D_MODEL)),
        "emb_ln_g": 1.0 + 0.02 * nrm(ks[5], (D_MODEL,)),
        "emb_ln_b": 0.02 * nrm(ks[6], (D_MODEL,)),
        "w_in": nrm(ks[7], (DEPTH, D_MODEL, D_IN)) * D_MODEL ** -0.5,
        "w_a2": nrm(ks[8], (DEPTH, GATE_RANK, D_GLA_K)) * GATE_RANK ** -0.5,
        "b_a": 0.1 * nrm(ks[9], (DEPTH, D_GLA_K)),
        "gla_norm_g": 1.0 + 0.02 * nrm(ks[10], (DEPTH, DV)),
        "conv_w": nrm(ks[11], (DEPTH, CONV_WIDTH, D_CONV)) * CONV_WIDTH ** -0.5,
        "w_out": nrm(ks[12], (DEPTH, D_MIX, D_MODEL)) * (D_MIX ** -0.5) * BETA,
        "ln1_g": 1.0 + 0.02 * nrm(ks[13], (DEPTH, D_MODEL)),
        "ln1_b": 0.02 * nrm(ks[14], (DEPTH, D_MODEL)),
        "w_ffn_gate": nrm(ks[15], (DEPTH, D_MODEL, D_FF)) * D_MODEL ** -0.5,
        "w_ffn_up": nrm(ks[16], (DEPTH, D_MODEL, D_FF)) * D_MODEL ** -0.5,
        "w_ffn_down": nrm(ks[17], (DEPTH, D_FF, D_MODEL)) * (D_FF ** -0.5) * BETA,
        "ln2_g": 1.0 + 0.02 * nrm(ks[18], (DEPTH, D_MODEL)),
        "ln2_b": 0.02 * nrm(ks[19], (DEPTH, D_MODEL)),
    }


def reference(x_prompt, x_sample, state_gla, state_conv, meta_tokens, emb_ln_g, emb_ln_b,
              w_in, w_a2, b_a, gla_norm_g, conv_w, w_out, ln1_g, ln1_b,
              w_ffn_gate, w_ffn_up, w_ffn_down, ln2_g, ln2_b):
    B = x_prompt.shape[0]
    meta = jnp.broadcast_to(meta_tokens[None].astype(x_prompt.dtype), (B, N_META, D_MODEL))
    hp = layer_norm(jnp.concatenate([meta, x_prompt], axis=1), emb_ln_g, emb_ln_b)
    hs = layer_norm(x_sample, emb_ln_g, emb_ln_b)

    gla_p, conv_p, gla_s, conv_s = [], [], [], []
    for l in range(DEPTH):
        lw = (w_in[l], w_a2[l], b_a[l], gla_norm_g[l], conv_w[l], w_out[l])
        S0p = jnp.zeros((B, N_GLA_HEADS, DK, DV), jnp.float32)
        buf0p = jnp.zeros((B, CONV_WIDTH - 1, D_CONV), hp.dtype)
        mp, Sp, bp = mixer(hp, S0p, buf0p, N_META, *lw)
        hp = layer_norm(ALPHA * hp + mp, ln1_g[l], ln1_b[l])
        hp = layer_norm(ALPHA * hp + swiglu(hp, w_ffn_gate[l], w_ffn_up[l], w_ffn_down[l]), ln2_g[l], ln2_b[l])
        ms, Ss, bs = mixer(hs, state_gla[l], state_conv[l], 0, *lw)
        hs = layer_norm(ALPHA * hs + ms, ln1_g[l], ln1_b[l])
        hs = layer_norm(ALPHA * hs + swiglu(hs, w_ffn_gate[l], w_ffn_up[l], w_ffn_down[l]), ln2_g[l], ln2_b[l])
        gla_p.append(Sp); conv_p.append(bp); gla_s.append(Ss); conv_s.append(bs)

    y_prompt = hp[:, N_META:]
    y_sample = hs
    return (y_prompt, y_sample, jnp.stack(gla_p), jnp.stack(conv_p), jnp.stack(gla_s), jnp.stack(conv_s))
```

```python
import functools

import jax
import jax.numpy as jnp
from jax import lax
from jax.experimental import pallas as pl
from jax.experimental.pallas import tpu as pltpu

f32 = jnp.float32
bf16 = jnp.bfloat16

D_MODEL = 4096
BATCH = 4
SEQ = 2048
DEC_BATCH = 128
N_META = 16
D_GLA_V = D_MODEL // 2
D_CONV = D_MODEL - D_GLA_V
D_GLA_K = D_GLA_V // 2
N_HEADS = 4
DK = D_GLA_K // N_HEADS
DV = D_GLA_V // N_HEADS
GATE_RANK = 16
GATE_TAU = 16.0
CONV_WIDTH = 3
D_FF = 11008
ALPHA = 2.0 ** 0.25
LN_EPS = 1e-5
RMS_EPS = 1e-6

LANES = 128
NP = BATCH * SEQ
R_MAIN = NP + DEC_BATCH
R_ALL = R_MAIN + LANES
D_PROJ = 2 * D_GLA_K + 2 * D_GLA_V + 3 * D_CONV
D_FF_PAD = 11264
CHUNK = 128
N_CHUNKS = SEQ // CHUNK
META_BLK = R_MAIN // CHUNK
VMEM_LIMIT = 56 * 1024 * 1024

COL_Q, COL_K, COL_V, COL_G = 0, D_GLA_K, 2 * D_GLA_K, 2 * D_GLA_K + D_GLA_V
COL_CB = COL_G + D_GLA_V
COL_CC = COL_CB + D_CONV
COL_CH = COL_CC + D_CONV


def _params(n_axes):
    return pltpu.CompilerParams(
        dimension_semantics=("arbitrary",) * n_axes, vmem_limit_bytes=VMEM_LIMIT)


def _layer_norm(x, g, b):
    mu = jnp.mean(x, axis=-1, keepdims=True)
    xc = x - mu
    var = jnp.mean(xc * xc, axis=-1, keepdims=True)
    return xc * lax.rsqrt(var + LN_EPS) * g + b


def _split3(x):
    hi = x.astype(bf16)
    r = x - hi.astype(f32)
    mid = r.astype(bf16)
    lo = (r - mid.astype(f32)).astype(bf16)
    return hi, mid, lo


def _dot(a, b, dims=(((1,), (0,)), ((), ()))):
    return lax.dot_general(a, b, dims, preferred_element_type=f32)


_NT = (((1,), (1,)), ((), ()))
_TN = (((0,), (0,)), ((), ()))


def _dot_sel(sel_bf16, x_f32, dims=(((1,), (0,)), ((), ()))):
    hi, mid, lo = _split3(x_f32)
    return _dot(sel_bf16, hi, dims) + _dot(sel_bf16, mid, dims) + _dot(sel_bf16, lo, dims)


def _dot_f32(a, b):
    ah, am, _ = _split3(a)
    bh, bm, _ = _split3(b)
    return _dot(ah, bh) + _dot(ah, bm) + _dot(am, bh)


def _log2(n):
    assert n & (n - 1) == 0
    return n.bit_length() - 1


def _log_sigmoid(z):
    return jnp.minimum(z, 0.0) - jnp.log1p(jnp.exp(-jnp.abs(z)))


LN_ROWS = 256


def _ln0_kernel(xp_ref, xt_ref, g_ref, b_ref, h_ref, hb_ref):
    i = pl.program_id(0)

    def emit(x):
        h = _layer_norm(x, g_ref[...], b_ref[...])
        h_ref[...] = h
        hb_ref[...] = h.astype(bf16)

    @pl.when(i < NP // LN_ROWS)
    def _():
        emit(xp_ref[...])

    @pl.when(i >= NP // LN_ROWS)
    def _():
        emit(xt_ref[...])


def _ln0(xp, xt, g, b):
    n_p = NP // LN_ROWS
    return pl.pallas_call(
        _ln0_kernel,
        grid=(R_ALL // LN_ROWS,),
        in_specs=[
            pl.BlockSpec((LN_ROWS, D_MODEL), lambda i: (jnp.minimum(i, n_p - 1), 0)),
            pl.BlockSpec((LN_ROWS, D_MODEL), lambda i: (0, 0)),
            pl.BlockSpec((1, D_MODEL), lambda i: (0, 0)),
            pl.BlockSpec((1, D_MODEL), lambda i: (0, 0)),
        ],
        out_specs=[
            pl.BlockSpec((LN_ROWS, D_MODEL), lambda i: (i, 0)),
            pl.BlockSpec((LN_ROWS, D_MODEL), lambda i: (i, 0)),
        ],
        out_shape=[jax.ShapeDtypeStruct((R_ALL, D_MODEL), f32),
                   jax.ShapeDtypeStruct((R_ALL, D_MODEL), bf16)],
        compiler_params=_params(1),
        name="ln0",
    )(xp, xt, g, b)


PROJ_TM, PROJ_TN = 768, 1024


def _proj_kernel(h_ref, w_ref, wa_ref, p_ref, a_ref):
    p_ref[...] = _dot(h_ref[...], w_ref[...]).astype(bf16)

    @pl.when(pl.program_id(1) == 0)
    def _():
        a_ref[...] = _dot(h_ref[...], wa_ref[...])


def _proj(hb, wm, wa):
    return pl.pallas_call(
        _proj_kernel,
        grid=(R_ALL // PROJ_TM, D_PROJ // PROJ_TN),
        in_specs=[
            pl.BlockSpec((PROJ_TM, D_MODEL), lambda i, j: (i, 0)),
            pl.BlockSpec((D_MODEL, PROJ_TN), lambda i, j: (0, j)),
            pl.BlockSpec((D_MODEL, LANES), lambda i, j: (0, 0)),
        ],
        out_specs=[
            pl.BlockSpec((PROJ_TM, PROJ_TN), lambda i, j: (i, j)),
            pl.BlockSpec((PROJ_TM, LANES), lambda i, j: (i, 0)),
        ],
        out_shape=[jax.ShapeDtypeStruct((R_ALL, D_PROJ), bf16),
                   jax.ShapeDtypeStruct((R_ALL, LANES), f32)],
        compiler_params=_params(2),
        name="proj",
    )(hb, wm, wa)


SUB = 8
LEVELS = (64, 32, 16, 8)


def _gla_prompt_kernel(q_ref, k_ref, v_ref, g_ref, alr_ref, wa2_ref, ba_ref, ng_ref,
                       og_ref, sfin_ref, st_ref, b_ref, k32_ref):
    s = pl.program_id(2)
    C = CHUNK

    @pl.when(s == 0)
    def _():
        st_ref[...] = jnp.zeros_like(st_ref)

    row = lax.broadcasted_iota(jnp.int32, (C, 1), 0)
    valid = jnp.logical_or(s > 0, row < N_META)

    z = _dot_f32(alr_ref[...], wa2_ref[...]) + ba_ref[...]
    la = jnp.where(valid, _log_sigmoid(z) * (1.0 / GATE_TAU), 0.0)
    kf = jnp.where(valid, k_ref[...].astype(f32), 0.0)
    vb = jnp.where(valid, v_ref[...], jnp.zeros_like(v_ref[...]))
    qf = q_ref[...].astype(f32) * (DK ** -0.5)

    rowi = lax.broadcasted_iota(jnp.int32, (C, C), 0)
    colj = lax.broadcasted_iota(jnp.int32, (C, C), 1)
    tri = (colj <= rowi).astype(bf16)
    b = _dot_sel(tri, la)
    b_ref[...] = b
    k32_ref[...] = kf
    b_last = b_ref[C - 1:C, :]

    st = st_ref[...]
    qe = (qf * jnp.exp(b)).astype(bf16)
    o = _dot(qe, st.astype(bf16), _NT)

    attn = jnp.zeros((C, C), f32)
    for m in LEVELS:
        ref = jnp.concatenate(
            [jnp.broadcast_to(b_ref[(2 * p + 1) * m - 1:(2 * p + 1) * m, :], (2 * m, DK))
             for p in range(C // (2 * m))], axis=0)
        e = jnp.exp(-jnp.abs(b - ref))
        odd = ((row >> _log2(m)) & 1) == 1
        qt = jnp.where(odd, qf * e, 0.0).astype(bf16)
        kt = jnp.where(odd, 0.0, kf * e).astype(bf16)
        a = _dot(qt, kt, _NT)
        attn = attn + jnp.where((rowi >> _log2(2 * m)) == (colj >> _log2(2 * m)), a, 0.0)

    sub = row & (SUB - 1)
    pieces = []
    for j in range(SUB):
        kb = jnp.concatenate(
            [jnp.broadcast_to(k32_ref[SUB * t + j:SUB * t + j + 1, :], (SUB, DK))
             for t in range(C // SUB)], axis=0)
        bb = jnp.concatenate(
            [jnp.broadcast_to(b_ref[SUB * t + j:SUB * t + j + 1, :], (SUB, DK))
             for t in range(C // SUB)], axis=0)
        e = jnp.exp(jnp.where(sub >= j, b - bb, -1e30))
        pieces.append((qf * kb * e).astype(bf16))
    stacked = jnp.concatenate(pieces, axis=0)
    rsum = _dot(stacked, jnp.ones((DK, LANES), bf16))
    for j in range(SUB):
        attn = attn + jnp.where(colj == (rowi & ~(SUB - 1)) + j, rsum[j * C:(j + 1) * C, :], 0.0)

    o = o + _dot(attn.astype(bf16), vb)
    ms = jnp.mean(o * o, axis=-1, keepdims=True)
    on = o * lax.rsqrt(ms + RMS_EPS) * ng_ref[...]
    gf = g_ref[...].astype(f32)
    og_ref[...] = (on * (gf * jax.nn.sigmoid(gf))).astype(bf16)

    kd = (kf * jnp.exp(b_last - b)).astype(bf16)
    st_new = st * jnp.exp(b_last) + _dot(vb, kd, _TN)
    st_ref[...] = st_new

    @pl.when(s == N_CHUNKS)
    def _():
        sfin_ref[0, 0] = st_new.T


def _gla_prompt(proj, alr, wa2p, ba, ng):
    def rb(b, s):
        return jnp.where(s == 0, META_BLK, b * N_CHUNKS + s - 1)

    return pl.pallas_call(
        _gla_prompt_kernel,
        grid=(BATCH, N_HEADS, N_CHUNKS + 1),
        in_specs=[
            pl.BlockSpec((CHUNK, DK), lambda b, h, s: (rb(b, s), COL_Q // DK + h)),
            pl.BlockSpec((CHUNK, DK), lambda b, h, s: (rb(b, s), COL_K // DK + h)),
            pl.BlockSpec((CHUNK, DV), lambda b, h, s: (rb(b, s), COL_V // DV + h)),
            pl.BlockSpec((CHUNK, DV), lambda b, h, s: (rb(b, s), COL_G // DV + h)),
            pl.BlockSpec((CHUNK, LANES), lambda b, h, s: (rb(b, s), 0)),
            pl.BlockSpec((LANES, DK), lambda b, h, s: (0, h)),
            pl.BlockSpec((1, DK), lambda b, h, s: (0, h)),
            pl.BlockSpec((1, DV), lambda b, h, s: (0, 0)),
        ],
        out_specs=[
            pl.BlockSpec((CHUNK, DV), lambda b, h, s: (rb(b, s), h)),
            pl.BlockSpec((1, 1, DK, DV), lambda b, h, s: (b, h, 0, 0)),
        ],
        out_shape=[jax.ShapeDtypeStruct((R_ALL, D_GLA_V), bf16),
                   jax.ShapeDtypeStruct((BATCH, N_HEADS, DK, DV), f32)],
        scratch_shapes=[pltpu.VMEM((DV, DK), f32),
                        pltpu.VMEM((CHUNK, DK), f32),
                        pltpu.VMEM((CHUNK, DK), f32)],
        compiler_params=_params(3),
        name="gla_prompt",
    )(proj, proj, proj, proj, alr, wa2p, ba, ng)


SAMPLE_TB = 16


def _gla_sample_kernel(q_ref, k_ref, v_ref, g_ref, alr_ref, wa2_ref, ba_ref, ng_ref, s0_ref,
                       og_in_ref, og_ref, s_ref):
    del og_in_ref
    TB = SAMPLE_TB
    z = _dot_f32(alr_ref[...], wa2_ref[...]) + ba_ref[...]
    a = jnp.exp(_log_sigmoid(z) * (1.0 / GATE_TAU))
    kb = k_ref[...]
    qb = (q_ref[...].astype(f32) * (DK ** -0.5)).astype(bf16)
    vf = v_ref[...].astype(f32)
    rid = lax.broadcasted_iota(jnp.int32, (TB, LANES), 0)
    outs = []
    for r in range(TB):
        pick = (rid == r).astype(bf16)
        a_col = sum(_dot(p, pick, _TN) for p in _split3(a))
        k_col = _dot(kb, pick, _TN)
        q_col = _dot(qb, pick, _TN)
        a4 = jnp.concatenate([a_col] * (DV // LANES), axis=1)
        k4 = jnp.concatenate([k_col] * (DV // LANES), axis=1)
        q4 = jnp.concatenate([q_col] * (DV // LANES), axis=1)
        s_new = a4 * s0_ref[r, 0] + k4 * vf[r:r + 1, :]
        s_ref[r, 0] = s_new
        outs.append(jnp.sum(q4 * s_new, axis=0, keepdims=True))
    o = jnp.concatenate(outs, axis=0)
    ms = jnp.mean(o * o, axis=-1, keepdims=True)
    on = o * lax.rsqrt(ms + RMS_EPS) * ng_ref[...]
    gf = g_ref[...].astype(f32)
    og_ref[...] = (on * (gf * jax.nn.sigmoid(gf))).astype(bf16)


def _gla_sample(proj, alr, wa2p, ba, ng, s0, og):
    TB = SAMPLE_TB
    r0 = NP // TB
    return pl.pallas_call(
        _gla_sample_kernel,
        grid=(DEC_BATCH // TB, N_HEADS),
        in_specs=[
            pl.BlockSpec((TB, DK), lambda i, h: (r0 + i, COL_Q // DK + h)),
            pl.BlockSpec((TB, DK), lambda i, h: (r0 + i, COL_K // DK + h)),
            pl.BlockSpec((TB, DV), lambda i, h: (r0 + i, COL_V // DV + h)),
            pl.BlockSpec((TB, DV), lambda i, h: (r0 + i, COL_G // DV + h)),
            pl.BlockSpec((TB, LANES), lambda i, h: (r0 + i, 0)),
            pl.BlockSpec((LANES, DK), lambda i, h: (0, h)),
            pl.BlockSpec((1, DK), lambda i, h: (0, h)),
            pl.BlockSpec((1, DV), lambda i, h: (0, 0)),
            pl.BlockSpec((TB, 1, DK, DV), lambda i, h: (i, h, 0, 0)),
            pl.BlockSpec(memory_space=pl.ANY),
        ],
        out_specs=[
            pl.BlockSpec((TB, DV), lambda i, h: (r0 + i, h)),
            pl.BlockSpec((TB, 1, DK, DV), lambda i, h: (i, h, 0, 0)),
        ],
        out_shape=[jax.ShapeDtypeStruct((R_ALL, D_GLA_V), bf16),
                   jax.ShapeDtypeStruct((DEC_BATCH, N_HEADS, DK, DV), f32)],
        input_output_aliases={9: 0},
        compiler_params=_params(2),
        name="gla_sample",
    )(proj, proj, proj, proj, alr, wa2p, ba, ng, s0, og)


CONV_TR, CONV_TC = 512, 512
CONV_PAD = 8


def _conv_prompt_kernel(cb_ref, cc_ref, ch_ref, mc_ref, mh_ref, w_ref, y_ref, nb_ref, u_ref):
    t = pl.program_id(2)
    TR = CONV_TR

    @pl.when(t == 0)
    def _():
        mu = mc_ref[...].astype(f32) * mh_ref[...].astype(f32)
        u_ref[CONV_PAD - 2:CONV_PAD, :] = mu[N_META - 2:N_META, :]

    u = cc_ref[...].astype(f32) * ch_ref[...].astype(f32)
    u_ref[CONV_PAD:CONV_PAD + TR, :] = u
    w = w_ref[...]
    zc = (w[0:1, :] * u_ref[CONV_PAD - 2:CONV_PAD - 2 + TR, :]
          + w[1:2, :] * u_ref[CONV_PAD - 1:CONV_PAD - 1 + TR, :]
          + w[2:3, :] * u)
    y_ref[...] = (cb_ref[...].astype(f32) * zc).astype(bf16)
    last = u[TR - 2:TR, :]
    u_ref[CONV_PAD - 2:CONV_PAD, :] = last

    @pl.when(t == SEQ // TR - 1)
    def _():
        nb_ref[0] = last


def _conv_prompt(proj, conv_w):
    TR, TC = CONV_TR, CONV_TC
    nt = SEQ // TR
    mrow = R_MAIN // N_META
    return pl.pallas_call(
        _conv_prompt_kernel,
        grid=(BATCH, D_CONV // TC, nt),
        in_specs=[
            pl.BlockSpec((TR, TC), lambda b, j, t: (b * nt + t, COL_CB // TC + j)),
            pl.BlockSpec((TR, TC), lambda b, j, t: (b * nt + t, COL_CC // TC + j)),
            pl.BlockSpec((TR, TC), lambda b, j, t: (b * nt + t, COL_CH // TC + j)),
            pl.BlockSpec((N_META, TC), lambda b, j, t: (mrow, COL_CC // TC + j)),
            pl.BlockSpec((N_META, TC), lambda b, j, t: (mrow, COL_CH // TC + j)),
            pl.BlockSpec((CONV_WIDTH, TC), lambda b, j, t: (0, j)),
        ],
        out_specs=[
            pl.BlockSpec((TR, TC), lambda b, j, t: (b * nt + t, j)),
            pl.BlockSpec((1, CONV_WIDTH - 1, TC), lambda b, j, t: (b, 0, j)),
        ],
        out_shape=[jax.ShapeDtypeStruct((R_ALL, D_CONV), bf16),
                   jax.ShapeDtypeStruct((BATCH, CONV_WIDTH - 1, D_CONV), f32)],
        scratch_shapes=[pltpu.VMEM((CONV_PAD + TR, TC), f32)],
        compiler_params=_params(3),
        name="conv_prompt",
    )(proj, proj, proj, proj, proj, conv_w)


def _conv_sample_kernel(cb_ref, cc_ref, ch_ref, buf_ref, w_ref, y_in_ref, y_ref, nb_ref):
    del y_in_ref
    u = cc_ref[...].astype(f32) * ch_ref[...].astype(f32)
    w = w_ref[...]
    b0 = buf_ref[:, 0:D_CONV]
    b1 = buf_ref[:, D_CONV:2 * D_CONV]
    zc = w[0:1, :] * b0 + w[1:2, :] * b1 + w[2:3, :] * u
    y_ref[...] = (cb_ref[...].astype(f32) * zc).astype(bf16)
    nb_ref[:, 0:D_CONV] = b1
    nb_ref[:, D_CONV:2 * D_CONV] = u


def _conv_sample(proj, buf, conv_w, yc):
    rb = NP // DEC_BATCH
    return pl.pallas_call(
        _conv_sample_kernel,
        grid=(1,),
        in_specs=[
            pl.BlockSpec((DEC_BATCH, D_CONV), lambda i: (rb, COL_CB // D_CONV)),
            pl.BlockSpec((DEC_BATCH, D_CONV), lambda i: (rb, COL_CC // D_CONV)),
            pl.BlockSpec((DEC_BATCH, D_CONV), lambda i: (rb, COL_CH // D_CONV)),
            pl.BlockSpec((DEC_BATCH, 2 * D_CONV), lambda i: (0, 0)),
            pl.BlockSpec((CONV_WIDTH, D_CONV), lambda i: (0, 0)),
            pl.BlockSpec(memory_space=pl.ANY),
        ],
        out_specs=[
            pl.BlockSpec((DEC_BATCH, D_CONV), lambda i: (rb, 0)),
            pl.BlockSpec((DEC_BATCH, 2 * D_CONV), lambda i: (0, 0)),
        ],
        out_shape=[jax.ShapeDtypeStruct((R_ALL, D_CONV), bf16),
                   jax.ShapeDtypeStruct((DEC_BATCH, 2 * D_CONV), f32)],
        input_output_aliases={5: 0},
        compiler_params=_params(1),
        name="conv_sample",
    )(proj, proj, proj, buf, conv_w, yc)


OUT_TM, OUT_TN = 832, 1024


def _outproj_kernel(o_ref, y_ref, wo_ref, wy_ref, h_ref, s_ref):
    s_ref[...] = (ALPHA * h_ref[...] + _dot(o_ref[...], wo_ref[...]) + _dot(y_ref[...], wy_ref[...]))


def _outproj(og, yc, w_out, h):
    return pl.pallas_call(
        _outproj_kernel,
        grid=(R_MAIN // OUT_TM, D_MODEL // OUT_TN),
        in_specs=[
            pl.BlockSpec((OUT_TM, D_GLA_V), lambda i, j: (i, 0)),
            pl.BlockSpec((OUT_TM, D_CONV), lambda i, j: (i, 0)),
            pl.BlockSpec((D_GLA_V, OUT_TN), lambda i, j: (0, j)),
            pl.BlockSpec((D_CONV, OUT_TN), lambda i, j: (1, j)),
            pl.BlockSpec((OUT_TM, OUT_TN), lambda i, j: (i, j)),
        ],
        out_specs=pl.BlockSpec((OUT_TM, OUT_TN), lambda i, j: (i, j)),
        out_shape=jax.ShapeDtypeStruct((R_MAIN, D_MODEL), f32),
        compiler_params=_params(2),
        name="outproj",
    )(og, yc, w_out, w_out, h)


LN1_ROWS = 320


def _ln1_kernel(s_ref, g_ref, b_ref, h_ref, hb_ref):
    h = _layer_norm(s_ref[...], g_ref[...], b_ref[...])
    h_ref[...] = h
    hb_ref[...] = h.astype(bf16)


def _ln1(s1, g, b):
    return pl.pallas_call(
        _ln1_kernel,
        grid=(R_MAIN // LN1_ROWS,),
        in_specs=[
            pl.BlockSpec((LN1_ROWS, D_MODEL), lambda i: (i, 0)),
            pl.BlockSpec((1, D_MODEL), lambda i: (0, 0)),
            pl.BlockSpec((1, D_MODEL), lambda i: (0, 0)),
        ],
        out_specs=[
            pl.BlockSpec((LN1_ROWS, D_MODEL), lambda i: (i, 0)),
            pl.BlockSpec((LN1_ROWS, D_MODEL), lambda i: (i, 0)),
        ],
        out_shape=[jax.ShapeDtypeStruct((R_MAIN, D_MODEL), f32),
                   jax.ShapeDtypeStruct((R_MAIN, D_MODEL), bf16)],
        compiler_params=_params(1),
        name="ln1",
    )(s1, g, b)


LN2_ROWS = 128


def _ln2_kernel(s_ref, g_ref, b_ref, yp_ref, ys_ref):
    i = pl.program_id(0)
    y = _layer_norm(s_ref[...], g_ref[...], b_ref[...])

    @pl.when(i < NP // LN2_ROWS)
    def _():
        yp_ref[...] = y

    @pl.when(i >= NP // LN2_ROWS)
    def _():
        ys_ref[...] = y


def _ln2(s2, g, b):
    n_p = NP // LN2_ROWS
    return pl.pallas_call(
        _ln2_kernel,
        grid=(R_MAIN // LN2_ROWS,),
        in_specs=[
            pl.BlockSpec((LN2_ROWS, D_MODEL), lambda i: (i, 0)),
            pl.BlockSpec((1, D_MODEL), lambda i: (0, 0)),
            pl.BlockSpec((1, D_MODEL), lambda i: (0, 0)),
        ],
        out_specs=[
            pl.BlockSpec((LN2_ROWS, D_MODEL), lambda i: (jnp.minimum(i, n_p - 1), 0)),
            pl.BlockSpec((DEC_BATCH, D_MODEL), lambda i: (0, 0)),
        ],
        out_shape=[jax.ShapeDtypeStruct((NP, D_MODEL), f32),
                   jax.ShapeDtypeStruct((DEC_BATCH, D_MODEL), f32)],
        compiler_params=_params(1),
        name="ln2",
    )(s2, g, b)


FFN_TM, FFN_TN = 1664, 256


def _ffn_up_kernel(h_ref, wg_ref, wu_ref, act_ref):
    h = h_ref[...]
    a = _dot(h, wg_ref[...])
    u = _dot(h, wu_ref[...])
    act_ref[...] = (a * jax.nn.sigmoid(a) * u).astype(bf16)


def _ffn_up(hb, wg, wu):
    return pl.pallas_call(
        _ffn_up_kernel,
        grid=(R_MAIN // FFN_TM, D_FF_PAD // FFN_TN),
        in_specs=[
            pl.BlockSpec((FFN_TM, D_MODEL), lambda i, j: (i, 0)),
            pl.BlockSpec((D_MODEL, FFN_TN), lambda i, j: (0, j)),
            pl.BlockSpec((D_MODEL, FFN_TN), lambda i, j: (0, j)),
        ],
        out_specs=pl.BlockSpec((FFN_TM, FFN_TN), lambda i, j: (i, j)),
        out_shape=jax.ShapeDtypeStruct((R_MAIN, D_FF_PAD), bf16),
        compiler_params=_params(2),
        name="ffn_up",
    )(hb, wg, wu)


DOWN_TM, DOWN_TN, DOWN_TK = 1664, 1024, 1024


def _ffn_down_kernel(a_ref, w_ref, h_ref, s_ref):
    k = pl.program_id(2)
    part = _dot(a_ref[...], w_ref[...])

    @pl.when(k == 0)
    def _():
        s_ref[...] = ALPHA * h_ref[...] + part

    @pl.when(k > 0)
    def _():
        s_ref[...] += part


def _ffn_down(act, wd, h1):
    return pl.pallas_call(
        _ffn_down_kernel,
        grid=(R_MAIN // DOWN_TM, D_MODEL // DOWN_TN, D_FF_PAD // DOWN_TK),
        in_specs=[
            pl.BlockSpec((DOWN_TM, DOWN_TK), lambda i, j, k: (i, k)),
            pl.BlockSpec((DOWN_TK, DOWN_TN), lambda i, j, k: (k, j)),
            pl.BlockSpec((DOWN_TM, DOWN_TN), lambda i, j, k: (i, j)),
        ],
        out_specs=pl.BlockSpec((DOWN_TM, DOWN_TN), lambda i, j, k: (i, j)),
        out_shape=jax.ShapeDtypeStruct((R_MAIN, D_MODEL), f32),
        compiler_params=_params(3),
        name="ffn_down",
    )(act, wd, h1)


def kernel(x_prompt, x_sample, state_gla, state_conv, meta_tokens, emb_ln_g, emb_ln_b,
           w_in, w_a2, b_a, gla_norm_g, conv_w, w_out, ln1_g, ln1_b,
           w_ffn_gate, w_ffn_up, w_ffn_down, ln2_g, ln2_b):
    assert x_prompt.shape == (BATCH, SEQ, D_MODEL) and x_sample.shape == (DEC_BATCH, 1, D_MODEL)
    assert w_in.shape[0] == 1, "single layer"
    row = lambda v: v.reshape(1, -1)

    a0 = COL_CB
    wm = jnp.concatenate([w_in[0, :, :a0], w_in[0, :, a0 + GATE_RANK:]], axis=1).astype(bf16)
    wa = jnp.pad(w_in[0, :, a0:a0 + GATE_RANK], ((0, 0), (0, LANES - GATE_RANK))).astype(bf16)
    wa2p = jnp.pad(w_a2[0], ((0, LANES - GATE_RANK), (0, 0)))
    wo = w_out[0].astype(bf16)
    ff_pad = D_FF_PAD - D_FF
    wg = jnp.pad(w_ffn_gate[0], ((0, 0), (0, ff_pad))).astype(bf16)
    wu = jnp.pad(w_ffn_up[0], ((0, 0), (0, ff_pad))).astype(bf16)
    wd = jnp.pad(w_ffn_down[0], ((0, ff_pad), (0, 0))).astype(bf16)

    xp = x_prompt.reshape(NP, D_MODEL)
    xt = jnp.concatenate([x_sample.reshape(DEC_BATCH, D_MODEL), meta_tokens.astype(f32),
                          jnp.zeros((LN_ROWS - DEC_BATCH - N_META, D_MODEL), f32)], axis=0)

    h, hb = _ln0(xp, xt, row(emb_ln_g), row(emb_ln_b))
    proj, alr = _proj(hb, wm, wa)

    ba, ng = row(b_a[0]), row(gla_norm_g[0])
    og, s_p = _gla_prompt(proj, alr, wa2p, ba, ng)
    og, s_s = _gla_sample(proj, alr, wa2p, ba, ng, state_gla[0], og)
    yc, nb_p = _conv_prompt(proj, conv_w[0])
    yc, nb_s = _conv_sample(proj, state_conv[0].reshape(DEC_BATCH, 2 * D_CONV), conv_w[0], yc)

    s1 = _outproj(og, yc, wo, h)
    h1, h1b = _ln1(s1, row(ln1_g[0]), row(ln1_b[0]))
    act = _ffn_up(h1b, wg, wu)
    s2 = _ffn_down(act, wd, h1)
    y_p, y_s = _ln2(s2, row(ln2_g[0]), row(ln2_b[0]))

    return (y_p.reshape(BATCH, SEQ, D_MODEL),
            y_s.reshape(DEC_BATCH, 1, D_MODEL),
            s_p[None],
            nb_p[None],
            s_s[None],
            nb_s.reshape(1, DEC_BATCH, CONV_WIDTH - 1, D_CONV))
```

```python
import functools

import jax
import jax.numpy as jnp
from jax import lax
from jax.experimental import pallas as pl
from jax.experimental.pallas import tpu as pltpu

f32 = jnp.float32
bf16 = jnp.bfloat16

D_MODEL = 4096
BATCH = 4
SEQ = 2048
DEC_BATCH = 128
N_META = 16
D_GLA_V = D_MODEL // 2
D_CONV = D_MODEL - D_GLA_V
D_GLA_K = D_GLA_V // 2
N_HEADS = 4
DK = D_GLA_K // N_HEADS
DV = D_GLA_V // N_HEADS
GATE_RANK = 16
GATE_TAU = 16.0
CONV_WIDTH = 3
D_FF = 11008
ALPHA = 2.0 ** 0.25
LN_EPS = 1e-5
RMS_EPS = 1e-6

LANES = 128
NP = BATCH * SEQ
R_MAIN = NP + DEC_BATCH
R_ALL = R_MAIN + LANES
D_PROJ = 2 * D_GLA_K + 2 * D_GLA_V + 3 * D_CONV
D_FF_PAD = 11264
CHUNK = 128
N_CHUNKS = SEQ // CHUNK
META_BLK = R_MAIN // CHUNK
VMEM_LIMIT = 56 * 1024 * 1024

COL_Q, COL_K, COL_V, COL_G = 0, D_GLA_K, 2 * D_GLA_K, 2 * D_GLA_K + D_GLA_V
COL_CB = COL_G + D_GLA_V
COL_CC = D_CONV
COL_CH = 2 * D_CONV


def _params(n_axes):
    return pltpu.CompilerParams(
        dimension_semantics=("arbitrary",) * n_axes, vmem_limit_bytes=VMEM_LIMIT)


def _layer_norm(x, g, b):
    mu = jnp.mean(x, axis=-1, keepdims=True)
    xc = x - mu
    var = jnp.mean(xc * xc, axis=-1, keepdims=True)
    return xc * lax.rsqrt(var + LN_EPS) * g + b


def _split3(x):
    hi = x.astype(bf16)
    r = x - hi.astype(f32)
    mid = r.astype(bf16)
    lo = (r - mid.astype(f32)).astype(bf16)
    return hi, mid, lo


def _dot(a, b, dims=(((1,), (0,)), ((), ()))):
    return lax.dot_general(a, b, dims, preferred_element_type=f32)


_NT = (((1,), (1,)), ((), ()))
_TN = (((0,), (0,)), ((), ()))


def _dot_sel(sel_bf16, x_f32, dims=(((1,), (0,)), ((), ()))):
    hi, mid, lo = _split3(x_f32)
    return _dot(sel_bf16, hi, dims) + _dot(sel_bf16, mid, dims) + _dot(sel_bf16, lo, dims)


def _dot_f32(a, b):
    ah, am, _ = _split3(a)
    bh, bm, _ = _split3(b)
    return _dot(ah, bh) + _dot(ah, bm) + _dot(am, bh)


def _log2(n):
    assert n & (n - 1) == 0
    return n.bit_length() - 1


def _log_sigmoid(z):
    return jnp.minimum(z, 0.0) - jnp.log1p(jnp.exp(-jnp.abs(z)))


LN_ROWS = 256


def _ln0_kernel(xp_ref, xt_ref, g_ref, b_ref, h_ref, hb_ref):
    i = pl.program_id(0)

    def emit(x):
        h = _layer_norm(x, g_ref[...], b_ref[...])
        h_ref[...] = h
        hb_ref[...] = h.astype(bf16)

    @pl.when(i < NP // LN_ROWS)
    def _():
        emit(xp_ref[...])

    @pl.when(i >= NP // LN_ROWS)
    def _():
        emit(xt_ref[...])


def _ln0(xp, xt, g, b):
    n_p = NP // LN_ROWS
    return pl.pallas_call(
        _ln0_kernel,
        grid=(R_ALL // LN_ROWS,),
        in_specs=[
            pl.BlockSpec((LN_ROWS, D_MODEL), lambda i: (jnp.minimum(i, n_p - 1), 0)),
            pl.BlockSpec((LN_ROWS, D_MODEL), lambda i: (0, 0)),
            pl.BlockSpec((1, D_MODEL), lambda i: (0, 0)),
            pl.BlockSpec((1, D_MODEL), lambda i: (0, 0)),
        ],
        out_specs=[
            pl.BlockSpec((LN_ROWS, D_MODEL), lambda i: (i, 0)),
            pl.BlockSpec((LN_ROWS, D_MODEL), lambda i: (i, 0)),
        ],
        out_shape=[jax.ShapeDtypeStruct((R_ALL, D_MODEL), f32),
                   jax.ShapeDtypeStruct((R_ALL, D_MODEL), bf16)],
        compiler_params=_params(1),
        name="ln0",
    )(xp, xt, g, b)


PROJ_TM, PROJ_TN = 1408, 512
D_PROJ_A = COL_CB
D_PROJ_B = 3 * D_CONV


def _proj_a_kernel(h_ref, w_ref, wa_ref, p_ref, a_ref):
    p_ref[...] = _dot(h_ref[...], w_ref[...].astype(bf16)).astype(bf16)

    @pl.when(pl.program_id(1) == 0)
    def _():
        a_ref[...] = _dot(h_ref[...], wa_ref[...].astype(bf16))


def _proj_a(hb, w_in):
    return pl.pallas_call(
        _proj_a_kernel,
        grid=(R_ALL // PROJ_TM, D_PROJ_A // PROJ_TN),
        in_specs=[
            pl.BlockSpec((PROJ_TM, D_MODEL), lambda i, j: (i, 0)),
            pl.BlockSpec((D_MODEL, PROJ_TN), lambda i, j: (0, j)),
            pl.BlockSpec((D_MODEL, LANES), lambda i, j: (0, D_PROJ_A // LANES)),
        ],
        out_specs=[
            pl.BlockSpec((PROJ_TM, PROJ_TN), lambda i, j: (i, j)),
            pl.BlockSpec((PROJ_TM, LANES), lambda i, j: (i, 0)),
        ],
        out_shape=[jax.ShapeDtypeStruct((R_ALL, D_PROJ_A), bf16),
                   jax.ShapeDtypeStruct((R_ALL, LANES), f32)],
        compiler_params=_params(2),
        name="proj_a",
    )(hb, w_in, w_in)


def _proj_b_kernel(h_ref, w_ref, p_ref):
    p_ref[...] = _dot(h_ref[...], w_ref[...].astype(bf16)).astype(bf16)


def _proj_b(hb, w_conv):
    return pl.pallas_call(
        _proj_b_kernel,
        grid=(R_ALL // PROJ_TM, D_PROJ_B // PROJ_TN),
        in_specs=[
            pl.BlockSpec((PROJ_TM, D_MODEL), lambda i, j: (i, 0)),
            pl.BlockSpec((D_MODEL, PROJ_TN), lambda i, j: (0, j)),
        ],
        out_specs=pl.BlockSpec((PROJ_TM, PROJ_TN), lambda i, j: (i, j)),
        out_shape=jax.ShapeDtypeStruct((R_ALL, D_PROJ_B), bf16),
        compiler_params=_params(2),
        name="proj_b",
    )(hb, w_conv)


SUB = 8
LEVELS = (64, 32, 16, 8)


def _gla_prompt_kernel(q_ref, k_ref, v_ref, g_ref, alr_ref, wa2_ref, ba_ref, ng_ref,
                       og_ref, sfin_ref, st_ref, b_ref, k32_ref):
    s = pl.program_id(2)
    C = CHUNK

    @pl.when(s == 0)
    def _():
        st_ref[...] = jnp.zeros_like(st_ref)

    row = lax.broadcasted_iota(jnp.int32, (C, 1), 0)
    valid = jnp.logical_or(s > 0, row < N_META)

    z = _dot_f32(alr_ref[...], wa2_ref[...]) + ba_ref[...]
    la = jnp.where(valid, _log_sigmoid(z) * (1.0 / GATE_TAU), 0.0)
    kf = jnp.where(valid, k_ref[...].astype(f32), 0.0)
    vb = jnp.where(valid, v_ref[...], jnp.zeros_like(v_ref[...]))
    qf = q_ref[...].astype(f32) * (DK ** -0.5)

    rowi = lax.broadcasted_iota(jnp.int32, (C, C), 0)
    colj = lax.broadcasted_iota(jnp.int32, (C, C), 1)
    tri = (colj <= rowi).astype(bf16)
    b = _dot_sel(tri, la)
    b_ref[...] = b
    k32_ref[...] = kf
    b_last = b_ref[C - 1:C, :]

    st = st_ref[...]
    qe = (qf * jnp.exp(b)).astype(bf16)
    o = _dot(qe, st.astype(bf16), _NT)

    attn = jnp.zeros((C, C), f32)
    for m in LEVELS:
        ref = jnp.concatenate(
            [jnp.broadcast_to(b_ref[(2 * p + 1) * m - 1:(2 * p + 1) * m, :], (2 * m, DK))
             for p in range(C // (2 * m))], axis=0)
        e = jnp.exp(-jnp.abs(b - ref))
        odd = ((row >> _log2(m)) & 1) == 1
        qt = jnp.where(odd, qf * e, 0.0).astype(bf16)
        kt = jnp.where(odd, 0.0, kf * e).astype(bf16)
        a = _dot(qt, kt, _NT)
        attn = attn + jnp.where((rowi >> _log2(2 * m)) == (colj >> _log2(2 * m)), a, 0.0)

    sub = row & (SUB - 1)
    pieces = []
    for j in range(SUB):
        kb = jnp.concatenate(
            [jnp.broadcast_to(k32_ref[SUB * t + j:SUB * t + j + 1, :], (SUB, DK))
             for t in range(C // SUB)], axis=0)
        bb = jnp.concatenate(
            [jnp.broadcast_to(b_ref[SUB * t + j:SUB * t + j + 1, :], (SUB, DK))
             for t in range(C // SUB)], axis=0)
        e = jnp.exp(jnp.where(sub >= j, b - bb, -1e30))
        pieces.append((qf * kb * e).astype(bf16))
    stacked = jnp.concatenate(pieces, axis=0)
    rsum = _dot(stacked, jnp.ones((DK, LANES), bf16))
    for j in range(SUB):
        attn = attn + jnp.where(colj == (rowi & ~(SUB - 1)) + j, rsum[j * C:(j + 1) * C, :], 0.0)

    o = o + _dot(attn.astype(bf16), vb)
    ms = jnp.mean(o * o, axis=-1, keepdims=True)
    on = o * lax.rsqrt(ms + RMS_EPS) * ng_ref[...]
    gf = g_ref[...].astype(f32)
    og_ref[...] = (on * (gf * jax.nn.sigmoid(gf))).astype(bf16)

    kd = (kf * jnp.exp(b_last - b)).astype(bf16)
    st_new = st * jnp.exp(b_last) + _dot(vb, kd, _TN)
    st_ref[...] = st_new

    @pl.when(s == N_CHUNKS)
    def _():
        sfin_ref[0, 0] = st_new.T


def _gla_prompt(proj, alr, wa2p, ba, ng):
    def rb(b, s):
        return jnp.where(s == 0, META_BLK, b * N_CHUNKS + s - 1)

    return pl.pallas_call(
        _gla_prompt_kernel,
        grid=(BATCH, N_HEADS, N_CHUNKS + 1),
        in_specs=[
            pl.BlockSpec((CHUNK, DK), lambda b, h, s: (rb(b, s), COL_Q // DK + h)),
            pl.BlockSpec((CHUNK, DK), lambda b, h, s: (rb(b, s), COL_K // DK + h)),
            pl.BlockSpec((CHUNK, DV), lambda b, h, s: (rb(b, s), COL_V // DV + h)),
            pl.BlockSpec((CHUNK, DV), lambda b, h, s: (rb(b, s), COL_G // DV + h)),
            pl.BlockSpec((CHUNK, LANES), lambda b, h, s: (rb(b, s), 0)),
            pl.BlockSpec((LANES, DK), lambda b, h, s: (0, h)),
            pl.BlockSpec((1, DK), lambda b, h, s: (0, h)),
            pl.BlockSpec((1, DV), lambda b, h, s: (0, 0)),
        ],
        out_specs=[
            pl.BlockSpec((CHUNK, DV), lambda b, h, s: (b * N_CHUNKS + jnp.maximum(s - 1, 0), h)),
            pl.BlockSpec((1, 1, DK, DV), lambda b, h, s: (b, h, 0, 0)),
        ],
        out_shape=[jax.ShapeDtypeStruct((R_ALL, D_GLA_V), bf16),
                   jax.ShapeDtypeStruct((BATCH, N_HEADS, DK, DV), f32)],
        scratch_shapes=[pltpu.VMEM((DV, DK), f32),
                        pltpu.VMEM((CHUNK, DK), f32),
                        pltpu.VMEM((CHUNK, DK), f32)],
        compiler_params=_params(3),
        name="gla_prompt",
    )(proj, proj, proj, proj, alr, wa2p, ba, ng)


SAMPLE_TB = 16


def _gla_sample_kernel(q_ref, k_ref, v_ref, g_ref, alr_ref, wa2_ref, ba_ref, ng_ref, s0_ref,
                       og_in_ref, og_ref, s_ref):
    del og_in_ref
    TB = SAMPLE_TB
    z = _dot_f32(alr_ref[...], wa2_ref[...]) + ba_ref[...]
    a = jnp.exp(_log_sigmoid(z) * (1.0 / GATE_TAU))
    kb = k_ref[...]
    qb = (q_ref[...].astype(f32) * (DK ** -0.5)).astype(bf16)
    vf = v_ref[...].astype(f32)
    rid = lax.broadcasted_iota(jnp.int32, (TB, LANES), 0)
    outs = []
    for r in range(TB):
        pick = (rid == r).astype(bf16)
        a_col = sum(_dot(p, pick, _TN) for p in _split3(a))
        k_col = _dot(kb, pick, _TN)
        q_col = _dot(qb, pick, _TN)
        a4 = jnp.concatenate([a_col] * (DV // LANES), axis=1)
        k4 = jnp.concatenate([k_col] * (DV // LANES), axis=1)
        q4 = jnp.concatenate([q_col] * (DV // LANES), axis=1)
        s_new = a4 * s0_ref[r, 0] + k4 * vf[r:r + 1, :]
        s_ref[r, 0] = s_new
        outs.append(jnp.sum(q4 * s_new, axis=0, keepdims=True))
    o = jnp.concatenate(outs, axis=0)
    ms = jnp.mean(o * o, axis=-1, keepdims=True)
    on = o * lax.rsqrt(ms + RMS_EPS) * ng_ref[...]
    gf = g_ref[...].astype(f32)
    og_ref[...] = (on * (gf * jax.nn.sigmoid(gf))).astype(bf16)


def _gla_sample(proj, alr, wa2p, ba, ng, s0, og):
    TB = SAMPLE_TB
    r0 = NP // TB
    return pl.pallas_call(
        _gla_sample_kernel,
        grid=(DEC_BATCH // TB, N_HEADS),
        in_specs=[
            pl.BlockSpec((TB, DK), lambda i, h: (r0 + i, COL_Q // DK + h)),
            pl.BlockSpec((TB, DK), lambda i, h: (r0 + i, COL_K // DK + h)),
            pl.BlockSpec((TB, DV), lambda i, h: (r0 + i, COL_V // DV + h)),
            pl.BlockSpec((TB, DV), lambda i, h: (r0 + i, COL_G // DV + h)),
            pl.BlockSpec((TB, LANES), lambda i, h: (r0 + i, 0)),
            pl.BlockSpec((LANES, DK), lambda i, h: (0, h)),
            pl.BlockSpec((1, DK), lambda i, h: (0, h)),
            pl.BlockSpec((1, DV), lambda i, h: (0, 0)),
            pl.BlockSpec((TB, 1, DK, DV), lambda i, h: (i, h, 0, 0)),
            pl.BlockSpec(memory_space=pl.ANY),
        ],
        out_specs=[
            pl.BlockSpec((TB, DV), lambda i, h: (r0 + i, h)),
            pl.BlockSpec((TB, 1, DK, DV), lambda i, h: (i, h, 0, 0)),
        ],
        out_shape=[jax.ShapeDtypeStruct((R_ALL, D_GLA_V), bf16),
                   jax.ShapeDtypeStruct((DEC_BATCH, N_HEADS, DK, DV), f32)],
        input_output_aliases={9: 0},
        compiler_params=_params(2),
        name="gla_sample",
    )(proj, proj, proj, proj, alr, wa2p, ba, ng, s0, og)


CONV_TR, CONV_TC = 512, 512
CONV_PAD = 8


def _conv_prompt_kernel(cb_ref, cc_ref, ch_ref, mc_ref, mh_ref, w_ref, y_ref, nb_ref, u_ref):
    t = pl.program_id(2)
    TR = CONV_TR

    @pl.when(t == 0)
    def _():
        mu = mc_ref[...].astype(f32) * mh_ref[...].astype(f32)
        u_ref[CONV_PAD - 2:CONV_PAD, :] = mu[N_META - 2:N_META, :]

    u = cc_ref[...].astype(f32) * ch_ref[...].astype(f32)
    u_ref[CONV_PAD:CONV_PAD + TR, :] = u
    w = w_ref[...]
    zc = (w[0:1, :] * u_ref[CONV_PAD - 2:CONV_PAD - 2 + TR, :]
          + w[1:2, :] * u_ref[CONV_PAD - 1:CONV_PAD - 1 + TR, :]
          + w[2:3, :] * u)
    y_ref[...] = (cb_ref[...].astype(f32) * zc).astype(bf16)
    last = u[TR - 2:TR, :]
    u_ref[CONV_PAD - 2:CONV_PAD, :] = last

    @pl.when(t == SEQ // TR - 1)
    def _():
        nb_ref[0] = last


def _conv_prompt(proj, conv_w):
    TR, TC = CONV_TR, CONV_TC
    nt = SEQ // TR
    mrow = R_MAIN // N_META
    return pl.pallas_call(
        _conv_prompt_kernel,
        grid=(BATCH, D_CONV // TC, nt),
        in_specs=[
            pl.BlockSpec((TR, TC), lambda b, j, t: (b * nt + t, j)),
            pl.BlockSpec((TR, TC), lambda b, j, t: (b * nt + t, COL_CC // TC + j)),
            pl.BlockSpec((TR, TC), lambda b, j, t: (b * nt + t, COL_CH // TC + j)),
            pl.BlockSpec((N_META, TC), lambda b, j, t: (mrow, COL_CC // TC + j)),
            pl.BlockSpec((N_META, TC), lambda b, j, t: (mrow, COL_CH // TC + j)),
            pl.BlockSpec((CONV_WIDTH, TC), lambda b, j, t: (0, j)),
        ],
        out_specs=[
            pl.BlockSpec((TR, TC), lambda b, j, t: (b * nt + t, j)),
            pl.BlockSpec((1, CONV_WIDTH - 1, TC), lambda b, j, t: (b, 0, j)),
        ],
        out_shape=[jax.ShapeDtypeStruct((R_ALL, D_CONV), bf16),
                   jax.ShapeDtypeStruct((BATCH, CONV_WIDTH - 1, D_CONV), f32)],
        scratch_shapes=[pltpu.VMEM((CONV_PAD + TR, TC), f32)],
        compiler_params=_params(3),
        name="conv_prompt",
    )(proj, proj, proj, proj, proj, conv_w)


def _conv_sample_kernel(cb_ref, cc_ref, ch_ref, buf_ref, w_ref, y_in_ref, y_ref, nb_ref):
    del y_in_ref
    u = cc_ref[...].astype(f32) * ch_ref[...].astype(f32)
    w = w_ref[...]
    b0 = buf_ref[:, 0:D_CONV]
    b1 = buf_ref[:, D_CONV:2 * D_CONV]
    zc = w[0:1, :] * b0 + w[1:2, :] * b1 + w[2:3, :] * u
    y_ref[...] = (cb_ref[...].astype(f32) * zc).astype(bf16)
    nb_ref[:, 0:D_CONV] = b1
    nb_ref[:, D_CONV:2 * D_CONV] = u


def _conv_sample(proj, buf, conv_w, yc):
    rb = NP // DEC_BATCH
    return pl.pallas_call(
        _conv_sample_kernel,
        grid=(1,),
        in_specs=[
            pl.BlockSpec((DEC_BATCH, D_CONV), lambda i: (rb, 0)),
            pl.BlockSpec((DEC_BATCH, D_CONV), lambda i: (rb, COL_CC // D_CONV)),
            pl.BlockSpec((DEC_BATCH, D_CONV), lambda i: (rb, COL_CH // D_CONV)),
            pl.BlockSpec((DEC_BATCH, 2 * D_CONV), lambda i: (0, 0)),
            pl.BlockSpec((CONV_WIDTH, D_CONV), lambda i: (0, 0)),
            pl.BlockSpec(memory_space=pl.ANY),
        ],
        out_specs=[
            pl.BlockSpec((DEC_BATCH, D_CONV), lambda i: (rb, 0)),
            pl.BlockSpec((DEC_BATCH, 2 * D_CONV), lambda i: (0, 0)),
        ],
        out_shape=[jax.ShapeDtypeStruct((R_ALL, D_CONV), bf16),
                   jax.ShapeDtypeStruct((DEC_BATCH, 2 * D_CONV), f32)],
        input_output_aliases={5: 0},
        compiler_params=_params(1),
        name="conv_sample",
    )(proj, proj, proj, buf, conv_w, yc)


OUT_TM, OUT_TN = 1664, 256


def _outproj_kernel(o_ref, y_ref, wo_ref, wy_ref, h_ref, s_ref):
    s_ref[...] = (ALPHA * h_ref[...] + _dot(o_ref[...], wo_ref[...].astype(bf16))
                  + _dot(y_ref[...], wy_ref[...].astype(bf16)))


def _outproj(og, yc, w_out, h):
    return pl.pallas_call(
        _outproj_kernel,
        grid=(R_MAIN // OUT_TM, D_MODEL // OUT_TN),
        in_specs=[
            pl.BlockSpec((OUT_TM, D_GLA_V), lambda i, j: (i, 0)),
            pl.BlockSpec((OUT_TM, D_CONV), lambda i, j: (i, 0)),
            pl.BlockSpec((D_GLA_V, OUT_TN), lambda i, j: (0, j)),
            pl.BlockSpec((D_CONV, OUT_TN), lambda i, j: (1, j)),
            pl.BlockSpec((OUT_TM, OUT_TN), lambda i, j: (i, j)),
        ],
        out_specs=pl.BlockSpec((OUT_TM, OUT_TN), lambda i, j: (i, j)),
        out_shape=jax.ShapeDtypeStruct((R_MAIN, D_MODEL), f32),
        compiler_params=_params(2),
        name="outproj",
    )(og, yc, w_out, w_out, h)


LN1_ROWS = 320


def _ln1_kernel(s_ref, g_ref, b_ref, h_ref, hb_ref):
    h = _layer_norm(s_ref[...], g_ref[...], b_ref[...])
    h_ref[...] = h
    hb_ref[...] = h.astype(bf16)


def _ln1(s1, g, b):
    return pl.pallas_call(
        _ln1_kernel,
        grid=(R_MAIN // LN1_ROWS,),
        in_specs=[
            pl.BlockSpec((LN1_ROWS, D_MODEL), lambda i: (i, 0)),
            pl.BlockSpec((1, D_MODEL), lambda i: (0, 0)),
            pl.BlockSpec((1, D_MODEL), lambda i: (0, 0)),
        ],
        out_specs=[
            pl.BlockSpec((LN1_ROWS, D_MODEL), lambda i: (i, 0)),
            pl.BlockSpec((LN1_ROWS, D_MODEL), lambda i: (i, 0)),
        ],
        out_shape=[jax.ShapeDtypeStruct((R_MAIN, D_MODEL), f32),
                   jax.ShapeDtypeStruct((R_MAIN, D_MODEL), bf16)],
        compiler_params=_params(1),
        name="ln1",
    )(s1, g, b)


LN2_ROWS = 128


def _ln2_kernel(s_ref, g_ref, b_ref, yp_ref, ys_ref):
    i = pl.program_id(0)
    y = _layer_norm(s_ref[...], g_ref[...], b_ref[...])

    @pl.when(i < NP // LN2_ROWS)
    def _():
        yp_ref[...] = y

    @pl.when(i >= NP // LN2_ROWS)
    def _():
        ys_ref[...] = y


def _ln2(s2, g, b):
    n_p = NP // LN2_ROWS
    return pl.pallas_call(
        _ln2_kernel,
        grid=(R_MAIN // LN2_ROWS,),
        in_specs=[
            pl.BlockSpec((LN2_ROWS, D_MODEL), lambda i: (i, 0)),
            pl.BlockSpec((1, D_MODEL), lambda i: (0, 0)),
            pl.BlockSpec((1, D_MODEL), lambda i: (0, 0)),
        ],
        out_specs=[
            pl.BlockSpec((LN2_ROWS, D_MODEL), lambda i: (jnp.minimum(i, n_p - 1), 0)),
            pl.BlockSpec((DEC_BATCH, D_MODEL), lambda i: (0, 0)),
        ],
        out_shape=[jax.ShapeDtypeStruct((NP, D_MODEL), f32),
                   jax.ShapeDtypeStruct((DEC_BATCH, D_MODEL), f32)],
        compiler_params=_params(1),
        name="ln2",
    )(s2, g, b)


FFN_TM, FFN_TN = 1664, 256


FFN_NT = D_FF // FFN_TN


def _ffn_up_kernel(h_ref, wg_ref, wu_ref, wd_ref, act_ref, wdb_ref):
    i = pl.program_id(0)
    j = pl.program_id(1)

    @pl.when(j < FFN_NT)
    def _():
        h = h_ref[...]
        a = _dot(h, wg_ref[...].astype(bf16))
        u = _dot(h, wu_ref[...].astype(bf16))
        act_ref[...] = (a * jax.nn.sigmoid(a) * u).astype(bf16)

    @pl.when(j >= FFN_NT)
    def _():
        act_ref[...] = jnp.zeros_like(act_ref)

    @pl.when(jnp.logical_and(i == 0, j < FFN_NT))
    def _():
        wdb_ref[...] = wd_ref[...].astype(bf16)

    @pl.when(jnp.logical_and(i == 0, j >= FFN_NT))
    def _():
        wdb_ref[...] = jnp.zeros_like(wdb_ref)


def _ffn_up(hb, wg, wu, wd):
    nt_pad = D_FF_PAD // FFN_TN
    jw = lambda j: jnp.minimum(j, FFN_NT - 1)
    return pl.pallas_call(
        _ffn_up_kernel,
        grid=(R_MAIN // FFN_TM, nt_pad),
        in_specs=[
            pl.BlockSpec((FFN_TM, D_MODEL), lambda i, j: (i, 0), pipeline_mode=pl.Buffered(1)),
            pl.BlockSpec((D_MODEL, FFN_TN), lambda i, j: (0, jw(j))),
            pl.BlockSpec((D_MODEL, FFN_TN), lambda i, j: (0, jw(j))),
            pl.BlockSpec((FFN_TN, D_MODEL), lambda i, j: (jnp.where(i == 0, jw(j), FFN_NT - 1), 0)),
        ],
        out_specs=[
            pl.BlockSpec((FFN_TM, FFN_TN), lambda i, j: (i, j)),
            pl.BlockSpec((FFN_TN, D_MODEL), lambda i, j: (jnp.where(i == 0, j, nt_pad - 1), 0)),
        ],
        out_shape=[jax.ShapeDtypeStruct((R_MAIN, D_FF_PAD), bf16),
                   jax.ShapeDtypeStruct((D_FF_PAD, D_MODEL), bf16)],
        compiler_params=_params(2),
        name="ffn_up",
    )(hb, wg, wu, wd)


DOWN_TM, DOWN_TN, DOWN_TK = 1664, 1024, 1024


def _ffn_down_kernel(a_ref, w_ref, h_ref, s_ref):
    @pl.when(pl.program_id(2) == 0)
    def _():
        s_ref[...] = ALPHA * h_ref[...]

    s_ref[...] += _dot(a_ref[...], w_ref[...])


def _ffn_down(act, wd, h1):
    return pl.pallas_call(
        _ffn_down_kernel,
        grid=(R_MAIN // DOWN_TM, D_MODEL // DOWN_TN, D_FF_PAD // DOWN_TK),
        in_specs=[
            pl.BlockSpec((DOWN_TM, DOWN_TK), lambda i, j, k: (i, k)),
            pl.BlockSpec((DOWN_TK, DOWN_TN), lambda i, j, k: (k, j)),
            pl.BlockSpec((DOWN_TM, DOWN_TN), lambda i, j, k: (i, j)),
        ],
        out_specs=pl.BlockSpec((DOWN_TM, DOWN_TN), lambda i, j, k: (i, j)),
        out_shape=jax.ShapeDtypeStruct((R_MAIN, D_MODEL), f32),
        compiler_params=_params(3),
        name="ffn_down",
    )(act, wd, h1)


def kernel(x_prompt, x_sample, state_gla, state_conv, meta_tokens, emb_ln_g, emb_ln_b,
           w_in, w_a2, b_a, gla_norm_g, conv_w, w_out, ln1_g, ln1_b,
           w_ffn_gate, w_ffn_up, w_ffn_down, ln2_g, ln2_b):
    assert x_prompt.shape == (BATCH, SEQ, D_MODEL) and x_sample.shape == (DEC_BATCH, 1, D_MODEL)
    assert w_in.shape[0] == 1, "single layer"
    row = lambda v: v.reshape(1, -1)

    w_conv = w_in[0, :, COL_CB + GATE_RANK:]
    wa2p = jnp.pad(w_a2[0], ((0, LANES - GATE_RANK), (0, 0)))

    xp = x_prompt.reshape(NP, D_MODEL)
    xt = jnp.concatenate([x_sample.reshape(DEC_BATCH, D_MODEL), meta_tokens.astype(f32),
                          jnp.zeros((LN_ROWS - DEC_BATCH - N_META, D_MODEL), f32)], axis=0)

    h, hb = _ln0(xp, xt, row(emb_ln_g), row(emb_ln_b))
    proj, alr = _proj_a(hb, w_in[0])
    proj_c = _proj_b(hb, w_conv)

    ba, ng = row(b_a[0]), row(gla_norm_g[0])
    og, s_p = _gla_prompt(proj, alr, wa2p, ba, ng)
    og, s_s = _gla_sample(proj, alr, wa2p, ba, ng, state_gla[0], og)
    yc, nb_p = _conv_prompt(proj_c, conv_w[0])
    yc, nb_s = _conv_sample(proj_c, state_conv[0].reshape(DEC_BATCH, 2 * D_CONV), conv_w[0], yc)

    s1 = _outproj(og, yc, w_out[0], h)
    h1, h1b = _ln1(s1, row(ln1_g[0]), row(ln1_b[0]))
    act, wd = _ffn_up(h1b, w_ffn_gate[0], w_ffn_up[0], w_ffn_down[0])
    s2 = _ffn_down(act, wd, h1)
    y_p, y_s = _ln2(s2, row(ln2_g[0]), row(ln2_b[0]))

    return (y_p.reshape(BATCH, SEQ, D_MODEL),
            y_s.reshape(DEC_BATCH, 1, D_MODEL),
            s_p[None],
            nb_p[None],
            s_s[None],
            nb_s.reshape(1, DEC_BATCH, CONV_WIDTH - 1, D_CONV))
```

```python
import jax
import jax.numpy as jnp
import numpy as np
from jax import lax
from jax.experimental import pallas as pl
from jax.experimental.pallas import tpu as pltpu

f32 = jnp.float32
bf16 = jnp.bfloat16

D_MODEL = 4096
BATCH = 4
SEQ = 2048
DEC_BATCH = 128
N_META = 16
D_GLA_V = D_MODEL // 2
D_CONV = D_MODEL - D_GLA_V
D_GLA_K = D_GLA_V // 2
N_HEADS = 4
DK = D_GLA_K // N_HEADS
DV = D_GLA_V // N_HEADS
GATE_RANK = 16
GATE_TAU = 16.0
CONV_WIDTH = 3
D_FF = 11008
ALPHA = 2.0 ** 0.25
LN_EPS = 1e-5
RMS_EPS = 1e-6

LANES = 128
NP = BATCH * SEQ
R_MAIN = NP + DEC_BATCH
R_ALL = R_MAIN + LANES
D_PROJ = 2 * D_GLA_K + 2 * D_GLA_V + 3 * D_CONV
D_FF_PAD = 11264
CHUNK = 128
N_CHUNKS = SEQ // CHUNK
VMEM_LIMIT = 56 * 1024 * 1024

COL_Q, COL_K, COL_V, COL_G = 0, D_GLA_K, 2 * D_GLA_K, 2 * D_GLA_K + D_GLA_V
COL_CB = COL_G + D_GLA_V
COL_CC = COL_CB + D_CONV
COL_CH = COL_CC + D_CONV


def _params(n_axes):
    return pltpu.CompilerParams(
        dimension_semantics=("arbitrary",) * n_axes, vmem_limit_bytes=VMEM_LIMIT)


def _layer_norm(x, g, b):
    mu = jnp.mean(x, axis=-1, keepdims=True)
    xc = x - mu
    var = jnp.mean(xc * xc, axis=-1, keepdims=True)
    return xc * lax.rsqrt(var + LN_EPS) * g + b


def _split3(x):
    hi = x.astype(bf16)
    r = x - hi.astype(f32)
    mid = r.astype(bf16)
    lo = (r - mid.astype(f32)).astype(bf16)
    return hi, mid, lo


def _dot(a, b, dims=(((1,), (0,)), ((), ()))):
    return lax.dot_general(a, b, dims, preferred_element_type=f32)


_NT = (((1,), (1,)), ((), ()))
_TN = (((0,), (0,)), ((), ()))


def _dot_sel(sel_bf16, x_f32, dims=(((1,), (0,)), ((), ()))):
    hi, mid, lo = _split3(x_f32)
    return _dot(sel_bf16, hi, dims) + _dot(sel_bf16, mid, dims) + _dot(sel_bf16, lo, dims)


def _dot_f32(a, b):
    ah, am, _ = _split3(a)
    bh, bm, _ = _split3(b)
    return _dot(ah, bh) + _dot(ah, bm) + _dot(am, bh)


def _log2(n):
    assert n & (n - 1) == 0
    return n.bit_length() - 1


def _log_sigmoid(z):
    return jnp.minimum(z, 0.0) - jnp.log(1.0 + jnp.exp(-jnp.abs(z)))


LN_ROWS = 256


def _ln0_kernel(xp_ref, xt_ref, g_ref, b_ref, h_ref, hb_ref):
    i = pl.program_id(0)

    def emit(x):
        h = _layer_norm(x, g_ref[...], b_ref[...])
        h_ref[...] = h
        hb_ref[...] = h.astype(bf16)

    @pl.when(i < NP // LN_ROWS)
    def _():
        emit(xp_ref[...])

    @pl.when(i >= NP // LN_ROWS)
    def _():
        emit(xt_ref[...])


def _ln0(xp, xt, g, b):
    n_p = NP // LN_ROWS
    return pl.pallas_call(
        _ln0_kernel,
        grid=(R_ALL // LN_ROWS,),
        in_specs=[
            pl.BlockSpec((LN_ROWS, D_MODEL), lambda i: (jnp.minimum(i, n_p - 1), 0)),
            pl.BlockSpec((LN_ROWS, D_MODEL), lambda i: (0, 0)),
            pl.BlockSpec((1, D_MODEL), lambda i: (0, 0)),
            pl.BlockSpec((1, D_MODEL), lambda i: (0, 0)),
        ],
        out_specs=[
            pl.BlockSpec((LN_ROWS, D_MODEL), lambda i: (i, 0)),
            pl.BlockSpec((LN_ROWS, D_MODEL), lambda i: (i, 0)),
        ],
        out_shape=[jax.ShapeDtypeStruct((R_ALL, D_MODEL), f32),
                   jax.ShapeDtypeStruct((R_ALL, D_MODEL), bf16)],
        compiler_params=_params(1),
        name="ln0",
    )(xp, xt, g, b)


PROJ_TM, PROJ_TN = 1408, 512


def _proj_kernel(h_ref, w_ref, wa_ref, p_ref, a_ref):
    p_ref[...] = _dot(h_ref[...], w_ref[...].astype(bf16), _NT).astype(bf16)

    @pl.when(pl.program_id(1) == 0)
    def _():
        a_ref[...] = _dot(h_ref[...], wa_ref[...].astype(bf16), _NT)


def _proj(hb, w_in_t):
    def w_row(j):
        return pl.multiple_of(j * PROJ_TN + jnp.where(j * PROJ_TN >= COL_CB, GATE_RANK, 0), GATE_RANK)

    return pl.pallas_call(
        _proj_kernel,
        grid=(R_ALL // PROJ_TM, D_PROJ // PROJ_TN),
        in_specs=[
            pl.BlockSpec((PROJ_TM, D_MODEL), lambda i, j: (i, 0)),
            pl.BlockSpec((pl.Element(PROJ_TN), pl.Element(D_MODEL)), lambda i, j: (w_row(j), 0)),
            pl.BlockSpec((LANES, D_MODEL), lambda i, j: (COL_CB // LANES, 0)),
        ],
        out_specs=[
            pl.BlockSpec((PROJ_TM, PROJ_TN), lambda i, j: (i, j)),
            pl.BlockSpec((PROJ_TM, LANES), lambda i, j: (i, 0)),
        ],
        out_shape=[jax.ShapeDtypeStruct((R_ALL, D_PROJ), bf16),
                   jax.ShapeDtypeStruct((R_ALL, LANES), f32)],
        compiler_params=_params(2),
        name="proj",
    )(hb, w_in_t, w_in_t)


GLA_BLK = 128
GLA_HPS = 4
LVL_DIAG = _log2(GLA_BLK)


def _gla_levels(c):
    return [c >> (t + 1) for t in range(_log2(c))]


def _gla_tables(c):
    idx = np.arange(c)
    tri = (idx[None, :] <= idx[:, None]).astype(np.float32)
    mats = [tri]
    for m in _gla_levels(c):
        ref = ((idx // m) | 1) * m - 1
        mats.append(tri - tri[ref])
    diff = np.concatenate(mats, axis=0)
    diff = np.concatenate([diff, diff], axis=1)
    t = np.arange(min(c, GLA_BLK))
    level = np.full((t.size, t.size), -1, np.int32)
    for m in _gla_levels(t.size):
        same_pair = (t[:, None] // (2 * m)) == (t[None, :] // (2 * m))
        odd_even = (((t[:, None] // m) & 1) == 1) & (((t[None, :] // m) & 1) == 0)
        level[same_pair & odd_even] = _log2(m)
    level[t[:, None] == t[None, :]] = LVL_DIAG
    return jnp.asarray(diff, bf16), jnp.asarray(level)


def _gate_log_decay(alr, wa2, ba):
    return _log_sigmoid(_dot_f32(alr, wa2) + ba) * (1.0 / GATE_TAU)


def _decay_sums(la, diff_ref, x_ref):
    hi = la.astype(bf16)
    mid = (la - hi.astype(f32)).astype(bf16)
    x_ref[...] = _dot(diff_ref[...], jnp.concatenate([hi, mid], axis=0))


def _gla_prompt_kernel(q_ref, k_ref, v_ref, g_ref, alr_ref, wa2_ref, ba_ref, ng_ref, s0_ref,
                       diff_ref, lvl_ref, og_ref, sfin_ref, st_ref, x_ref):
    s = pl.program_id(2)

    @pl.when(s == 0)
    def _():
        st_ref[...] = s0_ref[...]

    for hh in range(GLA_HPS):
        _gla_prompt_head(hh, q_ref, k_ref, v_ref, g_ref, alr_ref, wa2_ref, ba_ref, ng_ref,
                         diff_ref, lvl_ref, og_ref, st_ref, x_ref.at[hh])

    @pl.when(s == N_CHUNKS - 1)
    def _():
        for hh in range(GLA_HPS):
            sfin_ref[0, hh] = st_ref[hh].T


def _gla_prompt_head(hh, q_ref, k_ref, v_ref, g_ref, alr_ref, wa2_ref, ba_ref, ng_ref,
                     diff_ref, lvl_ref, og_ref, st_ref, x_ref):
    C = CHUNK
    nblk = C // GLA_BLK
    kcols = slice(hh * DK, (hh + 1) * DK)
    vcols = slice(hh * DV, (hh + 1) * DV)

    la = _gate_log_decay(alr_ref[...], wa2_ref[:, kcols], ba_ref[:, kcols])
    _decay_sums(la, diff_ref, x_ref)
    kb = k_ref[:, kcols]
    kf = kb.astype(f32)
    vb = v_ref[:, vcols]
    qf = q_ref[:, kcols].astype(f32) * (DK ** -0.5)
    qb = qf.astype(bf16)
    b = x_ref[0:C, :]
    b_last = x_ref[C - 1:C, :]

    st = st_ref[hh]
    qe = (qf * jnp.exp(b)).astype(bf16)
    o_state = _dot(qe, st.astype(bf16), _NT)

    lvl = lvl_ref[...]
    blk_rows = lambda a, i: a[i * GLA_BLK:(i + 1) * GLA_BLK]
    tiles = [jnp.where(lvl == LVL_DIAG, _dot(blk_rows(qb, i), blk_rows(kb, i), _NT), 0.0)
             for i in range(nblk)]
    cross = None
    row = lax.broadcasted_iota(jnp.int32, (C, 1), 0)
    for t, m in enumerate(_gla_levels(C)):
        e = jnp.exp(-jnp.abs(x_ref[(t + 1) * C:(t + 2) * C, :]))
        if m == GLA_BLK:
            qt = (blk_rows(qf, 1) * blk_rows(e, 1)).astype(bf16)
            kt = (blk_rows(kf, 0) * blk_rows(e, 0)).astype(bf16)
            cross = _dot(qt, kt, _NT)
            continue
        if m >= 8:
            n = C // (2 * m)
            e3, q3, k3 = (a.reshape(n, 2 * m, DK) for a in (e, qf, kf))
            zero = jnp.zeros((n, m, DK), f32)
            qt = jnp.concatenate([zero, q3[:, m:, :] * e3[:, m:, :]], axis=1).reshape(C, DK)
            kt = jnp.concatenate([k3[:, :m, :] * e3[:, :m, :], zero], axis=1).reshape(C, DK)
        else:
            odd = ((row >> _log2(m)) & 1) == 1
            qt = jnp.where(odd, qf * e, 0.0)
            kt = jnp.where(odd, 0.0, kf * e)
        qt, kt = qt.astype(bf16), kt.astype(bf16)
        tiles = [jnp.where(lvl == _log2(m), _dot(blk_rows(qt, i), blk_rows(kt, i), _NT), tiles[i])
                 for i in range(nblk)]

    outs = []
    for i in range(nblk):
        lhs = tiles[i] if i == 0 else jnp.concatenate([cross, tiles[i]], axis=1)
        outs.append(_dot(lhs.astype(bf16), vb[0:(i + 1) * GLA_BLK]))
    o = o_state + jnp.concatenate(outs, axis=0)
    ms = jnp.mean(o * o, axis=-1, keepdims=True)
    on = o * lax.rsqrt(ms + RMS_EPS) * ng_ref[...]
    gf = g_ref[:, vcols].astype(f32)
    og_ref[:, vcols] = (on * (gf * jax.nn.sigmoid(gf))).astype(bf16)

    kd = (kf * jnp.exp(b_last - b)).astype(bf16)
    st_new = st * jnp.exp(b_last) + _dot(vb, kd, _TN)
    st_ref[hh] = st_new


def _gla_prompt(proj, alr, wa2p, ba, ng, s_meta):
    assert CHUNK in (GLA_BLK, 2 * GLA_BLK) and N_HEADS % GLA_HPS == 0
    diff, level = _gla_tables(CHUNK)
    hps = GLA_HPS
    rb = lambda b, s: b * N_CHUNKS + s
    const = lambda b, h, s: (0, 0)
    return pl.pallas_call(
        _gla_prompt_kernel,
        grid=(BATCH, N_HEADS // hps, N_CHUNKS),
        in_specs=[
            pl.BlockSpec((CHUNK, hps * DK), lambda b, h, s: (rb(b, s), COL_Q // (hps * DK) + h)),
            pl.BlockSpec((CHUNK, hps * DK), lambda b, h, s: (rb(b, s), COL_K // (hps * DK) + h)),
            pl.BlockSpec((CHUNK, hps * DV), lambda b, h, s: (rb(b, s), COL_V // (hps * DV) + h)),
            pl.BlockSpec((CHUNK, hps * DV), lambda b, h, s: (rb(b, s), COL_G // (hps * DV) + h)),
            pl.BlockSpec((CHUNK, LANES), lambda b, h, s: (rb(b, s), 0)),
            pl.BlockSpec((LANES, hps * DK), lambda b, h, s: (0, h)),
            pl.BlockSpec((1, hps * DK), lambda b, h, s: (0, h)),
            pl.BlockSpec((1, DV), const),
            pl.BlockSpec((hps, DV, DK), lambda b, h, s: (h, 0, 0)),
            pl.BlockSpec(diff.shape, const),
            pl.BlockSpec(level.shape, const),
        ],
        out_specs=[
            pl.BlockSpec((CHUNK, hps * DV), lambda b, h, s: (rb(b, s), h)),
            pl.BlockSpec((1, hps, DK, DV), lambda b, h, s: (b, h, 0, 0)),
        ],
        out_shape=[jax.ShapeDtypeStruct((R_ALL, D_GLA_V), bf16),
                   jax.ShapeDtypeStruct((BATCH, N_HEADS, DK, DV), f32)],
        scratch_shapes=[pltpu.VMEM((hps, DV, DK), f32),
                        pltpu.VMEM((hps, diff.shape[0], DK), f32)],
        compiler_params=_params(3),
        name="gla_prompt",
    )(proj, proj, proj, proj, alr, wa2p, ba, ng, s_meta, diff, level)


def _gla_meta_kernel(k_ref, v_ref, alr_ref, wa2_ref, ba_ref, diff_ref, s_ref, x_ref):
    la = _gate_log_decay(alr_ref[...], wa2_ref[...], ba_ref[...])
    _decay_sums(la, diff_ref, x_ref)
    kd = (k_ref[...].astype(f32) * jnp.exp(x_ref[N_META - 1:N_META, :] - x_ref[...])).astype(bf16)
    s_ref[0] = _dot(v_ref[...], kd, _TN)


def _gla_meta(proj, alr, wa2p, ba):
    diff = _gla_tables(N_META)[0][:N_META]
    mb = R_MAIN // N_META
    return pl.pallas_call(
        _gla_meta_kernel,
        grid=(N_HEADS,),
        in_specs=[
            pl.BlockSpec((N_META, DK), lambda h: (mb, COL_K // DK + h)),
            pl.BlockSpec((N_META, DV), lambda h: (mb, COL_V // DV + h)),
            pl.BlockSpec((N_META, LANES), lambda h: (mb, 0)),
            pl.BlockSpec((LANES, DK), lambda h: (0, h)),
            pl.BlockSpec((1, DK), lambda h: (0, h)),
            pl.BlockSpec(diff.shape, lambda h: (0, 0)),
        ],
        out_specs=pl.BlockSpec((1, DV, DK), lambda h: (h, 0, 0)),
        out_shape=jax.ShapeDtypeStruct((N_HEADS, DV, DK), f32),
        scratch_shapes=[pltpu.VMEM((N_META, DK), f32)],
        compiler_params=_params(1),
        name="gla_meta",
    )(proj, proj, alr, wa2p, ba, diff)


SAMPLE_TB = 16


def _gla_sample_kernel(q_ref, k_ref, v_ref, g_ref, alr_ref, wa2_ref, ba_ref, ng_ref, s0_ref,
                       og_in_ref, og_ref, s_ref):
    del og_in_ref
    TB = SAMPLE_TB
    z = _dot_f32(alr_ref[...], wa2_ref[...]) + ba_ref[...]
    a = jnp.exp(_log_sigmoid(z) * (1.0 / GATE_TAU))
    kb = k_ref[...]
    qb = (q_ref[...].astype(f32) * (DK ** -0.5)).astype(bf16)
    vf = v_ref[...].astype(f32)
    rid = lax.broadcasted_iota(jnp.int32, (TB, LANES), 0)
    outs = []
    for r in range(TB):
        pick = (rid == r).astype(bf16)
        a_col = sum(_dot(p, pick, _TN) for p in _split3(a))
        k_col = _dot(kb, pick, _TN)
        q_col = _dot(qb, pick, _TN)
        a4 = jnp.concatenate([a_col] * (DV // LANES), axis=1)
        k4 = jnp.concatenate([k_col] * (DV // LANES), axis=1)
        q4 = jnp.concatenate([q_col] * (DV // LANES), axis=1)
        s_new = a4 * s0_ref[r, 0] + k4 * vf[r:r + 1, :]
        s_ref[r, 0] = s_new
        outs.append(jnp.sum(q4 * s_new, axis=0, keepdims=True))
    o = jnp.concatenate(outs, axis=0)
    ms = jnp.mean(o * o, axis=-1, keepdims=True)
    on = o * lax.rsqrt(ms + RMS_EPS) * ng_ref[...]
    gf = g_ref[...].astype(f32)
    og_ref[...] = (on * (gf * jax.nn.sigmoid(gf))).astype(bf16)


def _gla_sample(proj, alr, wa2p, ba, ng, s0, og):
    TB = SAMPLE_TB
    r0 = NP // TB
    return pl.pallas_call(
        _gla_sample_kernel,
        grid=(DEC_BATCH // TB, N_HEADS),
        in_specs=[
            pl.BlockSpec((TB, DK), lambda i, h: (r0 + i, COL_Q // DK + h)),
            pl.BlockSpec((TB, DK), lambda i, h: (r0 + i, COL_K // DK + h)),
            pl.BlockSpec((TB, DV), lambda i, h: (r0 + i, COL_V // DV + h)),
            pl.BlockSpec((TB, DV), lambda i, h: (r0 + i, COL_G // DV + h)),
            pl.BlockSpec((TB, LANES), lambda i, h: (r0 + i, 0)),
            pl.BlockSpec((LANES, DK), lambda i, h: (0, h)),
            pl.BlockSpec((1, DK), lambda i, h: (0, h)),
            pl.BlockSpec((1, DV), lambda i, h: (0, 0)),
            pl.BlockSpec((TB, 1, DK, DV), lambda i, h: (i, h, 0, 0)),
            pl.BlockSpec(memory_space=pl.ANY),
        ],
        out_specs=[
            pl.BlockSpec((TB, DV), lambda i, h: (r0 + i, h)),
            pl.BlockSpec((TB, 1, DK, DV), lambda i, h: (i, h, 0, 0)),
        ],
        out_shape=[jax.ShapeDtypeStruct((R_ALL, D_GLA_V), bf16),
                   jax.ShapeDtypeStruct((DEC_BATCH, N_HEADS, DK, DV), f32)],
        input_output_aliases={9: 0},
        compiler_params=_params(2),
        name="gla_sample",
    )(proj, proj, proj, proj, alr, wa2p, ba, ng, s0, og)


CONV_TR, CONV_TC = 512, 512
CONV_PAD = 8


def _conv_prompt_kernel(cb_ref, cc_ref, ch_ref, mc_ref, mh_ref, w_ref, y_ref, nb_ref, u_ref):
    t = pl.program_id(2)
    TR = CONV_TR

    @pl.when(t == 0)
    def _():
        mu = mc_ref[...].astype(f32) * mh_ref[...].astype(f32)
        u_ref[CONV_PAD - 2:CONV_PAD, :] = mu[N_META - 2:N_META, :]

    u = cc_ref[...].astype(f32) * ch_ref[...].astype(f32)
    u_ref[CONV_PAD:CONV_PAD + TR, :] = u
    w = w_ref[...]
    zc = (w[0:1, :] * u_ref[CONV_PAD - 2:CONV_PAD - 2 + TR, :]
          + w[1:2, :] * u_ref[CONV_PAD - 1:CONV_PAD - 1 + TR, :]
          + w[2:3, :] * u)
    y_ref[...] = (cb_ref[...].astype(f32) * zc).astype(bf16)
    last = u[TR - 2:TR, :]
    u_ref[CONV_PAD - 2:CONV_PAD, :] = last

    @pl.when(t == SEQ // TR - 1)
    def _():
        nb_ref[0] = last


def _conv_prompt(proj, conv_w):
    TR, TC = CONV_TR, CONV_TC
    nt = SEQ // TR
    mrow = R_MAIN // N_META
    return pl.pallas_call(
        _conv_prompt_kernel,
        grid=(BATCH, D_CONV // TC, nt),
        in_specs=[
            pl.BlockSpec((TR, TC), lambda b, j, t: (b * nt + t, COL_CB // TC + j)),
            pl.BlockSpec((TR, TC), lambda b, j, t: (b * nt + t, COL_CC // TC + j)),
            pl.BlockSpec((TR, TC), lambda b, j, t: (b * nt + t, COL_CH // TC + j)),
            pl.BlockSpec((N_META, TC), lambda b, j, t: (mrow, COL_CC // TC + j)),
            pl.BlockSpec((N_META, TC), lambda b, j, t: (mrow, COL_CH // TC + j)),
            pl.BlockSpec((CONV_WIDTH, TC), lambda b, j, t: (0, j)),
        ],
        out_specs=[
            pl.BlockSpec((TR, TC), lambda b, j, t: (b * nt + t, j)),
            pl.BlockSpec((1, CONV_WIDTH - 1, TC), lambda b, j, t: (b, 0, j)),
        ],
        out_shape=[jax.ShapeDtypeStruct((R_ALL, D_CONV), bf16),
                   jax.ShapeDtypeStruct((BATCH, CONV_WIDTH - 1, D_CONV), f32)],
        scratch_shapes=[pltpu.VMEM((CONV_PAD + TR, TC), f32)],
        compiler_params=_params(3),
        name="conv_prompt",
    )(proj, proj, proj, proj, proj, conv_w)


def _conv_sample_kernel(cb_ref, cc_ref, ch_ref, buf_ref, w_ref, y_in_ref, y_ref, nb_ref):
    del y_in_ref
    u = cc_ref[...].astype(f32) * ch_ref[...].astype(f32)
    w = w_ref[...]
    b0 = buf_ref[:, 0:D_CONV]
    b1 = buf_ref[:, D_CONV:2 * D_CONV]
    zc = w[0:1, :] * b0 + w[1:2, :] * b1 + w[2:3, :] * u
    y_ref[...] = (cb_ref[...].astype(f32) * zc).astype(bf16)
    nb_ref[:, 0:D_CONV] = b1
    nb_ref[:, D_CONV:2 * D_CONV] = u


def _conv_sample(proj, buf, conv_w, yc):
    rb = NP // DEC_BATCH
    return pl.pallas_call(
        _conv_sample_kernel,
        grid=(1,),
        in_specs=[
            pl.BlockSpec((DEC_BATCH, D_CONV), lambda i: (rb, COL_CB // D_CONV)),
            pl.BlockSpec((DEC_BATCH, D_CONV), lambda i: (rb, COL_CC // D_CONV)),
            pl.BlockSpec((DEC_BATCH, D_CONV), lambda i: (rb, COL_CH // D_CONV)),
            pl.BlockSpec((DEC_BATCH, 2 * D_CONV), lambda i: (0, 0)),
            pl.BlockSpec((CONV_WIDTH, D_CONV), lambda i: (0, 0)),
            pl.BlockSpec(memory_space=pl.ANY),
        ],
        out_specs=[
            pl.BlockSpec((DEC_BATCH, D_CONV), lambda i: (rb, 0)),
            pl.BlockSpec((DEC_BATCH, 2 * D_CONV), lambda i: (0, 0)),
        ],
        out_shape=[jax.ShapeDtypeStruct((R_ALL, D_CONV), bf16),
                   jax.ShapeDtypeStruct((DEC_BATCH, 2 * D_CONV), f32)],
        input_output_aliases={5: 0},
        compiler_params=_params(1),
        name="conv_sample",
    )(proj, proj, proj, buf, conv_w, yc)


OUT_TM, OUT_TN = 1664, 256


def _outproj_kernel(o_ref, y_ref, wo_ref, wy_ref, h_ref, s_ref):
    s_ref[...] = (ALPHA * h_ref[...] + _dot(o_ref[...], wo_ref[...].astype(bf16))
                  + _dot(y_ref[...], wy_ref[...].astype(bf16)))


def _outproj(og, yc, w_out, h):
    return pl.pallas_call(
        _outproj_kernel,
        grid=(R_MAIN // OUT_TM, D_MODEL // OUT_TN),
        in_specs=[
            pl.BlockSpec((OUT_TM, D_GLA_V), lambda i, j: (i, 0)),
            pl.BlockSpec((OUT_TM, D_CONV), lambda i, j: (i, 0)),
            pl.BlockSpec((D_GLA_V, OUT_TN), lambda i, j: (0, j)),
            pl.BlockSpec((D_CONV, OUT_TN), lambda i, j: (1, j)),
            pl.BlockSpec((OUT_TM, OUT_TN), lambda i, j: (i, j)),
        ],
        out_specs=pl.BlockSpec((OUT_TM, OUT_TN), lambda i, j: (i, j)),
        out_shape=jax.ShapeDtypeStruct((R_MAIN, D_MODEL), f32),
        compiler_params=_params(2),
        name="outproj",
    )(og, yc, w_out, w_out, h)


LN1_ROWS = 320


def _ln1_kernel(s_ref, g_ref, b_ref, h_ref, hb_ref):
    h = _layer_norm(s_ref[...], g_ref[...], b_ref[...])
    h_ref[...] = h
    hb_ref[...] = h.astype(bf16)


def _ln1(s1, g, b):
    return pl.pallas_call(
        _ln1_kernel,
        grid=(R_MAIN // LN1_ROWS,),
        in_specs=[
            pl.BlockSpec((LN1_ROWS, D_MODEL), lambda i: (i, 0)),
            pl.BlockSpec((1, D_MODEL), lambda i: (0, 0)),
            pl.BlockSpec((1, D_MODEL), lambda i: (0, 0)),
        ],
        out_specs=[
            pl.BlockSpec((LN1_ROWS, D_MODEL), lambda i: (i, 0)),
            pl.BlockSpec((LN1_ROWS, D_MODEL), lambda i: (i, 0)),
        ],
        out_shape=[jax.ShapeDtypeStruct((R_MAIN, D_MODEL), f32),
                   jax.ShapeDtypeStruct((R_MAIN, D_MODEL), bf16)],
        compiler_params=_params(1),
        name="ln1",
    )(s1, g, b)


LN2_ROWS = 128


def _ln2_kernel(s_ref, g_ref, b_ref, yp_ref, ys_ref):
    i = pl.program_id(0)
    y = _layer_norm(s_ref[...], g_ref[...], b_ref[...])

    @pl.when(i < NP // LN2_ROWS)
    def _():
        yp_ref[...] = y

    @pl.when(i >= NP // LN2_ROWS)
    def _():
        ys_ref[...] = y


def _ln2(s2, g, b):
    n_p = NP // LN2_ROWS
    return pl.pallas_call(
        _ln2_kernel,
        grid=(R_MAIN // LN2_ROWS,),
        in_specs=[
            pl.BlockSpec((LN2_ROWS, D_MODEL), lambda i: (i, 0)),
            pl.BlockSpec((1, D_MODEL), lambda i: (0, 0)),
            pl.BlockSpec((1, D_MODEL), lambda i: (0, 0)),
        ],
        out_specs=[
            pl.BlockSpec((LN2_ROWS, D_MODEL), lambda i: (jnp.minimum(i, n_p - 1), 0)),
            pl.BlockSpec((DEC_BATCH, D_MODEL), lambda i: (0, 0)),
        ],
        out_shape=[jax.ShapeDtypeStruct((NP, D_MODEL), f32),
                   jax.ShapeDtypeStruct((DEC_BATCH, D_MODEL), f32)],
        compiler_params=_params(1),
        name="ln2",
    )(s2, g, b)


FFN_TM, FFN_TN = 1664, 256


FFN_NT = D_FF // FFN_TN


def _ffn_up_kernel(h_ref, wg_ref, wu_ref, wd_ref, act_ref, wdb_ref):
    i = pl.program_id(0)
    j = pl.program_id(1)

    @pl.when(j < FFN_NT)
    def _():
        h = h_ref[...]
        a = _dot(h, wg_ref[...].astype(bf16))
        u = _dot(h, wu_ref[...].astype(bf16))
        act_ref[...] = (a * jax.nn.sigmoid(a) * u).astype(bf16)

    @pl.when(j >= FFN_NT)
    def _():
        act_ref[...] = jnp.zeros_like(act_ref)

    @pl.when(jnp.logical_and(i == 0, j < FFN_NT))
    def _():
        wdb_ref[...] = wd_ref[...].astype(bf16)

    @pl.when(jnp.logical_and(i == 0, j >= FFN_NT))
    def _():
        wdb_ref[...] = jnp.zeros_like(wdb_ref)


def _ffn_up(hb, wg, wu, wd):
    nt_pad = D_FF_PAD // FFN_TN
    jw = lambda j: jnp.minimum(j, FFN_NT - 1)
    return pl.pallas_call(
        _ffn_up_kernel,
        grid=(R_MAIN // FFN_TM, nt_pad),
        in_specs=[
            pl.BlockSpec((FFN_TM, D_MODEL), lambda i, j: (i, 0), pipeline_mode=pl.Buffered(1)),
            pl.BlockSpec((D_MODEL, FFN_TN), lambda i, j: (0, jw(j))),
            pl.BlockSpec((D_MODEL, FFN_TN), lambda i, j: (0, jw(j))),
            pl.BlockSpec((FFN_TN, D_MODEL), lambda i, j: (jnp.where(i == 0, jw(j), FFN_NT - 1), 0)),
        ],
        out_specs=[
            pl.BlockSpec((FFN_TM, FFN_TN), lambda i, j: (i, j)),
            pl.BlockSpec((FFN_TN, D_MODEL), lambda i, j: (jnp.where(i == 0, j, nt_pad - 1), 0)),
        ],
        out_shape=[jax.ShapeDtypeStruct((R_MAIN, D_FF_PAD), bf16),
                   jax.ShapeDtypeStruct((D_FF_PAD, D_MODEL), bf16)],
        compiler_params=_params(2),
        name="ffn_up",
    )(hb, wg, wu, wd)


DOWN_TM, DOWN_TN, DOWN_TK = 1664, 1024, 1024


def _ffn_down_kernel(a_ref, w_ref, h_ref, s_ref):
    @pl.when(pl.program_id(2) == 0)
    def _():
        s_ref[...] = ALPHA * h_ref[...]

    s_ref[...] += _dot(a_ref[...], w_ref[...])


def _ffn_down(act, wd, h1):
    return pl.pallas_call(
        _ffn_down_kernel,
        grid=(R_MAIN // DOWN_TM, D_MODEL // DOWN_TN, D_FF_PAD // DOWN_TK),
        in_specs=[
            pl.BlockSpec((DOWN_TM, DOWN_TK), lambda i, j, k: (i, k)),
            pl.BlockSpec((DOWN_TK, DOWN_TN), lambda i, j, k: (k, j)),
            pl.BlockSpec((DOWN_TM, DOWN_TN), lambda i, j, k: (i, j)),
        ],
        out_specs=pl.BlockSpec((DOWN_TM, DOWN_TN), lambda i, j, k: (i, j)),
        out_shape=jax.ShapeDtypeStruct((R_MAIN, D_MODEL), f32),
        compiler_params=_params(3),
        name="ffn_down",
    )(act, wd, h1)


def kernel(x_prompt, x_sample, state_gla, state_conv, meta_tokens, emb_ln_g, emb_ln_b,
           w_in, w_a2, b_a, gla_norm_g, conv_w, w_out, ln1_g, ln1_b,
           w_ffn_gate, w_ffn_up, w_ffn_down, ln2_g, ln2_b):
    assert x_prompt.shape == (BATCH, SEQ, D_MODEL) and x_sample.shape == (DEC_BATCH, 1, D_MODEL)
    assert w_in.shape[0] == 1, "single layer"
    row = lambda v: v.reshape(1, -1)

    w_in_t = jnp.transpose(w_in[0])
    wa2p = jnp.pad(w_a2[0], ((0, LANES - GATE_RANK), (0, 0)))

    xp = x_prompt.reshape(NP, D_MODEL)
    xt = jnp.concatenate([x_sample.reshape(DEC_BATCH, D_MODEL), meta_tokens.astype(f32),
                          jnp.zeros((LN_ROWS - DEC_BATCH - N_META, D_MODEL), f32)], axis=0)

    h, hb = _ln0(xp, xt, row(emb_ln_g), row(emb_ln_b))
    proj, alr = _proj(hb, w_in_t)

    ba, ng = row(b_a[0]), row(gla_norm_g[0])
    s_meta = _gla_meta(proj, alr, wa2p, ba)
    og, s_p = _gla_prompt(proj, alr, wa2p, ba, ng, s_meta)
    og, s_s = _gla_sample(proj, alr, wa2p, ba, ng, state_gla[0], og)
    yc, nb_p = _conv_prompt(proj, conv_w[0])
    yc, nb_s = _conv_sample(proj, state_conv[0].reshape(DEC_BATCH, 2 * D_CONV), conv_w[0], yc)

    s1 = _outproj(og, yc, w_out[0], h)
    h1, h1b = _ln1(s1, row(ln1_g[0]), row(ln1_b[0]))
    act, wd = _ffn_up(h1b, w_ffn_gate[0], w_ffn_up[0], w_ffn_down[0])
    s2 = _ffn_down(act, wd, h1)
    y_p, y_s = _ln2(s2, row(ln2_g[0]), row(ln2_b[0]))

    return (y_p.reshape(BATCH, SEQ, D_MODEL),
            y_s.reshape(DEC_BATCH, 1, D_MODEL),
            s_p[None],
            nb_p[None],
            s_s[None],
            nb_s.reshape(1, DEC_BATCH, CONV_WIDTH - 1, D_CONV))
```

```python
import jax
import jax.numpy as jnp
import numpy as np
from jax import lax
from jax.experimental import pallas as pl
from jax.experimental.pallas import tpu as pltpu

f32 = jnp.float32
bf16 = jnp.bfloat16

D_MODEL = 4096
BATCH = 4
SEQ = 2048
DEC_BATCH = 128
N_META = 16
D_GLA_V = D_MODEL // 2
D_CONV = D_MODEL - D_GLA_V
D_GLA_K = D_GLA_V // 2
N_HEADS = 4
DK = D_GLA_K // N_HEADS
DV = D_GLA_V // N_HEADS
GATE_RANK = 16
GATE_TAU = 16.0
CONV_WIDTH = 3
D_FF = 11008
ALPHA = 2.0 ** 0.25
LN_EPS = 1e-5
RMS_EPS = 1e-6

LANES = 128
NP = BATCH * SEQ
R_MAIN = NP + DEC_BATCH
R_ALL = R_MAIN + LANES
D_PROJ = 2 * D_GLA_K + 2 * D_GLA_V + 3 * D_CONV
D_FF_PAD = 11264
CHUNK = 128
N_CHUNKS = SEQ // CHUNK
VMEM_LIMIT = 56 * 1024 * 1024

COL_Q, COL_K, COL_V, COL_G = 0, D_GLA_K, 2 * D_GLA_K, 2 * D_GLA_K + D_GLA_V
COL_CB = COL_G + D_GLA_V
COL_CC = COL_CB + D_CONV
COL_CH = COL_CC + D_CONV


def _params(n_axes):
    return pltpu.CompilerParams(
        dimension_semantics=("arbitrary",) * n_axes, vmem_limit_bytes=VMEM_LIMIT)


def _layer_norm(x, g, b):
    mu = jnp.mean(x, axis=-1, keepdims=True)
    xc = x - mu
    var = jnp.mean(xc * xc, axis=-1, keepdims=True)
    return xc * lax.rsqrt(var + LN_EPS) * g + b


def _split3(x):
    hi = x.astype(bf16)
    r = x - hi.astype(f32)
    mid = r.astype(bf16)
    lo = (r - mid.astype(f32)).astype(bf16)
    return hi, mid, lo


def _dot(a, b, dims=(((1,), (0,)), ((), ()))):
    return lax.dot_general(a, b, dims, preferred_element_type=f32)


_NT = (((1,), (1,)), ((), ()))
_TN = (((0,), (0,)), ((), ()))


def _dot_sel(sel_bf16, x_f32, dims=(((1,), (0,)), ((), ()))):
    hi, mid, lo = _split3(x_f32)
    return _dot(sel_bf16, hi, dims) + _dot(sel_bf16, mid, dims) + _dot(sel_bf16, lo, dims)


def _dot_f32(a, b):
    ah, am, _ = _split3(a)
    bh, bm, _ = _split3(b)
    return _dot(ah, bh) + _dot(ah, bm) + _dot(am, bh)


def _log2(n):
    assert n & (n - 1) == 0
    return n.bit_length() - 1


def _log_sigmoid(z):
    return jnp.minimum(z, 0.0) - jnp.log(1.0 + jnp.exp(-jnp.abs(z)))


LN_ROWS = 256


def _ln0_kernel(xp_ref, xt_ref, g_ref, b_ref, h_ref, hb_ref):
    i = pl.program_id(0)

    def emit(x):
        h = _layer_norm(x, g_ref[...], b_ref[...])
        h_ref[...] = h
        hb_ref[...] = h.astype(bf16)

    @pl.when(i < NP // LN_ROWS)
    def _():
        emit(xp_ref[...])

    @pl.when(i >= NP // LN_ROWS)
    def _():
        emit(xt_ref[...])


def _ln0(xp, xt, g, b):
    n_p = NP // LN_ROWS
    return pl.pallas_call(
        _ln0_kernel,
        grid=(R_ALL // LN_ROWS,),
        in_specs=[
            pl.BlockSpec((LN_ROWS, D_MODEL), lambda i: (jnp.minimum(i, n_p - 1), 0)),
            pl.BlockSpec((LN_ROWS, D_MODEL), lambda i: (0, 0)),
            pl.BlockSpec((1, D_MODEL), lambda i: (0, 0)),
            pl.BlockSpec((1, D_MODEL), lambda i: (0, 0)),
        ],
        out_specs=[
            pl.BlockSpec((LN_ROWS, D_MODEL), lambda i: (i, 0)),
            pl.BlockSpec((LN_ROWS, D_MODEL), lambda i: (i, 0)),
        ],
        out_shape=[jax.ShapeDtypeStruct((R_ALL, D_MODEL), f32),
                   jax.ShapeDtypeStruct((R_ALL, D_MODEL), bf16)],
        compiler_params=_params(1),
        name="ln0",
    )(xp, xt, g, b)


PROJ_TM, PROJ_TN = 1408, 512


LA_W = 2 * DK


def _proj_kernel(h_ref, w_ref, wa_ref, wa2_ref, ba_ref, p_ref, la_ref, alr_ref):
    j = pl.program_id(1)

    @pl.when(j == 0)
    def _():
        alr_ref[...] = _dot(h_ref[...], wa_ref[...].astype(bf16), _NT)

    @pl.when(j < N_HEADS)
    def _():
        p_ref[...] = _dot(h_ref[...], w_ref[...].astype(bf16), _NT).astype(bf16)
        la = _gate_log_decay(alr_ref[...], wa2_ref[...], ba_ref[...])
        hi = la.astype(bf16)
        la_ref[:, 0:DK] = hi
        la_ref[:, DK:LA_W] = (la - hi.astype(f32)).astype(bf16)

    is_g = jnp.logical_and(j * PROJ_TN >= COL_G, j * PROJ_TN < COL_CB)

    @pl.when(is_g)
    def _():
        p = _dot(h_ref[...], w_ref[...].astype(bf16), _NT)
        p_ref[...] = (p * jax.nn.sigmoid(p)).astype(bf16)

    @pl.when(jnp.logical_and(j >= N_HEADS, jnp.logical_not(is_g)))
    def _():
        p_ref[...] = _dot(h_ref[...], w_ref[...].astype(bf16), _NT).astype(bf16)


def _proj(hb, w_in_t, wa2p, ba):
    assert COL_G >= N_HEADS * PROJ_TN

    def w_row(j):
        return pl.multiple_of(j * PROJ_TN + jnp.where(j * PROJ_TN >= COL_CB, GATE_RANK, 0), GATE_RANK)

    jh = lambda j: jnp.minimum(j, N_HEADS - 1)
    return pl.pallas_call(
        _proj_kernel,
        grid=(R_ALL // PROJ_TM, D_PROJ // PROJ_TN),
        in_specs=[
            pl.BlockSpec((PROJ_TM, D_MODEL), lambda i, j: (i, 0)),
            pl.BlockSpec((pl.Element(PROJ_TN), pl.Element(D_MODEL)), lambda i, j: (w_row(j), 0)),
            pl.BlockSpec((LANES, D_MODEL), lambda i, j: (COL_CB // LANES, 0)),
            pl.BlockSpec((LANES, DK), lambda i, j: (0, jh(j))),
            pl.BlockSpec((1, DK), lambda i, j: (0, jh(j))),
        ],
        out_specs=[
            pl.BlockSpec((PROJ_TM, PROJ_TN), lambda i, j: (i, j)),
            pl.BlockSpec((PROJ_TM, LA_W), lambda i, j: (i, jh(j))),
        ],
        out_shape=[jax.ShapeDtypeStruct((R_ALL, D_PROJ), bf16),
                   jax.ShapeDtypeStruct((R_ALL, N_HEADS * LA_W), bf16)],
        scratch_shapes=[pltpu.VMEM((PROJ_TM, LANES), f32)],
        compiler_params=_params(2),
        name="proj",
    )(hb, w_in_t, w_in_t, wa2p, ba)


GLA_BLK = 128
GLA_HPS = 4
LVL_DIAG = _log2(GLA_BLK)


def _gla_levels(c):
    return [c >> (t + 1) for t in range(_log2(c))]


def _gla_tables(c):
    idx = np.arange(c)
    tri = (idx[None, :] <= idx[:, None]).astype(np.float32)
    mats = [tri]
    for m in _gla_levels(c):
        ref = ((idx // m) | 1) * m - 1
        mats.append(tri - tri[ref])
    diff = np.concatenate(mats, axis=0)
    diff = np.concatenate([diff, diff], axis=1)
    t = np.arange(min(c, GLA_BLK))
    level = np.full((t.size, t.size), -1, np.int32)
    for m in _gla_levels(t.size):
        same_pair = (t[:, None] // (2 * m)) == (t[None, :] // (2 * m))
        odd_even = (((t[:, None] // m) & 1) == 1) & (((t[None, :] // m) & 1) == 0)
        level[same_pair & odd_even] = _log2(m)
    level[t[:, None] == t[None, :]] = LVL_DIAG
    return jnp.asarray(diff, bf16), jnp.asarray(level)


def _gate_log_decay(alr, wa2, ba):
    return _log_sigmoid(_dot_f32(alr, wa2) + ba) * (1.0 / GATE_TAU)


def _decay_sums(la2, diff_ref, x_ref):
    x_ref[...] = _dot(diff_ref[...], jnp.concatenate([la2[:, 0:DK], la2[:, DK:LA_W]], axis=0))


def _gla_prompt_kernel(q_ref, k_ref, v_ref, g_ref, la_ref, ng_ref, s0_ref,
                       diff_ref, lvl_ref, og_ref, sfin_ref, st_ref, x_ref):
    s = pl.program_id(2)

    @pl.when(s == 0)
    def _():
        st_ref[...] = s0_ref[...]

    for hh in range(GLA_HPS):
        _gla_prompt_head(hh, q_ref, k_ref, v_ref, g_ref, la_ref, ng_ref,
                         diff_ref, lvl_ref, og_ref, st_ref, x_ref.at[hh])

    @pl.when(s == N_CHUNKS - 1)
    def _():
        for hh in range(GLA_HPS):
            sfin_ref[0, hh] = st_ref[hh].T


def _gla_prompt_head(hh, q_ref, k_ref, v_ref, g_ref, la_ref, ng_ref,
                     diff_ref, lvl_ref, og_ref, st_ref, x_ref):
    C = CHUNK
    nblk = C // GLA_BLK
    kcols = slice(hh * DK, (hh + 1) * DK)
    vcols = slice(hh * DV, (hh + 1) * DV)

    _decay_sums(la_ref[:, hh * LA_W:(hh + 1) * LA_W], diff_ref, x_ref)
    kb = k_ref[:, kcols]
    kf = kb.astype(f32)
    vb = v_ref[:, vcols]
    qf = q_ref[:, kcols].astype(f32) * (DK ** -0.5)
    qb = qf.astype(bf16)
    b = x_ref[0:C, :]
    b_last = x_ref[C - 1:C, :]

    st = st_ref[hh]
    qe = (qf * jnp.exp(b)).astype(bf16)
    o_state = _dot(qe, st.astype(bf16), _NT)

    lvl = lvl_ref[...]
    blk_rows = lambda a, i: a[i * GLA_BLK:(i + 1) * GLA_BLK]
    tiles = [jnp.where(lvl == LVL_DIAG, _dot(blk_rows(qb, i), blk_rows(kb, i), _NT), 0.0)
             for i in range(nblk)]
    cross = None
    row = lax.broadcasted_iota(jnp.int32, (C, 1), 0)
    for t, m in enumerate(_gla_levels(C)):
        e = jnp.exp(-jnp.abs(x_ref[(t + 1) * C:(t + 2) * C, :]))
        if m == GLA_BLK:
            qt = (blk_rows(qf, 1) * blk_rows(e, 1)).astype(bf16)
            kt = (blk_rows(kf, 0) * blk_rows(e, 0)).astype(bf16)
            cross = _dot(qt, kt, _NT)
            continue
        if m >= 8:
            n = C // (2 * m)
            e3, q3, k3 = (a.reshape(n, 2 * m, DK) for a in (e, qf, kf))
            zero = jnp.zeros((n, m, DK), f32)
            qt = jnp.concatenate([zero, q3[:, m:, :] * e3[:, m:, :]], axis=1).reshape(C, DK)
            kt = jnp.concatenate([k3[:, :m, :] * e3[:, :m, :], zero], axis=1).reshape(C, DK)
        else:
            odd = ((row >> _log2(m)) & 1) == 1
            qt = jnp.where(odd, qf * e, 0.0)
            kt = jnp.where(odd, 0.0, kf * e)
        qt, kt = qt.astype(bf16), kt.astype(bf16)
        tiles = [jnp.where(lvl == _log2(m), _dot(blk_rows(qt, i), blk_rows(kt, i), _NT), tiles[i])
                 for i in range(nblk)]

    outs = []
    for i in range(nblk):
        lhs = tiles[i] if i == 0 else jnp.concatenate([cross, tiles[i]], axis=1)
        outs.append(_dot(lhs.astype(bf16), vb[0:(i + 1) * GLA_BLK]))
    o = o_state + jnp.concatenate(outs, axis=0)
    ms = jnp.mean(o * o, axis=-1, keepdims=True)
    on = o * lax.rsqrt(ms + RMS_EPS) * ng_ref[...]
    og_ref[:, vcols] = (on * g_ref[:, vcols].astype(f32)).astype(bf16)

    kd = (kf * jnp.exp(b_last - b)).astype(bf16)
    st_new = st * jnp.exp(b_last) + _dot(vb, kd, _TN)
    st_ref[hh] = st_new


def _gla_prompt(proj, la2, ng, s_meta):
    assert CHUNK in (GLA_BLK, 2 * GLA_BLK) and N_HEADS % GLA_HPS == 0
    diff, level = _gla_tables(CHUNK)
    hps = GLA_HPS
    rb = lambda b, s: b * N_CHUNKS + s
    const = lambda b, h, s: (0, 0)
    return pl.pallas_call(
        _gla_prompt_kernel,
        grid=(BATCH, N_HEADS // hps, N_CHUNKS),
        in_specs=[
            pl.BlockSpec((CHUNK, hps * DK), lambda b, h, s: (rb(b, s), COL_Q // (hps * DK) + h)),
            pl.BlockSpec((CHUNK, hps * DK), lambda b, h, s: (rb(b, s), COL_K // (hps * DK) + h)),
            pl.BlockSpec((CHUNK, hps * DV), lambda b, h, s: (rb(b, s), COL_V // (hps * DV) + h)),
            pl.BlockSpec((CHUNK, hps * DV), lambda b, h, s: (rb(b, s), COL_G // (hps * DV) + h)),
            pl.BlockSpec((CHUNK, hps * LA_W), lambda b, h, s: (rb(b, s), h)),
            pl.BlockSpec((1, DV), const),
            pl.BlockSpec((hps, DV, DK), lambda b, h, s: (h, 0, 0)),
            pl.BlockSpec(diff.shape, const),
            pl.BlockSpec(level.shape, const),
        ],
        out_specs=[
            pl.BlockSpec((CHUNK, hps * DV), lambda b, h, s: (rb(b, s), h)),
            pl.BlockSpec((1, hps, DK, DV), lambda b, h, s: (b, h, 0, 0)),
        ],
        out_shape=[jax.ShapeDtypeStruct((R_ALL, D_GLA_V), bf16),
                   jax.ShapeDtypeStruct((BATCH, N_HEADS, DK, DV), f32)],
        scratch_shapes=[pltpu.VMEM((hps, DV, DK), f32),
                        pltpu.VMEM((hps, diff.shape[0], DK), f32)],
        compiler_params=_params(3),
        name="gla_prompt",
    )(proj, proj, proj, proj, la2, ng, s_meta, diff, level)


def _gla_meta_kernel(k_ref, v_ref, la_ref, diff_ref, s_ref, x_ref):
    _decay_sums(la_ref[...], diff_ref, x_ref)
    kd = (k_ref[...].astype(f32) * jnp.exp(x_ref[N_META - 1:N_META, :] - x_ref[...])).astype(bf16)
    s_ref[0] = _dot(v_ref[...], kd, _TN)


def _gla_meta(proj, la2):
    diff = _gla_tables(N_META)[0][:N_META]
    mb = R_MAIN // N_META
    return pl.pallas_call(
        _gla_meta_kernel,
        grid=(N_HEADS,),
        in_specs=[
            pl.BlockSpec((N_META, DK), lambda h: (mb, COL_K // DK + h)),
            pl.BlockSpec((N_META, DV), lambda h: (mb, COL_V // DV + h)),
            pl.BlockSpec((N_META, LA_W), lambda h: (mb, h)),
            pl.BlockSpec(diff.shape, lambda h: (0, 0)),
        ],
        out_specs=pl.BlockSpec((1, DV, DK), lambda h: (h, 0, 0)),
        out_shape=jax.ShapeDtypeStruct((N_HEADS, DV, DK), f32),
        scratch_shapes=[pltpu.VMEM((N_META, DK), f32)],
        compiler_params=_params(1),
        name="gla_meta",
    )(proj, proj, la2, diff)


SAMPLE_TB = 16


def _gla_sample_kernel(q_ref, k_ref, v_ref, g_ref, la_ref, ng_ref, s0_ref,
                       og_in_ref, og_ref, s_ref):
    del og_in_ref
    TB = SAMPLE_TB
    a = jnp.exp(la_ref[:, 0:DK].astype(f32) + la_ref[:, DK:LA_W].astype(f32))
    kb = k_ref[...]
    qb = (q_ref[...].astype(f32) * (DK ** -0.5)).astype(bf16)
    vf = v_ref[...].astype(f32)
    rid = lax.broadcasted_iota(jnp.int32, (TB, LANES), 0)
    outs = []
    for r in range(TB):
        pick = (rid == r).astype(bf16)
        a_col = sum(_dot(p, pick, _TN) for p in _split3(a))
        k_col = _dot(kb, pick, _TN)
        q_col = _dot(qb, pick, _TN)
        a4 = jnp.concatenate([a_col] * (DV // LANES), axis=1)
        k4 = jnp.concatenate([k_col] * (DV // LANES), axis=1)
        q4 = jnp.concatenate([q_col] * (DV // LANES), axis=1)
        s_new = a4 * s0_ref[r, 0] + k4 * vf[r:r + 1, :]
        s_ref[r, 0] = s_new
        outs.append(jnp.sum(q4 * s_new, axis=0, keepdims=True))
    o = jnp.concatenate(outs, axis=0)
    ms = jnp.mean(o * o, axis=-1, keepdims=True)
    on = o * lax.rsqrt(ms + RMS_EPS) * ng_ref[...]
    og_ref[...] = (on * g_ref[...].astype(f32)).astype(bf16)


def _gla_sample(proj, la2, ng, s0, og):
    TB = SAMPLE_TB
    r0 = NP // TB
    return pl.pallas_call(
        _gla_sample_kernel,
        grid=(DEC_BATCH // TB, N_HEADS),
        in_specs=[
            pl.BlockSpec((TB, DK), lambda i, h: (r0 + i, COL_Q // DK + h)),
            pl.BlockSpec((TB, DK), lambda i, h: (r0 + i, COL_K // DK + h)),
            pl.BlockSpec((TB, DV), lambda i, h: (r0 + i, COL_V // DV + h)),
            pl.BlockSpec((TB, DV), lambda i, h: (r0 + i, COL_G // DV + h)),
            pl.BlockSpec((TB, LA_W), lambda i, h: (r0 + i, h)),
            pl.BlockSpec((1, DV), lambda i, h: (0, 0)),
            pl.BlockSpec((TB, 1, DK, DV), lambda i, h: (i, h, 0, 0)),
            pl.BlockSpec(memory_space=pl.ANY),
        ],
        out_specs=[
            pl.BlockSpec((TB, DV), lambda i, h: (r0 + i, h)),
            pl.BlockSpec((TB, 1, DK, DV), lambda i, h: (i, h, 0, 0)),
        ],
        out_shape=[jax.ShapeDtypeStruct((R_ALL, D_GLA_V), bf16),
                   jax.ShapeDtypeStruct((DEC_BATCH, N_HEADS, DK, DV), f32)],
        input_output_aliases={7: 0},
        compiler_params=_params(2),
        name="gla_sample",
    )(proj, proj, proj, proj, la2, ng, s0, og)


CONV_TR, CONV_TC = 1024, 1024
CONV_PAD = 8


def _conv_prompt_kernel(cb_ref, cc_ref, ch_ref, mc_ref, mh_ref, w_ref, y_ref, nb_ref, u_ref):
    t = pl.program_id(2)
    TR = CONV_TR

    @pl.when(t == 0)
    def _():
        mu = mc_ref[...].astype(f32) * mh_ref[...].astype(f32)
        u_ref[CONV_PAD - 2:CONV_PAD, :] = mu[N_META - 2:N_META, :]

    u = cc_ref[...].astype(f32) * ch_ref[...].astype(f32)
    u_ref[CONV_PAD:CONV_PAD + TR, :] = u
    w = w_ref[...]
    zc = (w[0:1, :] * u_ref[CONV_PAD - 2:CONV_PAD - 2 + TR, :]
          + w[1:2, :] * u_ref[CONV_PAD - 1:CONV_PAD - 1 + TR, :]
          + w[2:3, :] * u)
    y_ref[...] = (cb_ref[...].astype(f32) * zc).astype(bf16)
    last = u[TR - 2:TR, :]
    u_ref[CONV_PAD - 2:CONV_PAD, :] = last

    @pl.when(t == SEQ // TR - 1)
    def _():
        nb_ref[0] = last


def _conv_prompt(proj, conv_w):
    TR, TC = CONV_TR, CONV_TC
    nt = SEQ // TR
    mrow = R_MAIN // N_META
    return pl.pallas_call(
        _conv_prompt_kernel,
        grid=(BATCH, D_CONV // TC, nt),
        in_specs=[
            pl.BlockSpec((TR, TC), lambda b, j, t: (b * nt + t, COL_CB // TC + j)),
            pl.BlockSpec((TR, TC), lambda b, j, t: (b * nt + t, COL_CC // TC + j)),
            pl.BlockSpec((TR, TC), lambda b, j, t: (b * nt + t, COL_CH // TC + j)),
            pl.BlockSpec((N_META, TC), lambda b, j, t: (mrow, COL_CC // TC + j)),
            pl.BlockSpec((N_META, TC), lambda b, j, t: (mrow, COL_CH // TC + j)),
            pl.BlockSpec((CONV_WIDTH, TC), lambda b, j, t: (0, j)),
        ],
        out_specs=[
            pl.BlockSpec((TR, TC), lambda b, j, t: (b * nt + t, j)),
            pl.BlockSpec((1, CONV_WIDTH - 1, TC), lambda b, j, t: (b, 0, j)),
        ],
        out_shape=[jax.ShapeDtypeStruct((R_ALL, D_CONV), bf16),
                   jax.ShapeDtypeStruct((BATCH, CONV_WIDTH - 1, D_CONV), f32)],
        scratch_shapes=[pltpu.VMEM((CONV_PAD + TR, TC), f32)],
        compiler_params=_params(3),
        name="conv_prompt",
    )(proj, proj, proj, proj, proj, conv_w)


def _conv_sample_kernel(cb_ref, cc_ref, ch_ref, buf_ref, w_ref, y_in_ref, y_ref, nb_ref):
    del y_in_ref
    u = cc_ref[...].astype(f32) * ch_ref[...].astype(f32)
    w = w_ref[...]
    b0 = buf_ref[:, 0:D_CONV]
    b1 = buf_ref[:, D_CONV:2 * D_CONV]
    zc = w[0:1, :] * b0 + w[1:2, :] * b1 + w[2:3, :] * u
    y_ref[...] = (cb_ref[...].astype(f32) * zc).astype(bf16)
    nb_ref[:, 0:D_CONV] = b1
    nb_ref[:, D_CONV:2 * D_CONV] = u


def _conv_sample(proj, buf, conv_w, yc):
    rb = NP // DEC_BATCH
    return pl.pallas_call(
        _conv_sample_kernel,
        grid=(1,),
        in_specs=[
            pl.BlockSpec((DEC_BATCH, D_CONV), lambda i: (rb, COL_CB // D_CONV)),
            pl.BlockSpec((DEC_BATCH, D_CONV), lambda i: (rb, COL_CC // D_CONV)),
            pl.BlockSpec((DEC_BATCH, D_CONV), lambda i: (rb, COL_CH // D_CONV)),
            pl.BlockSpec((DEC_BATCH, 2 * D_CONV), lambda i: (0, 0)),
            pl.BlockSpec((CONV_WIDTH, D_CONV), lambda i: (0, 0)),
            pl.BlockSpec(memory_space=pl.ANY),
        ],
        out_specs=[
            pl.BlockSpec((DEC_BATCH, D_CONV), lambda i: (rb, 0)),
            pl.BlockSpec((DEC_BATCH, 2 * D_CONV), lambda i: (0, 0)),
        ],
        out_shape=[jax.ShapeDtypeStruct((R_ALL, D_CONV), bf16),
                   jax.ShapeDtypeStruct((DEC_BATCH, 2 * D_CONV), f32)],
        input_output_aliases={5: 0},
        compiler_params=_params(1),
        name="conv_sample",
    )(proj, proj, proj, buf, conv_w, yc)


OUT_TM, OUT_TN = 1664, 256


def _outproj_kernel(o_ref, y_ref, wo_ref, wy_ref, h_ref, s_ref):
    s_ref[...] = (ALPHA * h_ref[...] + _dot(o_ref[...], wo_ref[...].astype(bf16))
                  + _dot(y_ref[...], wy_ref[...].astype(bf16)))


def _outproj(og, yc, w_out, h):
    return pl.pallas_call(
        _outproj_kernel,
        grid=(R_MAIN // OUT_TM, D_MODEL // OUT_TN),
        in_specs=[
            pl.BlockSpec((OUT_TM, D_GLA_V), lambda i, j: (i, 0)),
            pl.BlockSpec((OUT_TM, D_CONV), lambda i, j: (i, 0)),
            pl.BlockSpec((D_GLA_V, OUT_TN), lambda i, j: (0, j)),
            pl.BlockSpec((D_CONV, OUT_TN), lambda i, j: (1, j)),
            pl.BlockSpec((OUT_TM, OUT_TN), lambda i, j: (i, j)),
        ],
        out_specs=pl.BlockSpec((OUT_TM, OUT_TN), lambda i, j: (i, j)),
        out_shape=jax.ShapeDtypeStruct((R_MAIN, D_MODEL), f32),
        compiler_params=_params(2),
        name="outproj",
    )(og, yc, w_out, w_out, h)


LN1_ROWS = 320


def _ln1_kernel(s_ref, g_ref, b_ref, h_ref, hb_ref):
    h = _layer_norm(s_ref[...], g_ref[...], b_ref[...])
    h_ref[...] = h
    hb_ref[...] = h.astype(bf16)


def _ln1(s1, g, b):
    return pl.pallas_call(
        _ln1_kernel,
        grid=(R_MAIN // LN1_ROWS,),
        in_specs=[
            pl.BlockSpec((LN1_ROWS, D_MODEL), lambda i: (i, 0)),
            pl.BlockSpec((1, D_MODEL), lambda i: (0, 0)),
            pl.BlockSpec((1, D_MODEL), lambda i: (0, 0)),
        ],
        out_specs=[
            pl.BlockSpec((LN1_ROWS, D_MODEL), lambda i: (i, 0)),
            pl.BlockSpec((LN1_ROWS, D_MODEL), lambda i: (i, 0)),
        ],
        out_shape=[jax.ShapeDtypeStruct((R_MAIN, D_MODEL), f32),
                   jax.ShapeDtypeStruct((R_MAIN, D_MODEL), bf16)],
        compiler_params=_params(1),
        name="ln1",
    )(s1, g, b)


LN2_ROWS = 512


def _ln2_kernel(sp_ref, ss_ref, g_ref, b_ref, yp_ref, ys_ref):
    yp_ref[...] = _layer_norm(sp_ref[...], g_ref[...], b_ref[...])

    @pl.when(pl.program_id(0) == 0)
    def _():
        ys_ref[...] = _layer_norm(ss_ref[...], g_ref[...], b_ref[...])


def _ln2(s2, g, b):
    return pl.pallas_call(
        _ln2_kernel,
        grid=(NP // LN2_ROWS,),
        in_specs=[
            pl.BlockSpec((LN2_ROWS, D_MODEL), lambda i: (i, 0)),
            pl.BlockSpec((DEC_BATCH, D_MODEL), lambda i: (NP // DEC_BATCH, 0)),
            pl.BlockSpec((1, D_MODEL), lambda i: (0, 0)),
            pl.BlockSpec((1, D_MODEL), lambda i: (0, 0)),
        ],
        out_specs=[
            pl.BlockSpec((LN2_ROWS, D_MODEL), lambda i: (i, 0)),
            pl.BlockSpec((DEC_BATCH, D_MODEL), lambda i: (0, 0)),
        ],
        out_shape=[jax.ShapeDtypeStruct((NP, D_MODEL), f32),
                   jax.ShapeDtypeStruct((DEC_BATCH, D_MODEL), f32)],
        compiler_params=_params(1),
        name="ln2",
    )(s2, s2, g, b)


FFN_TM, FFN_TN = 1664, 256


FFN_NT = D_FF // FFN_TN


def _ffn_up_kernel(h_ref, wg_ref, wu_ref, wd_ref, act_ref, wdb_ref):
    i = pl.program_id(0)
    j = pl.program_id(1)

    @pl.when(j < FFN_NT)
    def _():
        h = h_ref[...]
        a = _dot(h, wg_ref[...].astype(bf16))
        u = _dot(h, wu_ref[...].astype(bf16))
        act_ref[...] = (a * jax.nn.sigmoid(a) * u).astype(bf16)

    @pl.when(j >= FFN_NT)
    def _():
        act_ref[...] = jnp.zeros_like(act_ref)

    @pl.when(jnp.logical_and(i == 0, j < FFN_NT))
    def _():
        wdb_ref[...] = wd_ref[...].astype(bf16)

    @pl.when(jnp.logical_and(i == 0, j >= FFN_NT))
    def _():
        wdb_ref[...] = jnp.zeros_like(wdb_ref)


def _ffn_up(hb, wg, wu, wd):
    nt_pad = D_FF_PAD // FFN_TN
    jw = lambda j: jnp.minimum(j, FFN_NT - 1)
    return pl.pallas_call(
        _ffn_up_kernel,
        grid=(R_MAIN // FFN_TM, nt_pad),
        in_specs=[
            pl.BlockSpec((FFN_TM, D_MODEL), lambda i, j: (i, 0), pipeline_mode=pl.Buffered(1)),
            pl.BlockSpec((D_MODEL, FFN_TN), lambda i, j: (0, jw(j))),
            pl.BlockSpec((D_MODEL, FFN_TN), lambda i, j: (0, jw(j))),
            pl.BlockSpec((FFN_TN, D_MODEL), lambda i, j: (jnp.where(i == 0, jw(j), FFN_NT - 1), 0)),
        ],
        out_specs=[
            pl.BlockSpec((FFN_TM, FFN_TN), lambda i, j: (i, j)),
            pl.BlockSpec((FFN_TN, D_MODEL), lambda i, j: (jnp.where(i == 0, j, nt_pad - 1), 0)),
        ],
        out_shape=[jax.ShapeDtypeStruct((R_MAIN, D_FF_PAD), bf16),
                   jax.ShapeDtypeStruct((D_FF_PAD, D_MODEL), bf16)],
        compiler_params=_params(2),
        name="ffn_up",
    )(hb, wg, wu, wd)


DOWN_TM, DOWN_TN, DOWN_TK = 2080, 1024, 1024


def _ffn_down_kernel(a_ref, w_ref, h_ref, s_ref):
    @pl.when(pl.program_id(2) == 0)
    def _():
        s_ref[...] = ALPHA * h_ref[...]

    s_ref[...] += _dot(a_ref[...], w_ref[...])


def _ffn_down(act, wd, h1):
    return pl.pallas_call(
        _ffn_down_kernel,
        grid=(R_MAIN // DOWN_TM, D_MODEL // DOWN_TN, D_FF_PAD // DOWN_TK),
        in_specs=[
            pl.BlockSpec((DOWN_TM, DOWN_TK), lambda i, j, k: (i, k)),
            pl.BlockSpec((DOWN_TK, DOWN_TN), lambda i, j, k: (k, j)),
            pl.BlockSpec((DOWN_TM, DOWN_TN), lambda i, j, k: (i, j)),
        ],
        out_specs=pl.BlockSpec((DOWN_TM, DOWN_TN), lambda i, j, k: (i, j)),
        out_shape=jax.ShapeDtypeStruct((R_MAIN, D_MODEL), f32),
        compiler_params=_params(3),
        name="ffn_down",
    )(act, wd, h1)


def kernel(x_prompt, x_sample, state_gla, state_conv, meta_tokens, emb_ln_g, emb_ln_b,
           w_in, w_a2, b_a, gla_norm_g, conv_w, w_out, ln1_g, ln1_b,
           w_ffn_gate, w_ffn_up, w_ffn_down, ln2_g, ln2_b):
    assert x_prompt.shape == (BATCH, SEQ, D_MODEL) and x_sample.shape == (DEC_BATCH, 1, D_MODEL)
    assert w_in.shape[0] == 1, "single layer"
    row = lambda v: v.reshape(1, -1)

    w_in_t = jnp.transpose(w_in[0])
    wa2p = jnp.pad(w_a2[0], ((0, LANES - GATE_RANK), (0, 0)))

    xp = x_prompt.reshape(NP, D_MODEL)
    xt = jnp.concatenate([x_sample.reshape(DEC_BATCH, D_MODEL), meta_tokens.astype(f32),
                          jnp.zeros((LN_ROWS - DEC_BATCH - N_META, D_MODEL), f32)], axis=0)

    h, hb = _ln0(xp, xt, row(emb_ln_g), row(emb_ln_b))
    proj, la2 = _proj(hb, w_in_t, wa2p, row(b_a[0]))

    ng = row(gla_norm_g[0])
    s_meta = _gla_meta(proj, la2)
    og, s_p = _gla_prompt(proj, la2, ng, s_meta)
    og, s_s = _gla_sample(proj, la2, ng, state_gla[0], og)
    yc, nb_p = _conv_prompt(proj, conv_w[0])
    yc, nb_s = _conv_sample(proj, state_conv[0].reshape(DEC_BATCH, 2 * D_CONV), conv_w[0], yc)

    s1 = _outproj(og, yc, w_out[0], h)
    h1, h1b = _ln1(s1, row(ln1_g[0]), row(ln1_b[0]))
    act, wd = _ffn_up(h1b, w_ffn_gate[0], w_ffn_up[0], w_ffn_down[0])
    s2 = _ffn_down(act, wd, h1)
    y_p, y_s = _ln2(s2, row(ln2_g[0]), row(ln2_b[0]))

    return (y_p.reshape(BATCH, SEQ, D_MODEL),
            y_s.reshape(DEC_BATCH, 1, D_MODEL),
            s_p[None],
            nb_p[None],
            s_s[None],
            nb_s.reshape(1, DEC_BATCH, CONV_WIDTH - 1, D_CONV))
```

```python
import jax
import jax.numpy as jnp
import numpy as np
from jax import lax
from jax.experimental import pallas as pl
from jax.experimental.pallas import tpu as pltpu

f32 = jnp.float32
bf16 = jnp.bfloat16

D_MODEL = 4096
BATCH = 4
SEQ = 2048
DEC_BATCH = 128
N_META = 16
D_GLA_V = D_MODEL // 2
D_CONV = D_MODEL - D_GLA_V
D_GLA_K = D_GLA_V // 2
N_HEADS = 4
DK = D_GLA_K // N_HEADS
DV = D_GLA_V // N_HEADS
GATE_RANK = 16
GATE_TAU = 16.0
CONV_WIDTH = 3
D_FF = 11008
ALPHA = 2.0 ** 0.25
LN_EPS = 1e-5
RMS_EPS = 1e-6

LANES = 128
NP = BATCH * SEQ
R_MAIN = NP + DEC_BATCH
R_ALL = R_MAIN + LANES
D_PROJ = 2 * D_GLA_K + 2 * D_GLA_V + 3 * D_CONV
D_FF_PAD = 11264
CHUNK = 128
N_CHUNKS = SEQ // CHUNK
VMEM_LIMIT = 56 * 1024 * 1024

COL_Q, COL_K, COL_V, COL_G = 0, D_GLA_K, 2 * D_GLA_K, 2 * D_GLA_K + D_GLA_V
COL_CB = COL_G + D_GLA_V
COL_CC = COL_CB + D_CONV
COL_CH = COL_CC + D_CONV


def _params(n_axes):
    return pltpu.CompilerParams(
        dimension_semantics=("arbitrary",) * n_axes, vmem_limit_bytes=VMEM_LIMIT)


def _layer_norm(x, g, b):
    mu = jnp.mean(x, axis=-1, keepdims=True)
    xc = x - mu
    var = jnp.mean(xc * xc, axis=-1, keepdims=True)
    return xc * lax.rsqrt(var + LN_EPS) * g + b


def _split3(x):
    hi = x.astype(bf16)
    r = x - hi.astype(f32)
    mid = r.astype(bf16)
    lo = (r - mid.astype(f32)).astype(bf16)
    return hi, mid, lo


def _dot(a, b, dims=(((1,), (0,)), ((), ()))):
    return lax.dot_general(a, b, dims, preferred_element_type=f32)


_NT = (((1,), (1,)), ((), ()))
_TN = (((0,), (0,)), ((), ()))


def _dot_sel(sel_bf16, x_f32, dims=(((1,), (0,)), ((), ()))):
    hi, mid, lo = _split3(x_f32)
    return _dot(sel_bf16, hi, dims) + _dot(sel_bf16, mid, dims) + _dot(sel_bf16, lo, dims)


def _dot_f32(a, b):
    ah, am, _ = _split3(a)
    bh, bm, _ = _split3(b)
    return _dot(ah, bh) + _dot(ah, bm) + _dot(am, bh)


def _log2(n):
    assert n & (n - 1) == 0
    return n.bit_length() - 1


def _log_sigmoid(z):
    return jnp.minimum(z, 0.0) - jnp.log(1.0 + jnp.exp(-jnp.abs(z)))


LN_ROWS = 256


def _ln0_kernel(xp_ref, xt_ref, g_ref, b_ref, wa_ref, h_ref, hb_ref, alr_ref, wab_ref):
    i = pl.program_id(0)

    @pl.when(i == 0)
    def _():
        wab_ref[...] = wa_ref[...].astype(bf16)

    def emit(x):
        h = _layer_norm(x, g_ref[...], b_ref[...])
        hb = h.astype(bf16)
        h_ref[...] = h
        hb_ref[...] = hb
        alr_ref[...] = _dot(hb, wab_ref[...], _NT)

    @pl.when(i < NP // LN_ROWS)
    def _():
        emit(xp_ref[...])

    @pl.when(i >= NP // LN_ROWS)
    def _():
        emit(xt_ref[...])


def _ln0(xp, xt, g, b, w_in_t):
    n_p = NP // LN_ROWS
    return pl.pallas_call(
        _ln0_kernel,
        grid=(R_ALL // LN_ROWS,),
        in_specs=[
            pl.BlockSpec((LN_ROWS, D_MODEL), lambda i: (jnp.minimum(i, n_p - 1), 0)),
            pl.BlockSpec((LN_ROWS, D_MODEL), lambda i: (0, 0)),
            pl.BlockSpec((1, D_MODEL), lambda i: (0, 0)),
            pl.BlockSpec((1, D_MODEL), lambda i: (0, 0)),
            pl.BlockSpec((LANES, D_MODEL), lambda i: (COL_CB // LANES, 0)),
        ],
        out_specs=[
            pl.BlockSpec((LN_ROWS, D_MODEL), lambda i: (i, 0)),
            pl.BlockSpec((LN_ROWS, D_MODEL), lambda i: (i, 0)),
            pl.BlockSpec((LN_ROWS, LANES), lambda i: (i, 0)),
        ],
        out_shape=[jax.ShapeDtypeStruct((R_ALL, D_MODEL), f32),
                   jax.ShapeDtypeStruct((R_ALL, D_MODEL), bf16),
                   jax.ShapeDtypeStruct((R_ALL, LANES), f32)],
        scratch_shapes=[pltpu.VMEM((LANES, D_MODEL), bf16)],
        compiler_params=_params(1),
        name="ln0",
    )(xp, xt, g, b, w_in_t)


PROJ_TM, PROJ_TN = 1408, 512


LA_W = 2 * DK


def _proj_kernel(h_ref, w_ref, alr_ref, wa2_ref, ba_ref, p_ref, la_ref):
    j = pl.program_id(1)

    @pl.when(j < N_HEADS)
    def _():
        p_ref[...] = _dot(h_ref[...], w_ref[...].astype(bf16), _NT).astype(bf16)
        la = _gate_log_decay(alr_ref[...], wa2_ref[...], ba_ref[...])
        hi = la.astype(bf16)
        la_ref[:, 0:DK] = hi
        la_ref[:, DK:LA_W] = (la - hi.astype(f32)).astype(bf16)

    is_g = jnp.logical_and(j * PROJ_TN >= COL_G, j * PROJ_TN < COL_CB)

    @pl.when(is_g)
    def _():
        p = _dot(h_ref[...], w_ref[...].astype(bf16), _NT)
        p_ref[...] = (p * jax.nn.sigmoid(p)).astype(bf16)

    @pl.when(jnp.logical_and(j >= N_HEADS, jnp.logical_not(is_g)))
    def _():
        p_ref[...] = _dot(h_ref[...], w_ref[...].astype(bf16), _NT).astype(bf16)


def _proj(hb, w_in_t, alr, wa2p, ba):
    assert COL_G >= N_HEADS * PROJ_TN

    def w_row(j):
        return pl.multiple_of(j * PROJ_TN + jnp.where(j * PROJ_TN >= COL_CB, GATE_RANK, 0), GATE_RANK)

    jh = lambda j: jnp.minimum(j, N_HEADS - 1)
    return pl.pallas_call(
        _proj_kernel,
        grid=(R_ALL // PROJ_TM, D_PROJ // PROJ_TN),
        in_specs=[
            pl.BlockSpec((PROJ_TM, D_MODEL), lambda i, j: (i, 0)),
            pl.BlockSpec((pl.Element(PROJ_TN), pl.Element(D_MODEL)), lambda i, j: (w_row(j), 0)),
            pl.BlockSpec((PROJ_TM, LANES), lambda i, j: (i, 0)),
            pl.BlockSpec((LANES, DK), lambda i, j: (0, jh(j))),
            pl.BlockSpec((1, DK), lambda i, j: (0, jh(j))),
        ],
        out_specs=[
            pl.BlockSpec((PROJ_TM, PROJ_TN), lambda i, j: (i, j)),
            pl.BlockSpec((PROJ_TM, LA_W), lambda i, j: (i, jh(j))),
        ],
        out_shape=[jax.ShapeDtypeStruct((R_ALL, D_PROJ), bf16),
                   jax.ShapeDtypeStruct((R_ALL, N_HEADS * LA_W), bf16)],
        compiler_params=_params(2),
        name="proj",
    )(hb, w_in_t, alr, wa2p, ba)


GLA_BLK = 128
GLA_HPS = 4
LVL_DIAG = _log2(GLA_BLK)


def _gla_levels(c):
    return [c >> (t + 1) for t in range(_log2(c))]


def _gla_tables(c):
    idx = np.arange(c)
    tri = (idx[None, :] <= idx[:, None]).astype(np.float32)
    mats = [tri]
    for m in _gla_levels(c):
        ref = ((idx // m) | 1) * m - 1
        mats.append(tri - tri[ref])
    diff = np.concatenate(mats, axis=0)
    diff = np.concatenate([diff, diff], axis=1)
    t = np.arange(min(c, GLA_BLK))
    level = np.full((t.size, t.size), -1, np.int32)
    for m in _gla_levels(t.size):
        same_pair = (t[:, None] // (2 * m)) == (t[None, :] // (2 * m))
        odd_even = (((t[:, None] // m) & 1) == 1) & (((t[None, :] // m) & 1) == 0)
        level[same_pair & odd_even] = _log2(m)
    level[t[:, None] == t[None, :]] = LVL_DIAG
    return jnp.asarray(diff, bf16), jnp.asarray(level)


def _gate_log_decay(alr, wa2, ba):
    return _log_sigmoid(_dot_f32(alr, wa2) + ba) * (1.0 / GATE_TAU)


def _decay_sums(la2, diff_ref, x_ref):
    x_ref[...] = _dot(diff_ref[...], jnp.concatenate([la2[:, 0:DK], la2[:, DK:LA_W]], axis=0))


GLA_STEPS = BATCH * (N_HEADS // GLA_HPS) * N_CHUNKS
SAMPLE_PER_STEP = DEC_BATCH // GLA_STEPS


def _gla_kernel(q_ref, k_ref, v_ref, g_ref, la_ref, ng_ref, s0_ref, diff_ref, lvl_ref,
                qs_ref, ks_ref, vs_ref, gs_ref, las_ref, s0s_ref,
                og_ref, sfin_ref, ogs_ref, ss_ref, st_ref, x_ref, ogacc_ref):
    s = pl.program_id(2)
    step = (pl.program_id(0) * pl.num_programs(1) + pl.program_id(1)) * N_CHUNKS + s

    @pl.when(s == 0)
    def _():
        st_ref[...] = s0_ref[...]

    @pl.when(step == 0)
    def _():
        ogacc_ref[...] = jnp.zeros_like(ogacc_ref)

    for hh in range(GLA_HPS):
        _gla_prompt_head(hh, q_ref, k_ref, v_ref, g_ref, la_ref, ng_ref,
                         diff_ref, lvl_ref, og_ref, st_ref, x_ref.at[hh])
    for t in range(SAMPLE_PER_STEP):
        for hh in range(N_HEADS):
            _gla_sample_token(step * SAMPLE_PER_STEP + t, t, hh, qs_ref, ks_ref, vs_ref, gs_ref,
                              las_ref, ng_ref, s0s_ref, ss_ref, ogacc_ref)

    @pl.when(s == N_CHUNKS - 1)
    def _():
        for hh in range(GLA_HPS):
            sfin_ref[0, hh] = st_ref[hh].T

    @pl.when(step == GLA_STEPS - 1)
    def _():
        ogs_ref[...] = ogacc_ref[...].astype(bf16)


def _gla_sample_token(r, t, hh, qs_ref, ks_ref, vs_ref, gs_ref, las_ref, ng_ref, s0s_ref, ss_ref,
                      ogacc_ref):
    kcols = slice(hh * DK, (hh + 1) * DK)
    vcols = slice(hh * DV, (hh + 1) * DV)
    as_col = (lax.broadcasted_iota(jnp.int32, (DEC_BATCH, LANES), 0) == r).astype(bf16)
    as_row = (lax.broadcasted_iota(jnp.int32, (8, DEC_BATCH), 1) == r).astype(bf16)
    la_col = (_dot(las_ref[:, hh * LA_W:hh * LA_W + DK], as_col, _TN)
              + _dot(las_ref[:, hh * LA_W + DK:(hh + 1) * LA_W], as_col, _TN))
    a_col = jnp.exp(la_col)
    k_col = _dot(ks_ref[:, kcols], as_col, _TN)
    q_col = _dot(qs_ref[:, kcols], as_col, _TN) * (DK ** -0.5)
    v_row = _dot(as_row, vs_ref[:, vcols])[0:1]
    g_row = _dot(as_row, gs_ref[:, vcols])[0:1]
    wide = lambda c: jnp.concatenate([c] * (DV // LANES), axis=1)
    s_new = wide(a_col) * s0s_ref[t, hh] + wide(k_col) * v_row
    ss_ref[t, hh] = s_new
    o = jnp.sum(wide(q_col) * s_new, axis=0, keepdims=True)
    ms = jnp.mean(o * o, axis=-1, keepdims=True)
    og_row = o * lax.rsqrt(ms + RMS_EPS) * ng_ref[...] * g_row
    rows = lax.broadcasted_iota(jnp.int32, (DEC_BATCH, 1), 0)
    ogacc_ref[:, vcols] = jnp.where(rows == r, og_row, ogacc_ref[:, vcols])


def _gla_prompt_head(hh, q_ref, k_ref, v_ref, g_ref, la_ref, ng_ref,
                     diff_ref, lvl_ref, og_ref, st_ref, x_ref):
    C = CHUNK
    nblk = C // GLA_BLK
    kcols = slice(hh * DK, (hh + 1) * DK)
    vcols = slice(hh * DV, (hh + 1) * DV)

    _decay_sums(la_ref[:, hh * LA_W:(hh + 1) * LA_W], diff_ref, x_ref)
    kb = k_ref[:, kcols]
    kf = kb.astype(f32)
    vb = v_ref[:, vcols]
    qf = q_ref[:, kcols].astype(f32) * (DK ** -0.5)
    qb = qf.astype(bf16)
    b = x_ref[0:C, :]
    b_last = x_ref[C - 1:C, :]

    st = st_ref[hh]
    qe = (qf * jnp.exp(b)).astype(bf16)
    o_state = _dot(qe, st.astype(bf16), _NT)

    lvl = lvl_ref[...]
    blk_rows = lambda a, i: a[i * GLA_BLK:(i + 1) * GLA_BLK]
    tiles = [jnp.where(lvl == LVL_DIAG, _dot(blk_rows(qb, i), blk_rows(kb, i), _NT), 0.0)
             for i in range(nblk)]
    cross = None
    row = lax.broadcasted_iota(jnp.int32, (C, 1), 0)
    for t, m in enumerate(_gla_levels(C)):
        e = jnp.exp(-jnp.abs(x_ref[(t + 1) * C:(t + 2) * C, :]))
        if m == GLA_BLK:
            qt = (blk_rows(qf, 1) * blk_rows(e, 1)).astype(bf16)
            kt = (blk_rows(kf, 0) * blk_rows(e, 0)).astype(bf16)
            cross = _dot(qt, kt, _NT)
            continue
        if m >= 8:
            n = C // (2 * m)
            e3, q3, k3 = (a.reshape(n, 2 * m, DK) for a in (e, qf, kf))
            zero = jnp.zeros((n, m, DK), f32)
            qt = jnp.concatenate([zero, q3[:, m:, :] * e3[:, m:, :]], axis=1).reshape(C, DK)
            kt = jnp.concatenate([k3[:, :m, :] * e3[:, :m, :], zero], axis=1).reshape(C, DK)
        else:
            odd = ((row >> _log2(m)) & 1) == 1
            qt = jnp.where(odd, qf * e, 0.0)
            kt = jnp.where(odd, 0.0, kf * e)
        qt, kt = qt.astype(bf16), kt.astype(bf16)
        tiles = [jnp.where(lvl == _log2(m), _dot(blk_rows(qt, i), blk_rows(kt, i), _NT), tiles[i])
                 for i in range(nblk)]

    outs = []
    for i in range(nblk):
        lhs = tiles[i] if i == 0 else jnp.concatenate([cross, tiles[i]], axis=1)
        outs.append(_dot(lhs.astype(bf16), vb[0:(i + 1) * GLA_BLK]))
    o = o_state + jnp.concatenate(outs, axis=0)
    ms = jnp.mean(o * o, axis=-1, keepdims=True)
    on = o * lax.rsqrt(ms + RMS_EPS) * ng_ref[...]
    og_ref[:, vcols] = (on * g_ref[:, vcols].astype(f32)).astype(bf16)

    kd = (kf * jnp.exp(b_last - b)).astype(bf16)
    st_new = st * jnp.exp(b_last) + _dot(vb, kd, _TN)
    st_ref[hh] = st_new


def _gla(proj, la2, ng, s_meta, s0_sample):
    assert CHUNK in (GLA_BLK, 2 * GLA_BLK) and N_HEADS % GLA_HPS == 0
    assert DEC_BATCH % GLA_STEPS == 0
    diff, level = _gla_tables(CHUNK)
    hps, sps = GLA_HPS, SAMPLE_PER_STEP
    nh = N_HEADS // hps
    rb = lambda b, s: b * N_CHUNKS + s
    const = lambda b, h, s: (0, 0)
    step = lambda b, h, s: (b * nh + h) * N_CHUNKS + s
    srow = NP // DEC_BATCH
    scol = lambda col, width: (lambda b, h, s: (srow, col // width))
    return pl.pallas_call(
        _gla_kernel,
        grid=(BATCH, nh, N_CHUNKS),
        in_specs=[
            pl.BlockSpec((CHUNK, hps * DK), lambda b, h, s: (rb(b, s), COL_Q // (hps * DK) + h)),
            pl.BlockSpec((CHUNK, hps * DK), lambda b, h, s: (rb(b, s), COL_K // (hps * DK) + h)),
            pl.BlockSpec((CHUNK, hps * DV), lambda b, h, s: (rb(b, s), COL_V // (hps * DV) + h)),
            pl.BlockSpec((CHUNK, hps * DV), lambda b, h, s: (rb(b, s), COL_G // (hps * DV) + h)),
            pl.BlockSpec((CHUNK, hps * LA_W), lambda b, h, s: (rb(b, s), h)),
            pl.BlockSpec((1, DV), const),
            pl.BlockSpec((hps, DV, DK), lambda b, h, s: (h, 0, 0)),
            pl.BlockSpec(diff.shape, const),
            pl.BlockSpec(level.shape, const),
            pl.BlockSpec((DEC_BATCH, D_GLA_K), scol(COL_Q, D_GLA_K)),
            pl.BlockSpec((DEC_BATCH, D_GLA_K), scol(COL_K, D_GLA_K)),
            pl.BlockSpec((DEC_BATCH, D_GLA_V), scol(COL_V, D_GLA_V)),
            pl.BlockSpec((DEC_BATCH, D_GLA_V), scol(COL_G, D_GLA_V)),
            pl.BlockSpec((DEC_BATCH, N_HEADS * LA_W), lambda b, h, s: (srow, 0)),
            pl.BlockSpec((sps, N_HEADS, DK, DV), lambda b, h, s: (step(b, h, s), 0, 0, 0)),
        ],
        out_specs=[
            pl.BlockSpec((CHUNK, hps * DV), lambda b, h, s: (rb(b, s), h)),
            pl.BlockSpec((1, hps, DK, DV), lambda b, h, s: (b, h, 0, 0)),
            pl.BlockSpec((DEC_BATCH, D_GLA_V), const),
            pl.BlockSpec((sps, N_HEADS, DK, DV), lambda b, h, s: (step(b, h, s), 0, 0, 0)),
        ],
        out_shape=[jax.ShapeDtypeStruct((R_ALL, D_GLA_V), bf16),
                   jax.ShapeDtypeStruct((BATCH, N_HEADS, DK, DV), f32),
                   jax.ShapeDtypeStruct((DEC_BATCH, D_GLA_V), bf16),
                   jax.ShapeDtypeStruct((DEC_BATCH, N_HEADS, DK, DV), f32)],
        scratch_shapes=[pltpu.VMEM((hps, DV, DK), f32),
                        pltpu.VMEM((hps, diff.shape[0], DK), f32),
                        pltpu.VMEM((DEC_BATCH, D_GLA_V), f32)],
        compiler_params=_params(3),
        name="gla",
    )(proj, proj, proj, proj, la2, ng, s_meta, diff, level,
      proj, proj, proj, proj, la2, s0_sample)


def _gla_meta_kernel(k_ref, v_ref, la_ref, diff_ref, s_ref, x_ref):
    _decay_sums(la_ref[...], diff_ref, x_ref)
    kd = (k_ref[...].astype(f32) * jnp.exp(x_ref[N_META - 1:N_META, :] - x_ref[...])).astype(bf16)
    s_ref[0] = _dot(v_ref[...], kd, _TN)


def _gla_meta(proj, la2):
    diff = _gla_tables(N_META)[0][:N_META]
    mb = R_MAIN // N_META
    return pl.pallas_call(
        _gla_meta_kernel,
        grid=(N_HEADS,),
        in_specs=[
            pl.BlockSpec((N_META, DK), lambda h: (mb, COL_K // DK + h)),
            pl.BlockSpec((N_META, DV), lambda h: (mb, COL_V // DV + h)),
            pl.BlockSpec((N_META, LA_W), lambda h: (mb, h)),
            pl.BlockSpec(diff.shape, lambda h: (0, 0)),
        ],
        out_specs=pl.BlockSpec((1, DV, DK), lambda h: (h, 0, 0)),
        out_shape=jax.ShapeDtypeStruct((N_HEADS, DV, DK), f32),
        scratch_shapes=[pltpu.VMEM((N_META, DK), f32)],
        compiler_params=_params(1),
        name="gla_meta",
    )(proj, proj, la2, diff)


CONV_TR, CONV_TC = 1024, 1024
CONV_PAD = 8


def _conv_prompt_kernel(cb_ref, cc_ref, ch_ref, mc_ref, mh_ref, w_ref, y_ref, nb_ref, u_ref):
    t = pl.program_id(2)
    TR = CONV_TR

    @pl.when(t == 0)
    def _():
        mu = mc_ref[...].astype(f32) * mh_ref[...].astype(f32)
        u_ref[CONV_PAD - 2:CONV_PAD, :] = mu[N_META - 2:N_META, :]

    u = cc_ref[...].astype(f32) * ch_ref[...].astype(f32)
    u_ref[CONV_PAD:CONV_PAD + TR, :] = u
    w = w_ref[...]
    zc = (w[0:1, :] * u_ref[CONV_PAD - 2:CONV_PAD - 2 + TR, :]
          + w[1:2, :] * u_ref[CONV_PAD - 1:CONV_PAD - 1 + TR, :]
          + w[2:3, :] * u)
    y_ref[...] = (cb_ref[...].astype(f32) * zc).astype(bf16)
    last = u[TR - 2:TR, :]
    u_ref[CONV_PAD - 2:CONV_PAD, :] = last

    @pl.when(t == SEQ // TR - 1)
    def _():
        nb_ref[0] = last


def _conv_prompt(proj, conv_w):
    TR, TC = CONV_TR, CONV_TC
    nt = SEQ // TR
    mrow = R_MAIN // N_META
    return pl.pallas_call(
        _conv_prompt_kernel,
        grid=(BATCH, D_CONV // TC, nt),
        in_specs=[
            pl.BlockSpec((TR, TC), lambda b, j, t: (b * nt + t, COL_CB // TC + j)),
            pl.BlockSpec((TR, TC), lambda b, j, t: (b * nt + t, COL_CC // TC + j)),
            pl.BlockSpec((TR, TC), lambda b, j, t: (b * nt + t, COL_CH // TC + j)),
            pl.BlockSpec((N_META, TC), lambda b, j, t: (mrow, COL_CC // TC + j)),
            pl.BlockSpec((N_META, TC), lambda b, j, t: (mrow, COL_CH // TC + j)),
            pl.BlockSpec((CONV_WIDTH, TC), lambda b, j, t: (0, j)),
        ],
        out_specs=[
            pl.BlockSpec((TR, TC), lambda b, j, t: (b * nt + t, j)),
            pl.BlockSpec((1, CONV_WIDTH - 1, TC), lambda b, j, t: (b, 0, j)),
        ],
        out_shape=[jax.ShapeDtypeStruct((R_ALL, D_CONV), bf16),
                   jax.ShapeDtypeStruct((BATCH, CONV_WIDTH - 1, D_CONV), f32)],
        scratch_shapes=[pltpu.VMEM((CONV_PAD + TR, TC), f32)],
        compiler_params=_params(3),
        name="conv_prompt",
    )(proj, proj, proj, proj, proj, conv_w)


def _conv_sample_kernel(cb_ref, cc_ref, ch_ref, buf_ref, w_ref, y_in_ref, y_ref, nb_ref):
    del y_in_ref
    u = cc_ref[...].astype(f32) * ch_ref[...].astype(f32)
    w = w_ref[...]
    b0 = buf_ref[:, 0:D_CONV]
    b1 = buf_ref[:, D_CONV:2 * D_CONV]
    zc = w[0:1, :] * b0 + w[1:2, :] * b1 + w[2:3, :] * u
    y_ref[...] = (cb_ref[...].astype(f32) * zc).astype(bf16)
    nb_ref[:, 0:D_CONV] = b1
    nb_ref[:, D_CONV:2 * D_CONV] = u


def _conv_sample(proj, buf, conv_w, yc):
    rb = NP // DEC_BATCH
    return pl.pallas_call(
        _conv_sample_kernel,
        grid=(1,),
        in_specs=[
            pl.BlockSpec((DEC_BATCH, D_CONV), lambda i: (rb, COL_CB // D_CONV)),
            pl.BlockSpec((DEC_BATCH, D_CONV), lambda i: (rb, COL_CC // D_CONV)),
            pl.BlockSpec((DEC_BATCH, D_CONV), lambda i: (rb, COL_CH // D_CONV)),
            pl.BlockSpec((DEC_BATCH, 2 * D_CONV), lambda i: (0, 0)),
            pl.BlockSpec((CONV_WIDTH, D_CONV), lambda i: (0, 0)),
            pl.BlockSpec(memory_space=pl.ANY),
        ],
        out_specs=[
            pl.BlockSpec((DEC_BATCH, D_CONV), lambda i: (rb, 0)),
            pl.BlockSpec((DEC_BATCH, 2 * D_CONV), lambda i: (0, 0)),
        ],
        out_shape=[jax.ShapeDtypeStruct((R_ALL, D_CONV), bf16),
                   jax.ShapeDtypeStruct((DEC_BATCH, 2 * D_CONV), f32)],
        input_output_aliases={5: 0},
        compiler_params=_params(1),
        name="conv_sample",
    )(proj, proj, proj, buf, conv_w, yc)


OUT_TM, OUT_TN = 1664, 256


def _outproj_kernel(o_ref, y_ref, wo_ref, wy_ref, h_ref, s_ref):
    s_ref[...] = (ALPHA * h_ref[...] + _dot(o_ref[...], wo_ref[...].astype(bf16))
                  + _dot(y_ref[...], wy_ref[...].astype(bf16)))


def _outproj(og, yc, w_out, h):
    return pl.pallas_call(
        _outproj_kernel,
        grid=(R_MAIN // OUT_TM, D_MODEL // OUT_TN),
        in_specs=[
            pl.BlockSpec((OUT_TM, D_GLA_V), lambda i, j: (i, 0)),
            pl.BlockSpec((OUT_TM, D_CONV), lambda i, j: (i, 0)),
            pl.BlockSpec((D_GLA_V, OUT_TN), lambda i, j: (0, j)),
            pl.BlockSpec((D_CONV, OUT_TN), lambda i, j: (1, j)),
            pl.BlockSpec((OUT_TM, OUT_TN), lambda i, j: (i, j)),
        ],
        out_specs=pl.BlockSpec((OUT_TM, OUT_TN), lambda i, j: (i, j)),
        out_shape=jax.ShapeDtypeStruct((R_MAIN, D_MODEL), f32),
        compiler_params=_params(2),
        name="outproj",
    )(og, yc, w_out, w_out, h)


LN1_ROWS = 320


def _ln1_kernel(s_ref, g_ref, b_ref, h_ref, hb_ref):
    h = _layer_norm(s_ref[...], g_ref[...], b_ref[...])
    h_ref[...] = h
    hb_ref[...] = h.astype(bf16)


def _ln1(s1, g, b):
    return pl.pallas_call(
        _ln1_kernel,
        grid=(R_MAIN // LN1_ROWS,),
        in_specs=[
            pl.BlockSpec((LN1_ROWS, D_MODEL), lambda i: (i, 0)),
            pl.BlockSpec((1, D_MODEL), lambda i: (0, 0)),
            pl.BlockSpec((1, D_MODEL), lambda i: (0, 0)),
        ],
        out_specs=[
            pl.BlockSpec((LN1_ROWS, D_MODEL), lambda i: (i, 0)),
            pl.BlockSpec((LN1_ROWS, D_MODEL), lambda i: (i, 0)),
        ],
        out_shape=[jax.ShapeDtypeStruct((R_MAIN, D_MODEL), f32),
                   jax.ShapeDtypeStruct((R_MAIN, D_MODEL), bf16)],
        compiler_params=_params(1),
        name="ln1",
    )(s1, g, b)


LN2_ROWS = 512


def _ln2_kernel(sp_ref, ss_ref, g_ref, b_ref, yp_ref, ys_ref):
    yp_ref[...] = _layer_norm(sp_ref[...], g_ref[...], b_ref[...])

    @pl.when(pl.program_id(0) == 0)
    def _():
        ys_ref[...] = _layer_norm(ss_ref[...], g_ref[...], b_ref[...])


def _ln2(s2, g, b):
    return pl.pallas_call(
        _ln2_kernel,
        grid=(NP // LN2_ROWS,),
        in_specs=[
            pl.BlockSpec((LN2_ROWS, D_MODEL), lambda i: (i, 0)),
            pl.BlockSpec((DEC_BATCH, D_MODEL), lambda i: (NP // DEC_BATCH, 0)),
            pl.BlockSpec((1, D_MODEL), lambda i: (0, 0)),
            pl.BlockSpec((1, D_MODEL), lambda i: (0, 0)),
        ],
        out_specs=[
            pl.BlockSpec((LN2_ROWS, D_MODEL), lambda i: (i, 0)),
            pl.BlockSpec((DEC_BATCH, D_MODEL), lambda i: (0, 0)),
        ],
        out_shape=[jax.ShapeDtypeStruct((NP, D_MODEL), f32),
                   jax.ShapeDtypeStruct((DEC_BATCH, D_MODEL), f32)],
        compiler_params=_params(1),
        name="ln2",
    )(s2, s2, g, b)


FFN_TM, FFN_TN = 2080, 256


FFN_NT = D_FF // FFN_TN


def _ffn_up_kernel(h_ref, wg_ref, wu_ref, wd_ref, act_ref, wdb_ref):
    i = pl.program_id(0)
    j = pl.program_id(1)

    @pl.when(j < FFN_NT)
    def _():
        h = h_ref[...]
        a = _dot(h, wg_ref[...].astype(bf16))
        u = _dot(h, wu_ref[...].astype(bf16))
        act_ref[...] = (a * jax.nn.sigmoid(a) * u).astype(bf16)

    @pl.when(j >= FFN_NT)
    def _():
        act_ref[...] = jnp.zeros_like(act_ref)

    @pl.when(jnp.logical_and(i == 0, j < FFN_NT))
    def _():
        wdb_ref[...] = wd_ref[...].astype(bf16)

    @pl.when(jnp.logical_and(i == 0, j >= FFN_NT))
    def _():
        wdb_ref[...] = jnp.zeros_like(wdb_ref)


def _ffn_up(hb, wg, wu, wd):
    nt_pad = D_FF_PAD // FFN_TN
    jw = lambda j: jnp.minimum(j, FFN_NT - 1)
    return pl.pallas_call(
        _ffn_up_kernel,
        grid=(R_MAIN // FFN_TM, nt_pad),
        in_specs=[
            pl.BlockSpec((FFN_TM, D_MODEL), lambda i, j: (i, 0), pipeline_mode=pl.Buffered(1)),
            pl.BlockSpec((D_MODEL, FFN_TN), lambda i, j: (0, jw(j))),
            pl.BlockSpec((D_MODEL, FFN_TN), lambda i, j: (0, jw(j))),
            pl.BlockSpec((FFN_TN, D_MODEL), lambda i, j: (jnp.where(i == 0, jw(j), FFN_NT - 1), 0)),
        ],
        out_specs=[
            pl.BlockSpec((FFN_TM, FFN_TN), lambda i, j: (i, j)),
            pl.BlockSpec((FFN_TN, D_MODEL), lambda i, j: (jnp.where(i == 0, j, nt_pad - 1), 0)),
        ],
        out_shape=[jax.ShapeDtypeStruct((R_MAIN, D_FF_PAD), bf16),
                   jax.ShapeDtypeStruct((D_FF_PAD, D_MODEL), bf16)],
        compiler_params=_params(2),
        name="ffn_up",
    )(hb, wg, wu, wd)


DOWN_TM, DOWN_TN, DOWN_TK = 2080, 1024, 1024


def _ffn_down_kernel(a_ref, w_ref, h_ref, s_ref):
    @pl.when(pl.program_id(2) == 0)
    def _():
        s_ref[...] = ALPHA * h_ref[...]

    s_ref[...] += _dot(a_ref[...], w_ref[...])


def _ffn_down(act, wd, h1):
    return pl.pallas_call(
        _ffn_down_kernel,
        grid=(R_MAIN // DOWN_TM, D_MODEL // DOWN_TN, D_FF_PAD // DOWN_TK),
        in_specs=[
            pl.BlockSpec((DOWN_TM, DOWN_TK), lambda i, j, k: (i, k)),
            pl.BlockSpec((DOWN_TK, DOWN_TN), lambda i, j, k: (k, j)),
            pl.BlockSpec((DOWN_TM, DOWN_TN), lambda i, j, k: (i, j)),
        ],
        out_specs=pl.BlockSpec((DOWN_TM, DOWN_TN), lambda i, j, k: (i, j)),
        out_shape=jax.ShapeDtypeStruct((R_MAIN, D_MODEL), f32),
        compiler_params=_params(3),
        name="ffn_down",
    )(act, wd, h1)


def kernel(x_prompt, x_sample, state_gla, state_conv, meta_tokens, emb_ln_g, emb_ln_b,
           w_in, w_a2, b_a, gla_norm_g, conv_w, w_out, ln1_g, ln1_b,
           w_ffn_gate, w_ffn_up, w_ffn_down, ln2_g, ln2_b):
    assert x_prompt.shape == (BATCH, SEQ, D_MODEL) and x_sample.shape == (DEC_BATCH, 1, D_MODEL)
    assert w_in.shape[0] == 1, "single layer"
    row = lambda v: v.reshape(1, -1)

    w_in_t = jnp.transpose(w_in[0])
    wa2p = jnp.pad(w_a2[0], ((0, LANES - GATE_RANK), (0, 0)))

    xp = x_prompt.reshape(NP, D_MODEL)
    xt = jnp.concatenate([x_sample.reshape(DEC_BATCH, D_MODEL), meta_tokens.astype(f32),
                          jnp.zeros((LN_ROWS - DEC_BATCH - N_META, D_MODEL), f32)], axis=0)

    h, hb, alr = _ln0(xp, xt, row(emb_ln_g), row(emb_ln_b), w_in_t)
    proj, la2 = _proj(hb, w_in_t, alr, wa2p, row(b_a[0]))

    ng = row(gla_norm_g[0])
    s_meta = _gla_meta(proj, la2)
    og, s_p, og_s, s_s = _gla(proj, la2, ng, s_meta, state_gla[0])
    og = lax.dynamic_update_slice(og, og_s, (NP, 0))
    yc, nb_p = _conv_prompt(proj, conv_w[0])
    yc, nb_s = _conv_sample(proj, state_conv[0].reshape(DEC_BATCH, 2 * D_CONV), conv_w[0], yc)

    s1 = _outproj(og, yc, w_out[0], h)
    h1, h1b = _ln1(s1, row(ln1_g[0]), row(ln1_b[0]))
    act, wd = _ffn_up(h1b, w_ffn_gate[0], w_ffn_up[0], w_ffn_down[0])
    s2 = _ffn_down(act, wd, h1)
    y_p, y_s = _ln2(s2, row(ln2_g[0]), row(ln2_b[0]))

    return (y_p.reshape(BATCH, SEQ, D_MODEL),
            y_s.reshape(DEC_BATCH, 1, D_MODEL),
            s_p[None],
            nb_p[None],
            s_s[None],
            nb_s.reshape(1, DEC_BATCH, CONV_WIDTH - 1, D_CONV))
```

```python
import jax
import jax.numpy as jnp
import numpy as np
from jax import lax
from jax.experimental import pallas as pl
from jax.experimental.pallas import tpu as pltpu

f32 = jnp.float32
bf16 = jnp.bfloat16

D_MODEL = 4096
BATCH = 4
SEQ = 2048
DEC_BATCH = 128
N_META = 16
D_GLA_V = D_MODEL // 2
D_CONV = D_MODEL - D_GLA_V
D_GLA_K = D_GLA_V // 2
N_HEADS = 4
DK = D_GLA_K // N_HEADS
DV = D_GLA_V // N_HEADS
GATE_RANK = 16
GATE_TAU = 16.0
CONV_WIDTH = 3
D_FF = 11008
ALPHA = 2.0 ** 0.25
LN_EPS = 1e-5
RMS_EPS = 1e-6

LANES = 128
NP = BATCH * SEQ
R_MAIN = NP + DEC_BATCH
R_ALL = R_MAIN + LANES
D_PROJ = 2 * D_GLA_K + 2 * D_GLA_V + 3 * D_CONV
D_FF_PAD = 11264
CHUNK = 128
N_CHUNKS = SEQ // CHUNK
VMEM_LIMIT = 56 * 1024 * 1024

COL_Q, COL_K, COL_V, COL_G = 0, D_GLA_K, 2 * D_GLA_K, 2 * D_GLA_K + D_GLA_V
COL_CB = COL_G + D_GLA_V
COL_CC = COL_CB + D_CONV
COL_CH = COL_CC + D_CONV


def _params(n_axes):
    return pltpu.CompilerParams(
        dimension_semantics=("arbitrary",) * n_axes, vmem_limit_bytes=VMEM_LIMIT)


def _ln_stats(x):
    mu = jnp.mean(x, axis=-1, keepdims=True)
    xc = x - mu
    var = jnp.mean(xc * xc, axis=-1, keepdims=True)
    return mu, lax.rsqrt(var + LN_EPS)


def _layer_norm(x, g, b):
    mu, rs = _ln_stats(x)
    return (x - mu) * rs * g + b


def _ln_from_stats(x, mu_ref, rs_ref, g, b):
    reps = x.shape[-1] // LANES
    mu = jnp.concatenate([mu_ref[...]] * reps, axis=1)
    rs = jnp.concatenate([rs_ref[...]] * reps, axis=1)
    return (x - mu) * rs * g + b


def _store_stats(mu, rs, mu_ref, rs_ref):
    mu_ref[...] = jnp.broadcast_to(mu, mu_ref.shape)
    rs_ref[...] = jnp.broadcast_to(rs, rs_ref.shape)


def _split3(x):
    hi = x.astype(bf16)
    r = x - hi.astype(f32)
    mid = r.astype(bf16)
    lo = (r - mid.astype(f32)).astype(bf16)
    return hi, mid, lo


def _dot(a, b, dims=(((1,), (0,)), ((), ()))):
    return lax.dot_general(a, b, dims, preferred_element_type=f32)


_NT = (((1,), (1,)), ((), ()))
_TN = (((0,), (0,)), ((), ()))


def _dot_sel(sel_bf16, x_f32, dims=(((1,), (0,)), ((), ()))):
    hi, mid, lo = _split3(x_f32)
    return _dot(sel_bf16, hi, dims) + _dot(sel_bf16, mid, dims) + _dot(sel_bf16, lo, dims)


def _dot_f32(a, b):
    ah, am, _ = _split3(a)
    bh, bm, _ = _split3(b)
    return _dot(ah, bh) + _dot(ah, bm) + _dot(am, bh)


def _log2(n):
    assert n & (n - 1) == 0
    return n.bit_length() - 1


def _log_sigmoid(z):
    return jnp.minimum(z, 0.0) - jnp.log(1.0 + jnp.exp(-jnp.abs(z)))


LN_ROWS = 256


OUT_TM, OUT_TN = 1664, 256
TAIL_ROW0 = (R_MAIN // OUT_TM - 1) * OUT_TM
assert TAIL_ROW0 <= NP and TAIL_ROW0 % LN_ROWS == 0


def _ln0_kernel(xp_ref, xt_ref, g_ref, b_ref, wa_ref,
                hb_ref, alr_ref, mu_ref, rs_ref, ht_ref, wab_ref):
    i = pl.program_id(0)

    @pl.when(i == 0)
    def _():
        wab_ref[...] = wa_ref[...].astype(bf16)

    def emit(x):
        mu, rs = _ln_stats(x)
        h = (x - mu) * rs * g_ref[...] + b_ref[...]
        hb = h.astype(bf16)
        hb_ref[...] = hb
        alr_ref[...] = _dot(hb, wab_ref[...], _NT)
        _store_stats(mu, rs, mu_ref, rs_ref)
        ht_ref[...] = h

    @pl.when(i < NP // LN_ROWS)
    def _():
        emit(xp_ref[...])

    @pl.when(i >= NP // LN_ROWS)
    def _():
        emit(xt_ref[...])


def _ln0(xp, xt, g, b, w_in_t):
    n_p = NP // LN_ROWS
    t0 = TAIL_ROW0 // LN_ROWS
    return pl.pallas_call(
        _ln0_kernel,
        grid=(R_ALL // LN_ROWS,),
        in_specs=[
            pl.BlockSpec((LN_ROWS, D_MODEL), lambda i: (jnp.minimum(i, n_p - 1), 0)),
            pl.BlockSpec((LN_ROWS, D_MODEL), lambda i: (0, 0)),
            pl.BlockSpec((1, D_MODEL), lambda i: (0, 0)),
            pl.BlockSpec((1, D_MODEL), lambda i: (0, 0)),
            pl.BlockSpec((LANES, D_MODEL), lambda i: (COL_CB // LANES, 0)),
        ],
        out_specs=[
            pl.BlockSpec((LN_ROWS, D_MODEL), lambda i: (i, 0)),
            pl.BlockSpec((LN_ROWS, LANES), lambda i: (i, 0)),
            pl.BlockSpec((LN_ROWS, LANES), lambda i: (i, 0)),
            pl.BlockSpec((LN_ROWS, LANES), lambda i: (i, 0)),
            pl.BlockSpec((LN_ROWS, D_MODEL), lambda i: (jnp.maximum(i - t0, 0), 0)),
        ],
        out_shape=[jax.ShapeDtypeStruct((R_ALL, D_MODEL), bf16),
                   jax.ShapeDtypeStruct((R_ALL, LANES), f32),
                   jax.ShapeDtypeStruct((R_ALL, LANES), f32),
                   jax.ShapeDtypeStruct((R_ALL, LANES), f32),
                   jax.ShapeDtypeStruct((R_ALL - TAIL_ROW0, D_MODEL), f32)],
        scratch_shapes=[pltpu.VMEM((LANES, D_MODEL), bf16)],
        compiler_params=_params(1),
        name="ln0",
    )(xp, xt, g, b, w_in_t)


PROJ_TM, PROJ_TN = 1408, 512


LA_W = 2 * DK


def _proj_kernel(h_ref, w_ref, alr_ref, wa2_ref, ba_ref, p_ref, la_ref):
    j = pl.program_id(1)

    @pl.when(j < N_HEADS)
    def _():
        p_ref[...] = _dot(h_ref[...], w_ref[...].astype(bf16), _NT).astype(bf16)
        la = _gate_log_decay(alr_ref[...], wa2_ref[...], ba_ref[...])
        hi = la.astype(bf16)
        la_ref[:, 0:DK] = hi
        la_ref[:, DK:LA_W] = (la - hi.astype(f32)).astype(bf16)

    is_g = jnp.logical_and(j * PROJ_TN >= COL_G, j * PROJ_TN < COL_CB)

    @pl.when(is_g)
    def _():
        p = _dot(h_ref[...], w_ref[...].astype(bf16), _NT)
        p_ref[...] = (p * jax.nn.sigmoid(p)).astype(bf16)

    @pl.when(jnp.logical_and(j >= N_HEADS, jnp.logical_not(is_g)))
    def _():
        p_ref[...] = _dot(h_ref[...], w_ref[...].astype(bf16), _NT).astype(bf16)


def _proj(hb, w_in_t, alr, wa2p, ba):
    assert COL_G >= N_HEADS * PROJ_TN

    def w_row(j):
        return pl.multiple_of(j * PROJ_TN + jnp.where(j * PROJ_TN >= COL_CB, GATE_RANK, 0), GATE_RANK)

    jh = lambda j: jnp.minimum(j, N_HEADS - 1)
    return pl.pallas_call(
        _proj_kernel,
        grid=(R_ALL // PROJ_TM, D_PROJ // PROJ_TN),
        in_specs=[
            pl.BlockSpec((PROJ_TM, D_MODEL), lambda i, j: (i, 0)),
            pl.BlockSpec((pl.Element(PROJ_TN), pl.Element(D_MODEL)), lambda i, j: (w_row(j), 0)),
            pl.BlockSpec((PROJ_TM, LANES), lambda i, j: (i, 0)),
            pl.BlockSpec((LANES, DK), lambda i, j: (0, jh(j))),
            pl.BlockSpec((1, DK), lambda i, j: (0, jh(j))),
        ],
        out_specs=[
            pl.BlockSpec((PROJ_TM, PROJ_TN), lambda i, j: (i, j)),
            pl.BlockSpec((PROJ_TM, LA_W), lambda i, j: (i, jh(j))),
        ],
        out_shape=[jax.ShapeDtypeStruct((R_ALL, D_PROJ), bf16),
                   jax.ShapeDtypeStruct((R_ALL, N_HEADS * LA_W), bf16)],
        compiler_params=_params(2),
        name="proj",
    )(hb, w_in_t, alr, wa2p, ba)


GLA_BLK = 128
GLA_HPS = 4
LVL_DIAG = _log2(GLA_BLK)


def _gla_levels(c):
    return [c >> (t + 1) for t in range(_log2(c))]


def _gla_tables(c):
    idx = np.arange(c)
    tri = (idx[None, :] <= idx[:, None]).astype(np.float32)
    mats = [tri]
    for m in _gla_levels(c):
        ref = ((idx // m) | 1) * m - 1
        mats.append(tri - tri[ref])
    diff = np.concatenate(mats, axis=0)
    diff = np.concatenate([diff, diff], axis=1)
    t = np.arange(min(c, GLA_BLK))
    level = np.full((t.size, t.size), -1, np.int32)
    for m in _gla_levels(t.size):
        same_pair = (t[:, None] // (2 * m)) == (t[None, :] // (2 * m))
        odd_even = (((t[:, None] // m) & 1) == 1) & (((t[None, :] // m) & 1) == 0)
        level[same_pair & odd_even] = _log2(m)
    level[t[:, None] == t[None, :]] = LVL_DIAG
    return jnp.asarray(diff, bf16), jnp.asarray(level)


def _gate_log_decay(alr, wa2, ba):
    return _log_sigmoid(_dot_f32(alr, wa2) + ba) * (1.0 / GATE_TAU)


def _decay_sums(la2, diff_ref, x_ref):
    x_ref[...] = _dot(diff_ref[...], jnp.concatenate([la2[:, 0:DK], la2[:, DK:LA_W]], axis=0))


GLA_STEPS = BATCH * (N_HEADS // GLA_HPS) * N_CHUNKS
SAMPLE_PER_STEP = DEC_BATCH // GLA_STEPS


def _gla_kernel(q_ref, k_ref, v_ref, g_ref, la_ref, ng_ref, s0_ref, diff_ref, lvl_ref,
                qs_ref, ks_ref, vs_ref, gs_ref, las_ref, s0s_ref,
                og_ref, sfin_ref, ogs_ref, ss_ref, st_ref, x_ref, ogacc_ref):
    s = pl.program_id(2)
    step = (pl.program_id(0) * pl.num_programs(1) + pl.program_id(1)) * N_CHUNKS + s

    @pl.when(s == 0)
    def _():
        st_ref[...] = s0_ref[...]

    @pl.when(step == 0)
    def _():
        ogacc_ref[...] = jnp.zeros_like(ogacc_ref)

    for hh in range(GLA_HPS):
        _gla_prompt_head(hh, q_ref, k_ref, v_ref, g_ref, la_ref, ng_ref,
                         diff_ref, lvl_ref, og_ref, st_ref, x_ref.at[hh])
    for t in range(SAMPLE_PER_STEP):
        for hh in range(N_HEADS):
            _gla_sample_token(step * SAMPLE_PER_STEP + t, t, hh, qs_ref, ks_ref, vs_ref, gs_ref,
                              las_ref, ng_ref, s0s_ref, ss_ref, ogacc_ref)

    @pl.when(s == N_CHUNKS - 1)
    def _():
        for hh in range(GLA_HPS):
            sfin_ref[0, hh] = st_ref[hh].T

    @pl.when(step == GLA_STEPS - 1)
    def _():
        ogs_ref[...] = ogacc_ref[...].astype(bf16)


def _gla_sample_token(r, t, hh, qs_ref, ks_ref, vs_ref, gs_ref, las_ref, ng_ref, s0s_ref, ss_ref,
                      ogacc_ref):
    kcols = slice(hh * DK, (hh + 1) * DK)
    vcols = slice(hh * DV, (hh + 1) * DV)
    as_col = (lax.broadcasted_iota(jnp.int32, (DEC_BATCH, LANES), 0) == r).astype(bf16)
    as_row = (lax.broadcasted_iota(jnp.int32, (8, DEC_BATCH), 1) == r).astype(bf16)
    la_col = (_dot(las_ref[:, hh * LA_W:hh * LA_W + DK], as_col, _TN)
              + _dot(las_ref[:, hh * LA_W + DK:(hh + 1) * LA_W], as_col, _TN))
    a_col = jnp.exp(la_col)
    k_col = _dot(ks_ref[:, kcols], as_col, _TN)
    q_col = _dot(qs_ref[:, kcols], as_col, _TN) * (DK ** -0.5)
    v_row = _dot(as_row, vs_ref[:, vcols])[0:1]
    g_row = _dot(as_row, gs_ref[:, vcols])[0:1]
    wide = lambda c: jnp.concatenate([c] * (DV // LANES), axis=1)
    s_new = wide(a_col) * s0s_ref[t, hh] + wide(k_col) * v_row
    ss_ref[t, hh] = s_new
    o = jnp.sum(wide(q_col) * s_new, axis=0, keepdims=True)
    ms = jnp.mean(o * o, axis=-1, keepdims=True)
    og_row = o * lax.rsqrt(ms + RMS_EPS) * ng_ref[...] * g_row
    rows = lax.broadcasted_iota(jnp.int32, (DEC_BATCH, 1), 0)
    ogacc_ref[:, vcols] = jnp.where(rows == r, og_row, ogacc_ref[:, vcols])


def _gla_prompt_head(hh, q_ref, k_ref, v_ref, g_ref, la_ref, ng_ref,
                     diff_ref, lvl_ref, og_ref, st_ref, x_ref):
    C = CHUNK
    nblk = C // GLA_BLK
    kcols = slice(hh * DK, (hh + 1) * DK)
    vcols = slice(hh * DV, (hh + 1) * DV)

    _decay_sums(la_ref[:, hh * LA_W:(hh + 1) * LA_W], diff_ref, x_ref)
    kb = k_ref[:, kcols]
    kf = kb.astype(f32)
    vb = v_ref[:, vcols]
    qf = q_ref[:, kcols].astype(f32) * (DK ** -0.5)
    qb = qf.astype(bf16)
    b = x_ref[0:C, :]
    b_last = x_ref[C - 1:C, :]

    st = st_ref[hh]
    qe = (qf * jnp.exp(b)).astype(bf16)
    o_state = _dot(qe, st.astype(bf16), _NT)

    lvl = lvl_ref[...]
    blk_rows = lambda a, i: a[i * GLA_BLK:(i + 1) * GLA_BLK]
    tiles = [jnp.where(lvl == LVL_DIAG, _dot(blk_rows(qb, i), blk_rows(kb, i), _NT), 0.0)
             for i in range(nblk)]
    cross = None
    row = lax.broadcasted_iota(jnp.int32, (C, 1), 0)
    for t, m in enumerate(_gla_levels(C)):
        e = jnp.exp(-jnp.abs(x_ref[(t + 1) * C:(t + 2) * C, :]))
        if m == GLA_BLK:
            qt = (blk_rows(qf, 1) * blk_rows(e, 1)).astype(bf16)
            kt = (blk_rows(kf, 0) * blk_rows(e, 0)).astype(bf16)
            cross = _dot(qt, kt, _NT)
            continue
        if m >= 8:
            n = C // (2 * m)
            e3, q3, k3 = (a.reshape(n, 2 * m, DK) for a in (e, qf, kf))
            zero = jnp.zeros((n, m, DK), f32)
            qt = jnp.concatenate([zero, q3[:, m:, :] * e3[:, m:, :]], axis=1).reshape(C, DK)
            kt = jnp.concatenate([k3[:, :m, :] * e3[:, :m, :], zero], axis=1).reshape(C, DK)
        else:
            odd = ((row >> _log2(m)) & 1) == 1
            qt = jnp.where(odd, qf * e, 0.0)
            kt = jnp.where(odd, 0.0, kf * e)
        qt, kt = qt.astype(bf16), kt.astype(bf16)
        tiles = [jnp.where(lvl == _log2(m), _dot(blk_rows(qt, i), blk_rows(kt, i), _NT), tiles[i])
                 for i in range(nblk)]

    outs = []
    for i in range(nblk):
        lhs = tiles[i] if i == 0 else jnp.concatenate([cross, tiles[i]], axis=1)
        outs.append(_dot(lhs.astype(bf16), vb[0:(i + 1) * GLA_BLK]))
    o = o_state + jnp.concatenate(outs, axis=0)
    ms = jnp.mean(o * o, axis=-1, keepdims=True)
    on = o * lax.rsqrt(ms + RMS_EPS) * ng_ref[...]
    og_ref[:, vcols] = (on * g_ref[:, vcols].astype(f32)).astype(bf16)

    kd = (kf * jnp.exp(b_last - b)).astype(bf16)
    st_new = st * jnp.exp(b_last) + _dot(vb, kd, _TN)
    st_ref[hh] = st_new


def _gla(proj, la2, ng, s_meta, s0_sample):
    assert CHUNK in (GLA_BLK, 2 * GLA_BLK) and N_HEADS % GLA_HPS == 0
    assert DEC_BATCH % GLA_STEPS == 0
    diff, level = _gla_tables(CHUNK)
    hps, sps = GLA_HPS, SAMPLE_PER_STEP
    nh = N_HEADS // hps
    rb = lambda b, s: b * N_CHUNKS + s
    const = lambda b, h, s: (0, 0)
    step = lambda b, h, s: (b * nh + h) * N_CHUNKS + s
    srow = NP // DEC_BATCH
    scol = lambda col, width: (lambda b, h, s: (srow, col // width))
    return pl.pallas_call(
        _gla_kernel,
        grid=(BATCH, nh, N_CHUNKS),
        in_specs=[
            pl.BlockSpec((CHUNK, hps * DK), lambda b, h, s: (rb(b, s), COL_Q // (hps * DK) + h)),
            pl.BlockSpec((CHUNK, hps * DK), lambda b, h, s: (rb(b, s), COL_K // (hps * DK) + h)),
            pl.BlockSpec((CHUNK, hps * DV), lambda b, h, s: (rb(b, s), COL_V // (hps * DV) + h)),
            pl.BlockSpec((CHUNK, hps * DV), lambda b, h, s: (rb(b, s), COL_G // (hps * DV) + h)),
            pl.BlockSpec((CHUNK, hps * LA_W), lambda b, h, s: (rb(b, s), h)),
            pl.BlockSpec((1, DV), const),
            pl.BlockSpec((hps, DV, DK), lambda b, h, s: (h, 0, 0)),
            pl.BlockSpec(diff.shape, const),
            pl.BlockSpec(level.shape, const),
            pl.BlockSpec((DEC_BATCH, D_GLA_K), scol(COL_Q, D_GLA_K)),
            pl.BlockSpec((DEC_BATCH, D_GLA_K), scol(COL_K, D_GLA_K)),
            pl.BlockSpec((DEC_BATCH, D_GLA_V), scol(COL_V, D_GLA_V)),
            pl.BlockSpec((DEC_BATCH, D_GLA_V), scol(COL_G, D_GLA_V)),
            pl.BlockSpec((DEC_BATCH, N_HEADS * LA_W), lambda b, h, s: (srow, 0)),
            pl.BlockSpec((sps, N_HEADS, DK, DV), lambda b, h, s: (step(b, h, s), 0, 0, 0)),
        ],
        out_specs=[
            pl.BlockSpec((CHUNK, hps * DV), lambda b, h, s: (rb(b, s), h)),
            pl.BlockSpec((1, hps, DK, DV), lambda b, h, s: (b, h, 0, 0)),
            pl.BlockSpec((DEC_BATCH, D_GLA_V), const),
            pl.BlockSpec((sps, N_HEADS, DK, DV), lambda b, h, s: (step(b, h, s), 0, 0, 0)),
        ],
        out_shape=[jax.ShapeDtypeStruct((R_ALL, D_GLA_V), bf16),
                   jax.ShapeDtypeStruct((BATCH, N_HEADS, DK, DV), f32),
                   jax.ShapeDtypeStruct((DEC_BATCH, D_GLA_V), bf16),
                   jax.ShapeDtypeStruct((DEC_BATCH, N_HEADS, DK, DV), f32)],
        scratch_shapes=[pltpu.VMEM((hps, DV, DK), f32),
                        pltpu.VMEM((hps, diff.shape[0], DK), f32),
                        pltpu.VMEM((DEC_BATCH, D_GLA_V), f32)],
        compiler_params=_params(3),
        name="gla",
    )(proj, proj, proj, proj, la2, ng, s_meta, diff, level,
      proj, proj, proj, proj, la2, s0_sample)


def _gla_meta_kernel(k_ref, v_ref, la_ref, diff_ref, s_ref, x_ref):
    _decay_sums(la_ref[...], diff_ref, x_ref)
    kd = (k_ref[...].astype(f32) * jnp.exp(x_ref[N_META - 1:N_META, :] - x_ref[...])).astype(bf16)
    s_ref[0] = _dot(v_ref[...], kd, _TN)


def _gla_meta(proj, la2):
    diff = _gla_tables(N_META)[0][:N_META]
    mb = R_MAIN // N_META
    return pl.pallas_call(
        _gla_meta_kernel,
        grid=(N_HEADS,),
        in_specs=[
            pl.BlockSpec((N_META, DK), lambda h: (mb, COL_K // DK + h)),
            pl.BlockSpec((N_META, DV), lambda h: (mb, COL_V // DV + h)),
            pl.BlockSpec((N_META, LA_W), lambda h: (mb, h)),
            pl.BlockSpec(diff.shape, lambda h: (0, 0)),
        ],
        out_specs=pl.BlockSpec((1, DV, DK), lambda h: (h, 0, 0)),
        out_shape=jax.ShapeDtypeStruct((N_HEADS, DV, DK), f32),
        scratch_shapes=[pltpu.VMEM((N_META, DK), f32)],
        compiler_params=_params(1),
        name="gla_meta",
    )(proj, proj, la2, diff)


CONV_TR, CONV_TC = 1024, 1024
CONV_PAD = 8


def _conv_prompt_kernel(cb_ref, cc_ref, ch_ref, mc_ref, mh_ref, w_ref, y_ref, nb_ref, u_ref):
    t = pl.program_id(2)
    TR = CONV_TR

    @pl.when(t == 0)
    def _():
        mu = mc_ref[...].astype(f32) * mh_ref[...].astype(f32)
        u_ref[CONV_PAD - 2:CONV_PAD, :] = mu[N_META - 2:N_META, :]

    u = cc_ref[...].astype(f32) * ch_ref[...].astype(f32)
    u_ref[CONV_PAD:CONV_PAD + TR, :] = u
    w = w_ref[...]
    zc = (w[0:1, :] * u_ref[CONV_PAD - 2:CONV_PAD - 2 + TR, :]
          + w[1:2, :] * u_ref[CONV_PAD - 1:CONV_PAD - 1 + TR, :]
          + w[2:3, :] * u)
    y_ref[...] = (cb_ref[...].astype(f32) * zc).astype(bf16)
    last = u[TR - 2:TR, :]
    u_ref[CONV_PAD - 2:CONV_PAD, :] = last

    @pl.when(t == SEQ // TR - 1)
    def _():
        nb_ref[0] = last


def _conv_prompt(proj, conv_w):
    TR, TC = CONV_TR, CONV_TC
    nt = SEQ // TR
    mrow = R_MAIN // N_META
    return pl.pallas_call(
        _conv_prompt_kernel,
        grid=(BATCH, D_CONV // TC, nt),
        in_specs=[
            pl.BlockSpec((TR, TC), lambda b, j, t: (b * nt + t, COL_CB // TC + j)),
            pl.BlockSpec((TR, TC), lambda b, j, t: (b * nt + t, COL_CC // TC + j)),
            pl.BlockSpec((TR, TC), lambda b, j, t: (b * nt + t, COL_CH // TC + j)),
            pl.BlockSpec((N_META, TC), lambda b, j, t: (mrow, COL_CC // TC + j)),
            pl.BlockSpec((N_META, TC), lambda b, j, t: (mrow, COL_CH // TC + j)),
            pl.BlockSpec((CONV_WIDTH, TC), lambda b, j, t: (0, j)),
        ],
        out_specs=[
            pl.BlockSpec((TR, TC), lambda b, j, t: (b * nt + t, j)),
            pl.BlockSpec((1, CONV_WIDTH - 1, TC), lambda b, j, t: (b, 0, j)),
        ],
        out_shape=[jax.ShapeDtypeStruct((R_ALL, D_CONV), bf16),
                   jax.ShapeDtypeStruct((BATCH, CONV_WIDTH - 1, D_CONV), f32)],
        scratch_shapes=[pltpu.VMEM((CONV_PAD + TR, TC), f32)],
        compiler_params=_params(3),
        name="conv_prompt",
    )(proj, proj, proj, proj, proj, conv_w)


def _conv_sample_kernel(cb_ref, cc_ref, ch_ref, buf_ref, w_ref, y_in_ref, y_ref, nb_ref):
    del y_in_ref
    u = cc_ref[...].astype(f32) * ch_ref[...].astype(f32)
    w = w_ref[...]
    b0 = buf_ref[:, 0:D_CONV]
    b1 = buf_ref[:, D_CONV:2 * D_CONV]
    zc = w[0:1, :] * b0 + w[1:2, :] * b1 + w[2:3, :] * u
    y_ref[...] = (cb_ref[...].astype(f32) * zc).astype(bf16)
    nb_ref[:, 0:D_CONV] = b1
    nb_ref[:, D_CONV:2 * D_CONV] = u


def _conv_sample(proj, buf, conv_w, yc):
    rb = NP // DEC_BATCH
    return pl.pallas_call(
        _conv_sample_kernel,
        grid=(1,),
        in_specs=[
            pl.BlockSpec((DEC_BATCH, D_CONV), lambda i: (rb, COL_CB // D_CONV)),
            pl.BlockSpec((DEC_BATCH, D_CONV), lambda i: (rb, COL_CC // D_CONV)),
            pl.BlockSpec((DEC_BATCH, D_CONV), lambda i: (rb, COL_CH // D_CONV)),
            pl.BlockSpec((DEC_BATCH, 2 * D_CONV), lambda i: (0, 0)),
            pl.BlockSpec((CONV_WIDTH, D_CONV), lambda i: (0, 0)),
            pl.BlockSpec(memory_space=pl.ANY),
        ],
        out_specs=[
            pl.BlockSpec((DEC_BATCH, D_CONV), lambda i: (rb, 0)),
            pl.BlockSpec((DEC_BATCH, 2 * D_CONV), lambda i: (0, 0)),
        ],
        out_shape=[jax.ShapeDtypeStruct((R_ALL, D_CONV), bf16),
                   jax.ShapeDtypeStruct((DEC_BATCH, 2 * D_CONV), f32)],
        input_output_aliases={5: 0},
        compiler_params=_params(1),
        name="conv_sample",
    )(proj, proj, proj, buf, conv_w, yc)


OUT_FULL = TAIL_ROW0 // OUT_TM


def _outproj_kernel(o_ref, y_ref, wo_ref, wy_ref, x_ref, mu_ref, rs_ref, g_ref, b_ref, ht_ref,
                    s_ref):
    i = pl.program_id(0)

    def mix():
        return (_dot(o_ref[...], wo_ref[...].astype(bf16))
                + _dot(y_ref[...], wy_ref[...].astype(bf16)))

    @pl.when(i < OUT_FULL)
    def _():
        h = _ln_from_stats(x_ref[...], mu_ref, rs_ref, g_ref[...], b_ref[...])
        s_ref[...] = ALPHA * h + mix()

    @pl.when(i >= OUT_FULL)
    def _():
        s_ref[...] = ALPHA * ht_ref[...] + mix()


def _outproj(og, yc, w_out, xp, mu, rs, g, b, h_tail):
    full = lambda i: i < OUT_FULL
    return pl.pallas_call(
        _outproj_kernel,
        grid=(R_MAIN // OUT_TM, D_MODEL // OUT_TN),
        in_specs=[
            pl.BlockSpec((OUT_TM, D_GLA_V), lambda i, j: (i, 0)),
            pl.BlockSpec((OUT_TM, D_CONV), lambda i, j: (i, 0)),
            pl.BlockSpec((D_GLA_V, OUT_TN), lambda i, j: (0, j)),
            pl.BlockSpec((D_CONV, OUT_TN), lambda i, j: (1, j)),
            pl.BlockSpec((OUT_TM, OUT_TN),
                         lambda i, j: (jnp.minimum(i, OUT_FULL - 1), jnp.where(full(i), j, 0))),
            pl.BlockSpec((OUT_TM, LANES), lambda i, j: (i, 0)),
            pl.BlockSpec((OUT_TM, LANES), lambda i, j: (i, 0)),
            pl.BlockSpec((1, OUT_TN), lambda i, j: (0, j)),
            pl.BlockSpec((1, OUT_TN), lambda i, j: (0, j)),
            pl.BlockSpec((OUT_TM, OUT_TN), lambda i, j: (0, jnp.where(full(i), 0, j))),
        ],
        out_specs=pl.BlockSpec((OUT_TM, OUT_TN), lambda i, j: (i, j)),
        out_shape=jax.ShapeDtypeStruct((R_MAIN, D_MODEL), f32),
        compiler_params=_params(2),
        name="outproj",
    )(og, yc, w_out, w_out, xp, mu, rs, g, b, h_tail)


LN1_ROWS = 320


def _ln1_kernel(s_ref, g_ref, b_ref, hb_ref, mu_ref, rs_ref):
    x = s_ref[...]
    mu, rs = _ln_stats(x)
    hb_ref[...] = ((x - mu) * rs * g_ref[...] + b_ref[...]).astype(bf16)
    _store_stats(mu, rs, mu_ref, rs_ref)


def _ln1(s1, g, b):
    return pl.pallas_call(
        _ln1_kernel,
        grid=(R_MAIN // LN1_ROWS,),
        in_specs=[
            pl.BlockSpec((LN1_ROWS, D_MODEL), lambda i: (i, 0)),
            pl.BlockSpec((1, D_MODEL), lambda i: (0, 0)),
            pl.BlockSpec((1, D_MODEL), lambda i: (0, 0)),
        ],
        out_specs=[
            pl.BlockSpec((LN1_ROWS, D_MODEL), lambda i: (i, 0)),
            pl.BlockSpec((LN1_ROWS, LANES), lambda i: (i, 0)),
            pl.BlockSpec((LN1_ROWS, LANES), lambda i: (i, 0)),
        ],
        out_shape=[jax.ShapeDtypeStruct((R_MAIN, D_MODEL), bf16),
                   jax.ShapeDtypeStruct((R_MAIN, LANES), f32),
                   jax.ShapeDtypeStruct((R_MAIN, LANES), f32)],
        compiler_params=_params(1),
        name="ln1",
    )(s1, g, b)


LN2_ROWS = 512


def _ln2_kernel(sp_ref, ss_ref, g_ref, b_ref, yp_ref, ys_ref):
    yp_ref[...] = _layer_norm(sp_ref[...], g_ref[...], b_ref[...])

    @pl.when(pl.program_id(0) == 0)
    def _():
        ys_ref[...] = _layer_norm(ss_ref[...], g_ref[...], b_ref[...])


def _ln2(s2, g, b):
    return pl.pallas_call(
        _ln2_kernel,
        grid=(NP // LN2_ROWS,),
        in_specs=[
            pl.BlockSpec((LN2_ROWS, D_MODEL), lambda i: (i, 0)),
            pl.BlockSpec((DEC_BATCH, D_MODEL), lambda i: (NP // DEC_BATCH, 0)),
            pl.BlockSpec((1, D_MODEL), lambda i: (0, 0)),
            pl.BlockSpec((1, D_MODEL), lambda i: (0, 0)),
        ],
        out_specs=[
            pl.BlockSpec((LN2_ROWS, D_MODEL), lambda i: (i, 0)),
            pl.BlockSpec((DEC_BATCH, D_MODEL), lambda i: (0, 0)),
        ],
        out_shape=[jax.ShapeDtypeStruct((NP, D_MODEL), f32),
                   jax.ShapeDtypeStruct((DEC_BATCH, D_MODEL), f32)],
        compiler_params=_params(1),
        name="ln2",
    )(s2, s2, g, b)


FFN_TM, FFN_TN = 2080, 256


FFN_NT = D_FF // FFN_TN


def _ffn_up_kernel(h_ref, wg_ref, wu_ref, wd_ref, act_ref, wdb_ref):
    i = pl.program_id(0)
    j = pl.program_id(1)

    @pl.when(j < FFN_NT)
    def _():
        h = h_ref[...]
        a = _dot(h, wg_ref[...].astype(bf16))
        u = _dot(h, wu_ref[...].astype(bf16))
        act_ref[...] = (a * jax.nn.sigmoid(a) * u).astype(bf16)

    @pl.when(j >= FFN_NT)
    def _():
        act_ref[...] = jnp.zeros_like(act_ref)

    @pl.when(jnp.logical_and(i == 0, j < FFN_NT))
    def _():
        wdb_ref[...] = wd_ref[...].astype(bf16)

    @pl.when(jnp.logical_and(i == 0, j >= FFN_NT))
    def _():
        wdb_ref[...] = jnp.zeros_like(wdb_ref)


def _ffn_up(hb, wg, wu, wd):
    nt_pad = D_FF_PAD // FFN_TN
    jw = lambda j: jnp.minimum(j, FFN_NT - 1)
    return pl.pallas_call(
        _ffn_up_kernel,
        grid=(R_MAIN // FFN_TM, nt_pad),
        in_specs=[
            pl.BlockSpec((FFN_TM, D_MODEL), lambda i, j: (i, 0), pipeline_mode=pl.Buffered(1)),
            pl.BlockSpec((D_MODEL, FFN_TN), lambda i, j: (0, jw(j))),
            pl.BlockSpec((D_MODEL, FFN_TN), lambda i, j: (0, jw(j))),
            pl.BlockSpec((FFN_TN, D_MODEL), lambda i, j: (jnp.where(i == 0, jw(j), FFN_NT - 1), 0)),
        ],
        out_specs=[
            pl.BlockSpec((FFN_TM, FFN_TN), lambda i, j: (i, j)),
            pl.BlockSpec((FFN_TN, D_MODEL), lambda i, j: (jnp.where(i == 0, j, nt_pad - 1), 0)),
        ],
        out_shape=[jax.ShapeDtypeStruct((R_MAIN, D_FF_PAD), bf16),
                   jax.ShapeDtypeStruct((D_FF_PAD, D_MODEL), bf16)],
        compiler_params=_params(2),
        name="ffn_up",
    )(hb, wg, wu, wd)


DOWN_TM, DOWN_TN, DOWN_TK = 2080, 1024, 1024


def _ffn_down_kernel(a_ref, w_ref, s1_ref, mu_ref, rs_ref, g_ref, b_ref, s_ref):
    @pl.when(pl.program_id(2) == 0)
    def _():
        s_ref[...] = ALPHA * _ln_from_stats(s1_ref[...], mu_ref, rs_ref, g_ref[...], b_ref[...])

    s_ref[...] += _dot(a_ref[...], w_ref[...])


def _ffn_down(act, wd, s1, mu, rs, g, b):
    return pl.pallas_call(
        _ffn_down_kernel,
        grid=(R_MAIN // DOWN_TM, D_MODEL // DOWN_TN, D_FF_PAD // DOWN_TK),
        in_specs=[
            pl.BlockSpec((DOWN_TM, DOWN_TK), lambda i, j, k: (i, k)),
            pl.BlockSpec((DOWN_TK, DOWN_TN), lambda i, j, k: (k, j)),
            pl.BlockSpec((DOWN_TM, DOWN_TN), lambda i, j, k: (i, j)),
            pl.BlockSpec((DOWN_TM, LANES), lambda i, j, k: (i, 0)),
            pl.BlockSpec((DOWN_TM, LANES), lambda i, j, k: (i, 0)),
            pl.BlockSpec((1, DOWN_TN), lambda i, j, k: (0, j)),
            pl.BlockSpec((1, DOWN_TN), lambda i, j, k: (0, j)),
        ],
        out_specs=pl.BlockSpec((DOWN_TM, DOWN_TN), lambda i, j, k: (i, j)),
        out_shape=jax.ShapeDtypeStruct((R_MAIN, D_MODEL), f32),
        compiler_params=_params(3),
        name="ffn_down",
    )(act, wd, s1, mu, rs, g, b)


def kernel(x_prompt, x_sample, state_gla, state_conv, meta_tokens, emb_ln_g, emb_ln_b,
           w_in, w_a2, b_a, gla_norm_g, conv_w, w_out, ln1_g, ln1_b,
           w_ffn_gate, w_ffn_up, w_ffn_down, ln2_g, ln2_b):
    assert x_prompt.shape == (BATCH, SEQ, D_MODEL) and x_sample.shape == (DEC_BATCH, 1, D_MODEL)
    assert w_in.shape[0] == 1, "single layer"
    row = lambda v: v.reshape(1, -1)

    w_in_t = jnp.transpose(w_in[0])
    wa2p = jnp.pad(w_a2[0], ((0, LANES - GATE_RANK), (0, 0)))

    xp = x_prompt.reshape(NP, D_MODEL)
    xt = jnp.concatenate([x_sample.reshape(DEC_BATCH, D_MODEL), meta_tokens.astype(f32),
                          jnp.zeros((LN_ROWS - DEC_BATCH - N_META, D_MODEL), f32)], axis=0)

    g0, b0 = row(emb_ln_g), row(emb_ln_b)
    g1, b1 = row(ln1_g[0]), row(ln1_b[0])
    hb, alr, mu0, rs0, h_tail = _ln0(xp, xt, g0, b0, w_in_t)
    proj, la2 = _proj(hb, w_in_t, alr, wa2p, row(b_a[0]))

    ng = row(gla_norm_g[0])
    s_meta = _gla_meta(proj, la2)
    og, s_p, og_s, s_s = _gla(proj, la2, ng, s_meta, state_gla[0])
    og = lax.dynamic_update_slice(og, og_s, (NP, 0))
    yc, nb_p = _conv_prompt(proj, conv_w[0])
    yc, nb_s = _conv_sample(proj, state_conv[0].reshape(DEC_BATCH, 2 * D_CONV), conv_w[0], yc)

    s1 = _outproj(og, yc, w_out[0], xp, mu0, rs0, g0, b0, h_tail)
    h1b, mu1, rs1 = _ln1(s1, g1, b1)
    act, wd = _ffn_up(h1b, w_ffn_gate[0], w_ffn_up[0], w_ffn_down[0])
    s2 = _ffn_down(act, wd, s1, mu1, rs1, g1, b1)
    y_p, y_s = _ln2(s2, row(ln2_g[0]), row(ln2_b[0]))

    return (y_p.reshape(BATCH, SEQ, D_MODEL),
            y_s.reshape(DEC_BATCH, 1, D_MODEL),
            s_p[None],
            nb_p[None],
            s_s[None],
            nb_s.reshape(1, DEC_BATCH, CONV_WIDTH - 1, D_CONV))
```

```python
import jax
import jax.numpy as jnp
import numpy as np
from jax import lax
from jax.experimental import pallas as pl
from jax.experimental.pallas import tpu as pltpu

f32 = jnp.float32
bf16 = jnp.bfloat16

D_MODEL = 4096
BATCH = 4
SEQ = 2048
DEC_BATCH = 128
N_META = 16
D_GLA_V = D_MODEL // 2
D_CONV = D_MODEL - D_GLA_V
D_GLA_K = D_GLA_V // 2
N_HEADS = 4
DK = D_GLA_K // N_HEADS
DV = D_GLA_V // N_HEADS
GATE_RANK = 16
GATE_TAU = 16.0
CONV_WIDTH = 3
D_FF = 11008
ALPHA = 2.0 ** 0.25
LN_EPS = 1e-5
RMS_EPS = 1e-6

LANES = 128
NP = BATCH * SEQ
R_MAIN = NP + DEC_BATCH
R_ALL = R_MAIN + LANES
D_PROJ = 2 * D_GLA_K + 2 * D_GLA_V + 3 * D_CONV
D_FF_PAD = 11264
CHUNK = 128
N_CHUNKS = SEQ // CHUNK
VMEM_LIMIT = 58 * 1024 * 1024

COL_Q, COL_K, COL_V, COL_G = 0, D_GLA_K, 2 * D_GLA_K, 2 * D_GLA_K + D_GLA_V
COL_CB = COL_G + D_GLA_V
COL_CC = COL_CB + D_CONV
COL_CH = COL_CC + D_CONV


def _params(n_axes):
    return pltpu.CompilerParams(
        dimension_semantics=("arbitrary",) * n_axes, vmem_limit_bytes=VMEM_LIMIT)


def _ln_stats(x):
    mu = jnp.mean(x, axis=-1, keepdims=True)
    xc = x - mu
    var = jnp.mean(xc * xc, axis=-1, keepdims=True)
    return mu, lax.rsqrt(var + LN_EPS)


def _layer_norm(x, g, b):
    mu, rs = _ln_stats(x)
    return (x - mu) * rs * g + b


def _ln_from_stats(x, mu_ref, rs_ref, g, b):
    reps = x.shape[-1] // LANES
    mu = jnp.concatenate([mu_ref[...]] * reps, axis=1)
    rs = jnp.concatenate([rs_ref[...]] * reps, axis=1)
    return (x - mu) * rs * g + b


def _store_stats(mu, rs, mu_ref, rs_ref):
    mu_ref[...] = jnp.broadcast_to(mu, mu_ref.shape)
    rs_ref[...] = jnp.broadcast_to(rs, rs_ref.shape)


def _split3(x):
    hi = x.astype(bf16)
    r = x - hi.astype(f32)
    mid = r.astype(bf16)
    lo = (r - mid.astype(f32)).astype(bf16)
    return hi, mid, lo


def _dot(a, b, dims=(((1,), (0,)), ((), ()))):
    return lax.dot_general(a, b, dims, preferred_element_type=f32)


_NT = (((1,), (1,)), ((), ()))
_TN = (((0,), (0,)), ((), ()))


def _dot_sel(sel_bf16, x_f32, dims=(((1,), (0,)), ((), ()))):
    hi, mid, lo = _split3(x_f32)
    return _dot(sel_bf16, hi, dims) + _dot(sel_bf16, mid, dims) + _dot(sel_bf16, lo, dims)


def _dot_f32(a, b):
    ah, am, _ = _split3(a)
    bh, bm, _ = _split3(b)
    return _dot(ah, bh) + _dot(ah, bm) + _dot(am, bh)


def _log2(n):
    assert n & (n - 1) == 0
    return n.bit_length() - 1


def _log_sigmoid(z):
    return jnp.minimum(z, 0.0) - jnp.log(1.0 + jnp.exp(-jnp.abs(z)))


LN_ROWS = 256


OUT_TM, OUT_TN = 1664, 256
TAIL_ROW0 = (R_MAIN // OUT_TM - 1) * OUT_TM
assert TAIL_ROW0 <= NP and TAIL_ROW0 % LN_ROWS == 0


def _ln0_kernel(xp_ref, xt_ref, g_ref, b_ref, wa_ref,
                hb_ref, alr_ref, mu_ref, rs_ref, ht_ref, wab_ref):
    i = pl.program_id(0)

    @pl.when(i == 0)
    def _():
        wab_ref[...] = wa_ref[...].astype(bf16)

    def emit(x):
        mu, rs = _ln_stats(x)
        h = (x - mu) * rs * g_ref[...] + b_ref[...]
        hb = h.astype(bf16)
        hb_ref[...] = hb
        alr_ref[...] = _dot(hb, wab_ref[...], _NT)
        _store_stats(mu, rs, mu_ref, rs_ref)
        ht_ref[...] = h

    @pl.when(i < NP // LN_ROWS)
    def _():
        emit(xp_ref[...])

    @pl.when(i >= NP // LN_ROWS)
    def _():
        emit(xt_ref[...])


def _ln0(xp, xt, g, b, w_in_t):
    n_p = NP // LN_ROWS
    t0 = TAIL_ROW0 // LN_ROWS
    return pl.pallas_call(
        _ln0_kernel,
        grid=(R_ALL // LN_ROWS,),
        in_specs=[
            pl.BlockSpec((LN_ROWS, D_MODEL), lambda i: (jnp.minimum(i, n_p - 1), 0)),
            pl.BlockSpec((LN_ROWS, D_MODEL), lambda i: (0, 0)),
            pl.BlockSpec((1, D_MODEL), lambda i: (0, 0)),
            pl.BlockSpec((1, D_MODEL), lambda i: (0, 0)),
            pl.BlockSpec((LANES, D_MODEL), lambda i: (COL_CB // LANES, 0)),
        ],
        out_specs=[
            pl.BlockSpec((LN_ROWS, D_MODEL), lambda i: (i, 0)),
            pl.BlockSpec((LN_ROWS, LANES), lambda i: (i, 0)),
            pl.BlockSpec((LN_ROWS, LANES), lambda i: (i, 0)),
            pl.BlockSpec((LN_ROWS, LANES), lambda i: (i, 0)),
            pl.BlockSpec((LN_ROWS, D_MODEL), lambda i: (jnp.maximum(i - t0, 0), 0)),
        ],
        out_shape=[jax.ShapeDtypeStruct((R_ALL, D_MODEL), bf16),
                   jax.ShapeDtypeStruct((R_ALL, LANES), f32),
                   jax.ShapeDtypeStruct((R_ALL, LANES), f32),
                   jax.ShapeDtypeStruct((R_ALL, LANES), f32),
                   jax.ShapeDtypeStruct((R_ALL - TAIL_ROW0, D_MODEL), f32)],
        scratch_shapes=[pltpu.VMEM((LANES, D_MODEL), bf16)],
        compiler_params=_params(1),
        name="ln0",
    )(xp, xt, g, b, w_in_t)


PROJ_TM, PROJ_TN = 1408, 512


LA_W = 2 * DK


def _proj_kernel(h_ref, w_ref, alr_ref, wa2_ref, ba_ref, p_ref, la_ref):
    j = pl.program_id(1)

    @pl.when(j < N_HEADS)
    def _():
        p_ref[...] = _dot(h_ref[...], w_ref[...].astype(bf16), _NT).astype(bf16)
        la = _gate_log_decay(alr_ref[...], wa2_ref[...], ba_ref[...])
        hi = la.astype(bf16)
        la_ref[:, 0:DK] = hi
        la_ref[:, DK:LA_W] = (la - hi.astype(f32)).astype(bf16)

    is_g = jnp.logical_and(j * PROJ_TN >= COL_G, j * PROJ_TN < COL_CB)

    @pl.when(is_g)
    def _():
        p = _dot(h_ref[...], w_ref[...].astype(bf16), _NT)
        p_ref[...] = (p * jax.nn.sigmoid(p)).astype(bf16)

    @pl.when(jnp.logical_and(j >= N_HEADS, jnp.logical_not(is_g)))
    def _():
        p_ref[...] = _dot(h_ref[...], w_ref[...].astype(bf16), _NT).astype(bf16)


def _proj(hb, w_in_t, alr, wa2p, ba):
    assert COL_G >= N_HEADS * PROJ_TN

    def w_row(j):
        return pl.multiple_of(j * PROJ_TN + jnp.where(j * PROJ_TN >= COL_CB, GATE_RANK, 0), GATE_RANK)

    jh = lambda j: jnp.minimum(j, N_HEADS - 1)
    return pl.pallas_call(
        _proj_kernel,
        grid=(R_ALL // PROJ_TM, D_PROJ // PROJ_TN),
        in_specs=[
            pl.BlockSpec((PROJ_TM, D_MODEL), lambda i, j: (i, 0)),
            pl.BlockSpec((pl.Element(PROJ_TN), pl.Element(D_MODEL)), lambda i, j: (w_row(j), 0)),
            pl.BlockSpec((PROJ_TM, LANES), lambda i, j: (i, 0)),
            pl.BlockSpec((LANES, DK), lambda i, j: (0, jh(j))),
            pl.BlockSpec((1, DK), lambda i, j: (0, jh(j))),
        ],
        out_specs=[
            pl.BlockSpec((PROJ_TM, PROJ_TN), lambda i, j: (i, j)),
            pl.BlockSpec((PROJ_TM, LA_W), lambda i, j: (i, jh(j))),
        ],
        out_shape=[jax.ShapeDtypeStruct((R_ALL, D_PROJ), bf16),
                   jax.ShapeDtypeStruct((R_ALL, N_HEADS * LA_W), bf16)],
        compiler_params=_params(2),
        name="proj",
    )(hb, w_in_t, alr, wa2p, ba)


GLA_BLK = 128
GLA_HPS = 4
GLA_ILV = 2
LVL_DIAG = _log2(GLA_BLK)


def _gla_levels(c):
    return [c >> (t + 1) for t in range(_log2(c))]


def _gla_tables(c):
    idx = np.arange(c)
    tri = (idx[None, :] <= idx[:, None]).astype(np.float32)
    mats = [tri]
    for m in _gla_levels(c):
        ref = ((idx // m) | 1) * m - 1
        mats.append(tri - tri[ref])
    diff = np.concatenate(mats, axis=0)
    diff = np.concatenate([diff, diff], axis=1)
    t = np.arange(min(c, GLA_BLK))
    level = np.full((t.size, t.size), -1, np.int32)
    for m in _gla_levels(t.size):
        same_pair = (t[:, None] // (2 * m)) == (t[None, :] // (2 * m))
        odd_even = (((t[:, None] // m) & 1) == 1) & (((t[None, :] // m) & 1) == 0)
        level[same_pair & odd_even] = _log2(m)
    level[t[:, None] == t[None, :]] = LVL_DIAG
    return jnp.asarray(diff, bf16), jnp.asarray(level)


def _gate_log_decay(alr, wa2, ba):
    return _log_sigmoid(_dot_f32(alr, wa2) + ba) * (1.0 / GATE_TAU)


def _decay_sums(la2, diff_ref, x_ref):
    x_ref[...] = _dot(diff_ref[...], jnp.concatenate([la2[:, 0:DK], la2[:, DK:LA_W]], axis=0))


GLA_STEPS = BATCH * (N_HEADS // GLA_HPS) * N_CHUNKS
SAMPLE_PER_STEP = DEC_BATCH // GLA_STEPS


def _gla_kernel(q_ref, k_ref, v_ref, g_ref, la_ref, ng_ref, s0_ref, diff_ref, lvl_ref,
                qs_ref, ks_ref, vs_ref, gs_ref, las_ref, s0s_ref,
                og_ref, sfin_ref, ogs_ref, ss_ref, ogacc_ref, *head_scratch):
    s = pl.program_id(2)
    step = (pl.program_id(0) * pl.num_programs(1) + pl.program_id(1)) * N_CHUNKS + s
    st_refs, x_refs = head_scratch[:GLA_HPS], head_scratch[GLA_HPS:]

    @pl.when(s == 0)
    def _():
        for hh in range(GLA_HPS):
            st_refs[hh][...] = s0_ref[hh]

    @pl.when(step == 0)
    def _():
        ogacc_ref[...] = jnp.zeros_like(ogacc_ref)

    for h0 in range(0, GLA_HPS, GLA_ILV):
        heads = [_gla_prompt_head(hh, q_ref, k_ref, v_ref, g_ref, la_ref, ng_ref,
                                  diff_ref, lvl_ref, og_ref, st_refs[hh], x_refs[hh])
                 for hh in range(h0, h0 + GLA_ILV)]
        while heads:
            heads = [h for h in heads if next(h, _DONE) is not _DONE]

    rows = lax.broadcasted_iota(jnp.int32, (DEC_BATCH, 1), 0)
    for hh in range(N_HEADS):
        vcols = slice(hh * DV, (hh + 1) * DV)
        acc = ogacc_ref[:, vcols]
        for t in range(SAMPLE_PER_STEP):
            r = step * SAMPLE_PER_STEP + t
            og_row = _gla_sample_token(r, t, hh, qs_ref, ks_ref, vs_ref, gs_ref, las_ref, ng_ref,
                                       s0s_ref, ss_ref)
            acc = jnp.where(rows == r, og_row, acc)
        ogacc_ref[:, vcols] = acc

    @pl.when(s == N_CHUNKS - 1)
    def _():
        for hh in range(GLA_HPS):
            sfin_ref[0, hh] = st_refs[hh][...].T

    @pl.when(step == GLA_STEPS - 1)
    def _():
        ogs_ref[...] = ogacc_ref[...].astype(bf16)


def _gla_sample_token(r, t, hh, qs_ref, ks_ref, vs_ref, gs_ref, las_ref, ng_ref, s0s_ref, ss_ref):
    kcols = slice(hh * DK, (hh + 1) * DK)
    vcols = slice(hh * DV, (hh + 1) * DV)
    as_col = (lax.broadcasted_iota(jnp.int32, (DEC_BATCH, LANES), 0) == r).astype(bf16)
    as_row = (lax.broadcasted_iota(jnp.int32, (8, DEC_BATCH), 1) == r).astype(bf16)
    la_col = (_dot(las_ref[:, hh * LA_W:hh * LA_W + DK], as_col, _TN)
              + _dot(las_ref[:, hh * LA_W + DK:(hh + 1) * LA_W], as_col, _TN))
    a_col = jnp.exp(la_col)
    k_col = _dot(ks_ref[:, kcols], as_col, _TN)
    q_col = _dot(qs_ref[:, kcols], as_col, _TN) * (DK ** -0.5)
    v_row = _dot(as_row, vs_ref[:, vcols])[0:1]
    g_row = _dot(as_row, gs_ref[:, vcols])[0:1]
    wide = lambda c: jnp.concatenate([c] * (DV // LANES), axis=1)
    s_new = wide(a_col) * s0s_ref[t, hh] + wide(k_col) * v_row
    ss_ref[t, hh] = s_new
    o = jnp.sum(wide(q_col) * s_new, axis=0, keepdims=True)
    ms = jnp.mean(o * o, axis=-1, keepdims=True)
    return o * lax.rsqrt(ms + RMS_EPS) * ng_ref[...] * g_row


_DONE = object()


def _gla_prompt_head(hh, q_ref, k_ref, v_ref, g_ref, la_ref, ng_ref,
                     diff_ref, lvl_ref, og_ref, st_ref, x_ref):
    C = CHUNK
    nblk = C // GLA_BLK
    kcols = slice(hh * DK, (hh + 1) * DK)
    vcols = slice(hh * DV, (hh + 1) * DV)

    _decay_sums(la_ref[:, hh * LA_W:(hh + 1) * LA_W], diff_ref, x_ref)
    yield
    kb = k_ref[:, kcols]
    kf = kb.astype(f32)
    vb = v_ref[:, vcols]
    qf = q_ref[:, kcols].astype(f32) * (DK ** -0.5)
    qb = qf.astype(bf16)
    b = x_ref[0:C, :]
    b_last = x_ref[C - 1:C, :]

    lvl = lvl_ref[...]
    blk_rows = lambda a, i: a[i * GLA_BLK:(i + 1) * GLA_BLK]
    tiles = [jnp.where(lvl == LVL_DIAG, _dot(blk_rows(qb, i), blk_rows(kb, i), _NT), 0.0)
             for i in range(nblk)]
    cross = None
    row = lax.broadcasted_iota(jnp.int32, (C, 1), 0)
    for t, m in enumerate(_gla_levels(C)):
        yield
        e = jnp.exp(-jnp.abs(x_ref[(t + 1) * C:(t + 2) * C, :]))
        if m == GLA_BLK:
            qt = (blk_rows(qf, 1) * blk_rows(e, 1)).astype(bf16)
            kt = (blk_rows(kf, 0) * blk_rows(e, 0)).astype(bf16)
            cross = _dot(qt, kt, _NT)
            continue
        if m >= 8:
            n = C // (2 * m)
            e3, q3, k3 = (a.reshape(n, 2 * m, DK) for a in (e, qf, kf))
            zero = jnp.zeros((n, m, DK), f32)
            qt = jnp.concatenate([zero, q3[:, m:, :] * e3[:, m:, :]], axis=1).reshape(C, DK)
            kt = jnp.concatenate([k3[:, :m, :] * e3[:, :m, :], zero], axis=1).reshape(C, DK)
        else:
            odd = ((row >> _log2(m)) & 1) == 1
            qt = jnp.where(odd, qf * e, 0.0)
            kt = jnp.where(odd, 0.0, kf * e)
        qt, kt = qt.astype(bf16), kt.astype(bf16)
        tiles = [jnp.where(lvl == _log2(m), _dot(blk_rows(qt, i), blk_rows(kt, i), _NT), tiles[i])
                 for i in range(nblk)]

    yield
    st = st_ref[...]
    qe = (qf * jnp.exp(b)).astype(bf16)
    outs = []
    for i in range(nblk):
        lhs = tiles[i] if i == 0 else jnp.concatenate([cross, tiles[i]], axis=1)
        outs.append(_dot(lhs.astype(bf16), vb[0:(i + 1) * GLA_BLK]))
    o = _dot(qe, st.astype(bf16), _NT) + jnp.concatenate(outs, axis=0)
    ms = jnp.mean(o * o, axis=-1, keepdims=True)
    on = o * lax.rsqrt(ms + RMS_EPS) * ng_ref[...]
    og_ref[:, vcols] = (on * g_ref[:, vcols].astype(f32)).astype(bf16)

    yield
    kd = (kf * jnp.exp(b_last - b)).astype(bf16)
    st_new = st * jnp.exp(b_last) + _dot(vb, kd, _TN)
    st_ref[...] = st_new


def _gla(proj, la2, ng, s_meta, s0_sample):
    assert CHUNK in (GLA_BLK, 2 * GLA_BLK) and N_HEADS % GLA_HPS == 0
    assert DEC_BATCH % GLA_STEPS == 0
    diff, level = _gla_tables(CHUNK)
    hps, sps = GLA_HPS, SAMPLE_PER_STEP
    nh = N_HEADS // hps
    rb = lambda b, s: b * N_CHUNKS + s
    const = lambda b, h, s: (0, 0)
    step = lambda b, h, s: (b * nh + h) * N_CHUNKS + s
    srow = NP // DEC_BATCH
    scol = lambda col, width: (lambda b, h, s: (srow, col // width))
    return pl.pallas_call(
        _gla_kernel,
        grid=(BATCH, nh, N_CHUNKS),
        in_specs=[
            pl.BlockSpec((CHUNK, hps * DK), lambda b, h, s: (rb(b, s), COL_Q // (hps * DK) + h)),
            pl.BlockSpec((CHUNK, hps * DK), lambda b, h, s: (rb(b, s), COL_K // (hps * DK) + h)),
            pl.BlockSpec((CHUNK, hps * DV), lambda b, h, s: (rb(b, s), COL_V // (hps * DV) + h)),
            pl.BlockSpec((CHUNK, hps * DV), lambda b, h, s: (rb(b, s), COL_G // (hps * DV) + h)),
            pl.BlockSpec((CHUNK, hps * LA_W), lambda b, h, s: (rb(b, s), h)),
            pl.BlockSpec((1, DV), const),
            pl.BlockSpec((hps, DV, DK), lambda b, h, s: (h, 0, 0)),
            pl.BlockSpec(diff.shape, const),
            pl.BlockSpec(level.shape, const),
            pl.BlockSpec((DEC_BATCH, D_GLA_K), scol(COL_Q, D_GLA_K)),
            pl.BlockSpec((DEC_BATCH, D_GLA_K), scol(COL_K, D_GLA_K)),
            pl.BlockSpec((DEC_BATCH, D_GLA_V), scol(COL_V, D_GLA_V)),
            pl.BlockSpec((DEC_BATCH, D_GLA_V), scol(COL_G, D_GLA_V)),
            pl.BlockSpec((DEC_BATCH, N_HEADS * LA_W), lambda b, h, s: (srow, 0)),
            pl.BlockSpec((sps, N_HEADS, DK, DV), lambda b, h, s: (step(b, h, s), 0, 0, 0)),
        ],
        out_specs=[
            pl.BlockSpec((CHUNK, hps * DV), lambda b, h, s: (rb(b, s), h)),
            pl.BlockSpec((1, hps, DK, DV), lambda b, h, s: (b, h, 0, 0)),
            pl.BlockSpec((DEC_BATCH, D_GLA_V), const),
            pl.BlockSpec((sps, N_HEADS, DK, DV), lambda b, h, s: (step(b, h, s), 0, 0, 0)),
        ],
        out_shape=[jax.ShapeDtypeStruct((R_ALL, D_GLA_V), bf16),
                   jax.ShapeDtypeStruct((BATCH, N_HEADS, DK, DV), f32),
                   jax.ShapeDtypeStruct((DEC_BATCH, D_GLA_V), bf16),
                   jax.ShapeDtypeStruct((DEC_BATCH, N_HEADS, DK, DV), f32)],
        scratch_shapes=([pltpu.VMEM((DEC_BATCH, D_GLA_V), f32)]
                        + [pltpu.VMEM((DV, DK), f32)] * hps
                        + [pltpu.VMEM((diff.shape[0], DK), f32)] * hps),
        compiler_params=_params(3),
        name="gla",
    )(proj, proj, proj, proj, la2, ng, s_meta, diff, level,
      proj, proj, proj, proj, la2, s0_sample)


def _gla_meta_kernel(k_ref, v_ref, la_ref, diff_ref, s_ref, x_ref):
    _decay_sums(la_ref[...], diff_ref, x_ref)
    kd = (k_ref[...].astype(f32) * jnp.exp(x_ref[N_META - 1:N_META, :] - x_ref[...])).astype(bf16)
    s_ref[0] = _dot(v_ref[...], kd, _TN)


def _gla_meta(proj, la2):
    diff = _gla_tables(N_META)[0][:N_META]
    mb = R_MAIN // N_META
    return pl.pallas_call(
        _gla_meta_kernel,
        grid=(N_HEADS,),
        in_specs=[
            pl.BlockSpec((N_META, DK), lambda h: (mb, COL_K // DK + h)),
            pl.BlockSpec((N_META, DV), lambda h: (mb, COL_V // DV + h)),
            pl.BlockSpec((N_META, LA_W), lambda h: (mb, h)),
            pl.BlockSpec(diff.shape, lambda h: (0, 0)),
        ],
        out_specs=pl.BlockSpec((1, DV, DK), lambda h: (h, 0, 0)),
        out_shape=jax.ShapeDtypeStruct((N_HEADS, DV, DK), f32),
        scratch_shapes=[pltpu.VMEM((N_META, DK), f32)],
        compiler_params=_params(1),
        name="gla_meta",
    )(proj, proj, la2, diff)


CONV_TR, CONV_TC = 1024, 1024
CONV_PAD = 8


def _conv_prompt_kernel(cb_ref, cc_ref, ch_ref, mc_ref, mh_ref, w_ref, y_ref, nb_ref, u_ref):
    t = pl.program_id(2)
    TR = CONV_TR

    @pl.when(t == 0)
    def _():
        mu = mc_ref[...].astype(f32) * mh_ref[...].astype(f32)
        u_ref[CONV_PAD - 2:CONV_PAD, :] = mu[N_META - 2:N_META, :]

    u = cc_ref[...].astype(f32) * ch_ref[...].astype(f32)
    u_ref[CONV_PAD:CONV_PAD + TR, :] = u
    w = w_ref[...]
    zc = (w[0:1, :] * u_ref[CONV_PAD - 2:CONV_PAD - 2 + TR, :]
          + w[1:2, :] * u_ref[CONV_PAD - 1:CONV_PAD - 1 + TR, :]
          + w[2:3, :] * u)
    y_ref[...] = (cb_ref[...].astype(f32) * zc).astype(bf16)
    last = u[TR - 2:TR, :]
    u_ref[CONV_PAD - 2:CONV_PAD, :] = last

    @pl.when(t == SEQ // TR - 1)
    def _():
        nb_ref[0] = last


def _conv_prompt(proj, conv_w):
    TR, TC = CONV_TR, CONV_TC
    nt = SEQ // TR
    mrow = R_MAIN // N_META
    return pl.pallas_call(
        _conv_prompt_kernel,
        grid=(BATCH, D_CONV // TC, nt),
        in_specs=[
            pl.BlockSpec((TR, TC), lambda b, j, t: (b * nt + t, COL_CB // TC + j)),
            pl.BlockSpec((TR, TC), lambda b, j, t: (b * nt + t, COL_CC // TC + j)),
            pl.BlockSpec((TR, TC), lambda b, j, t: (b * nt + t, COL_CH // TC + j)),
            pl.BlockSpec((N_META, TC), lambda b, j, t: (mrow, COL_CC // TC + j)),
            pl.BlockSpec((N_META, TC), lambda b, j, t: (mrow, COL_CH // TC + j)),
            pl.BlockSpec((CONV_WIDTH, TC), lambda b, j, t: (0, j)),
        ],
        out_specs=[
            pl.BlockSpec((TR, TC), lambda b, j, t: (b * nt + t, j)),
            pl.BlockSpec((1, CONV_WIDTH - 1, TC), lambda b, j, t: (b, 0, j)),
        ],
        out_shape=[jax.ShapeDtypeStruct((R_ALL, D_CONV), bf16),
                   jax.ShapeDtypeStruct((BATCH, CONV_WIDTH - 1, D_CONV), f32)],
        scratch_shapes=[pltpu.VMEM((CONV_PAD + TR, TC), f32)],
        compiler_params=_params(3),
        name="conv_prompt",
    )(proj, proj, proj, proj, proj, conv_w)


def _conv_sample_kernel(cb_ref, cc_ref, ch_ref, buf_ref, w_ref, y_in_ref, y_ref, nb_ref):
    del y_in_ref
    u = cc_ref[...].astype(f32) * ch_ref[...].astype(f32)
    w = w_ref[...]
    b0 = buf_ref[:, 0:D_CONV]
    b1 = buf_ref[:, D_CONV:2 * D_CONV]
    zc = w[0:1, :] * b0 + w[1:2, :] * b1 + w[2:3, :] * u
    y_ref[...] = (cb_ref[...].astype(f32) * zc).astype(bf16)
    nb_ref[:, 0:D_CONV] = b1
    nb_ref[:, D_CONV:2 * D_CONV] = u


def _conv_sample(proj, buf, conv_w, yc):
    rb = NP // DEC_BATCH
    return pl.pallas_call(
        _conv_sample_kernel,
        grid=(1,),
        in_specs=[
            pl.BlockSpec((DEC_BATCH, D_CONV), lambda i: (rb, COL_CB // D_CONV)),
            pl.BlockSpec((DEC_BATCH, D_CONV), lambda i: (rb, COL_CC // D_CONV)),
            pl.BlockSpec((DEC_BATCH, D_CONV), lambda i: (rb, COL_CH // D_CONV)),
            pl.BlockSpec((DEC_BATCH, 2 * D_CONV), lambda i: (0, 0)),
            pl.BlockSpec((CONV_WIDTH, D_CONV), lambda i: (0, 0)),
            pl.BlockSpec(memory_space=pl.ANY),
        ],
        out_specs=[
            pl.BlockSpec((DEC_BATCH, D_CONV), lambda i: (rb, 0)),
            pl.BlockSpec((DEC_BATCH, 2 * D_CONV), lambda i: (0, 0)),
        ],
        out_shape=[jax.ShapeDtypeStruct((R_ALL, D_CONV), bf16),
                   jax.ShapeDtypeStruct((DEC_BATCH, 2 * D_CONV), f32)],
        input_output_aliases={5: 0},
        compiler_params=_params(1),
        name="conv_sample",
    )(proj, proj, proj, buf, conv_w, yc)


OUT_FULL = TAIL_ROW0 // OUT_TM


def _outproj_kernel(o_ref, y_ref, wo_ref, wy_ref, x_ref, mu_ref, rs_ref, g_ref, b_ref, ht_ref,
                    s_ref):
    i = pl.program_id(0)

    def mix():
        return (_dot(o_ref[...], wo_ref[...].astype(bf16))
                + _dot(y_ref[...], wy_ref[...].astype(bf16)))

    @pl.when(i < OUT_FULL)
    def _():
        s_ref[...] = _ln_from_stats(x_ref[...], mu_ref, rs_ref, g_ref[...], b_ref[...]) + mix()

    @pl.when(i >= OUT_FULL)
    def _():
        s_ref[...] = ALPHA * ht_ref[...] + mix()


def _outproj(og, yc, w_out, xp, mu, rs, g, b, h_tail):
    full = lambda i: i < OUT_FULL
    return pl.pallas_call(
        _outproj_kernel,
        grid=(R_MAIN // OUT_TM, D_MODEL // OUT_TN),
        in_specs=[
            pl.BlockSpec((OUT_TM, D_GLA_V), lambda i, j: (i, 0)),
            pl.BlockSpec((OUT_TM, D_CONV), lambda i, j: (i, 0)),
            pl.BlockSpec((D_GLA_V, OUT_TN), lambda i, j: (0, j)),
            pl.BlockSpec((D_CONV, OUT_TN), lambda i, j: (1, j)),
            pl.BlockSpec((OUT_TM, OUT_TN),
                         lambda i, j: (jnp.minimum(i, OUT_FULL - 1), jnp.where(full(i), j, 0))),
            pl.BlockSpec((OUT_TM, LANES), lambda i, j: (i, 0)),
            pl.BlockSpec((OUT_TM, LANES), lambda i, j: (i, 0)),
            pl.BlockSpec((1, OUT_TN), lambda i, j: (0, j)),
            pl.BlockSpec((1, OUT_TN), lambda i, j: (0, j)),
            pl.BlockSpec((OUT_TM, OUT_TN), lambda i, j: (0, jnp.where(full(i), 0, j))),
        ],
        out_specs=pl.BlockSpec((OUT_TM, OUT_TN), lambda i, j: (i, j)),
        out_shape=jax.ShapeDtypeStruct((R_MAIN, D_MODEL), f32),
        compiler_params=_params(2),
        name="outproj",
    )(og, yc, w_out, w_out, xp, mu, rs, g, b, h_tail)


LN1_ROWS = 320


def _ln1_kernel(s_ref, g_ref, b_ref, hb_ref, mu_ref, rs_ref):
    x = s_ref[...]
    mu, rs = _ln_stats(x)
    hb_ref[...] = ((x - mu) * rs * g_ref[...] + b_ref[...]).astype(bf16)
    _store_stats(mu, rs, mu_ref, rs_ref)


def _ln1(s1, g, b):
    return pl.pallas_call(
        _ln1_kernel,
        grid=(R_MAIN // LN1_ROWS,),
        in_specs=[
            pl.BlockSpec((LN1_ROWS, D_MODEL), lambda i: (i, 0)),
            pl.BlockSpec((1, D_MODEL), lambda i: (0, 0)),
            pl.BlockSpec((1, D_MODEL), lambda i: (0, 0)),
        ],
        out_specs=[
            pl.BlockSpec((LN1_ROWS, D_MODEL), lambda i: (i, 0)),
            pl.BlockSpec((LN1_ROWS, LANES), lambda i: (i, 0)),
            pl.BlockSpec((LN1_ROWS, LANES), lambda i: (i, 0)),
        ],
        out_shape=[jax.ShapeDtypeStruct((R_MAIN, D_MODEL), bf16),
                   jax.ShapeDtypeStruct((R_MAIN, LANES), f32),
                   jax.ShapeDtypeStruct((R_MAIN, LANES), f32)],
        compiler_params=_params(1),
        name="ln1",
    )(s1, g, b)


LN2_ROWS = 512


def _ln2_kernel(sp_ref, ss_ref, g_ref, b_ref, yp_ref, ys_ref):
    yp_ref[...] = _layer_norm(sp_ref[...], g_ref[...], b_ref[...])

    @pl.when(pl.program_id(0) == 0)
    def _():
        ys_ref[...] = _layer_norm(ss_ref[...], g_ref[...], b_ref[...])


def _ln2(s2, g, b):
    return pl.pallas_call(
        _ln2_kernel,
        grid=(NP // LN2_ROWS,),
        in_specs=[
            pl.BlockSpec((LN2_ROWS, D_MODEL), lambda i: (i, 0)),
            pl.BlockSpec((DEC_BATCH, D_MODEL), lambda i: (NP // DEC_BATCH, 0)),
            pl.BlockSpec((1, D_MODEL), lambda i: (0, 0)),
            pl.BlockSpec((1, D_MODEL), lambda i: (0, 0)),
        ],
        out_specs=[
            pl.BlockSpec((LN2_ROWS, D_MODEL), lambda i: (i, 0)),
            pl.BlockSpec((DEC_BATCH, D_MODEL), lambda i: (0, 0)),
        ],
        out_shape=[jax.ShapeDtypeStruct((NP, D_MODEL), f32),
                   jax.ShapeDtypeStruct((DEC_BATCH, D_MODEL), f32)],
        compiler_params=_params(1),
        name="ln2",
    )(s2, s2, g, b)


FFN_TM, FFN_TN = 2080, 256


FFN_NT = D_FF // FFN_TN


def _ffn_up_kernel(h_ref, wg_ref, wu_ref, wd_ref, act_ref, wdb_ref):
    i = pl.program_id(0)
    j = pl.program_id(1)

    @pl.when(j < FFN_NT)
    def _():
        h = h_ref[...]
        a = _dot(h, wg_ref[...].astype(bf16))
        u = _dot(h, wu_ref[...].astype(bf16))
        act_ref[...] = (a * jax.nn.sigmoid(a) * u).astype(bf16)

    @pl.when(j >= FFN_NT)
    def _():
        act_ref[...] = jnp.zeros_like(act_ref)

    @pl.when(jnp.logical_and(i == 0, j < FFN_NT))
    def _():
        wdb_ref[...] = wd_ref[...].astype(bf16)

    @pl.when(jnp.logical_and(i == 0, j >= FFN_NT))
    def _():
        wdb_ref[...] = jnp.zeros_like(wdb_ref)


def _ffn_up(hb, wg, wu, wd):
    nt_pad = D_FF_PAD // FFN_TN
    jw = lambda j: jnp.minimum(j, FFN_NT - 1)
    return pl.pallas_call(
        _ffn_up_kernel,
        grid=(R_MAIN // FFN_TM, nt_pad),
        in_specs=[
            pl.BlockSpec((FFN_TM, D_MODEL), lambda i, j: (i, 0), pipeline_mode=pl.Buffered(1)),
            pl.BlockSpec((D_MODEL, FFN_TN), lambda i, j: (0, jw(j))),
            pl.BlockSpec((D_MODEL, FFN_TN), lambda i, j: (0, jw(j))),
            pl.BlockSpec((FFN_TN, D_MODEL), lambda i, j: (jnp.where(i == 0, jw(j), FFN_NT - 1), 0)),
        ],
        out_specs=[
            pl.BlockSpec((FFN_TM, FFN_TN), lambda i, j: (i, j)),
            pl.BlockSpec((FFN_TN, D_MODEL), lambda i, j: (jnp.where(i == 0, j, nt_pad - 1), 0)),
        ],
        out_shape=[jax.ShapeDtypeStruct((R_MAIN, D_FF_PAD), bf16),
                   jax.ShapeDtypeStruct((D_FF_PAD, D_MODEL), bf16)],
        compiler_params=_params(2),
        name="ffn_up",
    )(hb, wg, wu, wd)


DOWN_TM, DOWN_TN, DOWN_TK = 2080, 1024, 1024


def _ffn_down_kernel(a_ref, w_ref, s1_ref, mu_ref, rs_ref, g_ref, b_ref, s_ref):
    @pl.when(pl.program_id(2) == 0)
    def _():
        s_ref[...] = (_ln_from_stats(s1_ref[...], mu_ref, rs_ref, g_ref[...], b_ref[...])
                      + _dot(a_ref[...], w_ref[...]))

    @pl.when(pl.program_id(2) > 0)
    def _():
        s_ref[...] += _dot(a_ref[...], w_ref[...])


def _ffn_down(act, wd, s1, mu, rs, g, b):
    return pl.pallas_call(
        _ffn_down_kernel,
        grid=(R_MAIN // DOWN_TM, D_MODEL // DOWN_TN, D_FF_PAD // DOWN_TK),
        in_specs=[
            pl.BlockSpec((DOWN_TM, DOWN_TK), lambda i, j, k: (i, k)),
            pl.BlockSpec((DOWN_TK, DOWN_TN), lambda i, j, k: (k, j)),
            pl.BlockSpec((DOWN_TM, DOWN_TN), lambda i, j, k: (i, j)),
            pl.BlockSpec((DOWN_TM, LANES), lambda i, j, k: (i, 0)),
            pl.BlockSpec((DOWN_TM, LANES), lambda i, j, k: (i, 0)),
            pl.BlockSpec((1, DOWN_TN), lambda i, j, k: (0, j)),
            pl.BlockSpec((1, DOWN_TN), lambda i, j, k: (0, j)),
        ],
        out_specs=pl.BlockSpec((DOWN_TM, DOWN_TN), lambda i, j, k: (i, j)),
        out_shape=jax.ShapeDtypeStruct((R_MAIN, D_MODEL), f32),
        compiler_params=_params(3),
        name="ffn_down",
    )(act, wd, s1, mu, rs, g, b)


def kernel(x_prompt, x_sample, state_gla, state_conv, meta_tokens, emb_ln_g, emb_ln_b,
           w_in, w_a2, b_a, gla_norm_g, conv_w, w_out, ln1_g, ln1_b,
           w_ffn_gate, w_ffn_up, w_ffn_down, ln2_g, ln2_b):
    assert x_prompt.shape == (BATCH, SEQ, D_MODEL) and x_sample.shape == (DEC_BATCH, 1, D_MODEL)
    assert w_in.shape[0] == 1, "single layer"
    row = lambda v: v.reshape(1, -1)

    w_in_t = jnp.transpose(w_in[0])
    wa2p = jnp.pad(w_a2[0], ((0, LANES - GATE_RANK), (0, 0)))

    xp = x_prompt.reshape(NP, D_MODEL)
    xt = jnp.concatenate([x_sample.reshape(DEC_BATCH, D_MODEL), meta_tokens.astype(f32),
                          jnp.zeros((LN_ROWS - DEC_BATCH - N_META, D_MODEL), f32)], axis=0)

    g0, b0 = row(emb_ln_g), row(emb_ln_b)
    g1, b1 = row(ln1_g[0]), row(ln1_b[0])
    hb, alr, mu0, rs0, h_tail = _ln0(xp, xt, g0, b0, w_in_t)
    proj, la2 = _proj(hb, w_in_t, alr, wa2p, row(b_a[0]))

    ng = row(gla_norm_g[0])
    s_meta = _gla_meta(proj, la2)
    og, s_p, og_s, s_s = _gla(proj, la2, ng, s_meta, state_gla[0])
    og = lax.dynamic_update_slice(og, og_s, (NP, 0))
    yc, nb_p = _conv_prompt(proj, conv_w[0])
    yc, nb_s = _conv_sample(proj, state_conv[0].reshape(DEC_BATCH, 2 * D_CONV), conv_w[0], yc)

    s1 = _outproj(og, yc, w_out[0], xp, mu0, rs0, ALPHA * g0, ALPHA * b0, h_tail)
    h1b, mu1, rs1 = _ln1(s1, g1, b1)
    act, wd = _ffn_up(h1b, w_ffn_gate[0], w_ffn_up[0], w_ffn_down[0])
    s2 = _ffn_down(act, wd, s1, mu1, rs1, ALPHA * g1, ALPHA * b1)
    y_p, y_s = _ln2(s2, row(ln2_g[0]), row(ln2_b[0]))

    return (y_p.reshape(BATCH, SEQ, D_MODEL),
            y_s.reshape(DEC_BATCH, 1, D_MODEL),
            s_p[None],
            nb_p[None],
            s_s[None],
            nb_s.reshape(1, DEC_BATCH, CONV_WIDTH - 1, D_CONV))
```

```python
import jax
import jax.numpy as jnp
import numpy as np
from jax import lax
from jax.experimental import pallas as pl
from jax.experimental.pallas import tpu as pltpu

f32 = jnp.float32
bf16 = jnp.bfloat16

D_MODEL = 4096
BATCH = 4
SEQ = 2048
DEC_BATCH = 128
N_META = 16
D_GLA_V = D_MODEL // 2
D_CONV = D_MODEL - D_GLA_V
D_GLA_K = D_GLA_V // 2
N_HEADS = 4
DK = D_GLA_K // N_HEADS
DV = D_GLA_V // N_HEADS
GATE_RANK = 16
GATE_TAU = 16.0
CONV_WIDTH = 3
D_FF = 11008
ALPHA = 2.0 ** 0.25
LN_EPS = 1e-5
RMS_EPS = 1e-6

LANES = 128
SUBLANES = 8
NP = BATCH * SEQ
R_MAIN = NP + DEC_BATCH
R_ALL = R_MAIN + LANES
D_PROJ = 2 * D_GLA_K + 2 * D_GLA_V + 3 * D_CONV
D_FF_PAD = 11264
CHUNK = 128
N_CHUNKS = SEQ // CHUNK
VMEM_LIMIT = 58 * 1024 * 1024

COL_Q, COL_K, COL_V, COL_G = 0, D_GLA_K, 2 * D_GLA_K, 2 * D_GLA_K + D_GLA_V
COL_CB = COL_G + D_GLA_V
COL_CC = COL_CB + D_CONV
COL_CH = COL_CC + D_CONV


def _params(n_axes):
    return pltpu.CompilerParams(
        dimension_semantics=("arbitrary",) * n_axes, vmem_limit_bytes=VMEM_LIMIT)


def _ln_stats(x):
    mu = jnp.mean(x, axis=-1, keepdims=True)
    xc = x - mu
    var = jnp.mean(xc * xc, axis=-1, keepdims=True)
    return mu, lax.rsqrt(var + LN_EPS)


def _layer_norm(x, g, b):
    mu, rs = _ln_stats(x)
    return (x - mu) * rs * g + b


def _ln_from_stats(x, mu_ref, rs_ref, g, b):
    reps = x.shape[-1] // LANES
    mu = jnp.concatenate([mu_ref[...]] * reps, axis=1)
    rs = jnp.concatenate([rs_ref[...]] * reps, axis=1)
    return (x - mu) * rs * g + b


def _store_stats(mu, rs, mu_ref, rs_ref):
    mu_ref[...] = jnp.broadcast_to(mu, mu_ref.shape)
    rs_ref[...] = jnp.broadcast_to(rs, rs_ref.shape)


def _split2(x):
    hi = x.astype(bf16)
    return hi, (x - hi.astype(f32)).astype(bf16)


def _dot(a, b, dims=(((1,), (0,)), ((), ()))):
    return lax.dot_general(a, b, dims, preferred_element_type=f32)


_NT = (((1,), (1,)), ((), ()))
_TN = (((0,), (0,)), ((), ()))


def _dot_f32(a, b):
    ah, am = _split2(a)
    bh, bm = _split2(b)
    return _dot(ah, bh) + _dot(ah, bm) + _dot(am, bh)


def _log2(n):
    assert n & (n - 1) == 0
    return n.bit_length() - 1


def _log_sigmoid(z):
    return jnp.minimum(z, 0.0) - jnp.log(1.0 + jnp.exp(-jnp.abs(z)))


LN_ROWS = 256


OUT_TM, OUT_TN = 1664, 256
TAIL_ROW0 = (R_MAIN // OUT_TM - 1) * OUT_TM
assert TAIL_ROW0 <= NP and TAIL_ROW0 % LN_ROWS == 0


def _ln0_kernel(xp_ref, xt_ref, g_ref, b_ref, wa_ref,
                hb_ref, alr_ref, mu_ref, rs_ref, ht_ref, wab_ref):
    i = pl.program_id(0)

    @pl.when(i == 0)
    def _():
        wab_ref[...] = wa_ref[...].astype(bf16)

    def emit(x):
        mu, rs = _ln_stats(x)
        h = (x - mu) * rs * g_ref[...] + b_ref[...]
        hb = h.astype(bf16)
        hb_ref[...] = hb
        alr_ref[...] = _dot(hb, wab_ref[...], _NT)
        _store_stats(mu, rs, mu_ref, rs_ref)
        ht_ref[...] = h

    @pl.when(i < NP // LN_ROWS)
    def _():
        emit(xp_ref[...])

    @pl.when(i >= NP // LN_ROWS)
    def _():
        emit(xt_ref[...])


def _ln0(xp, xt, g, b, w_in_t):
    n_p = NP // LN_ROWS
    t0 = TAIL_ROW0 // LN_ROWS
    return pl.pallas_call(
        _ln0_kernel,
        grid=(R_ALL // LN_ROWS,),
        in_specs=[
            pl.BlockSpec((LN_ROWS, D_MODEL), lambda i: (jnp.minimum(i, n_p - 1), 0)),
            pl.BlockSpec((LN_ROWS, D_MODEL), lambda i: (0, 0)),
            pl.BlockSpec((1, D_MODEL), lambda i: (0, 0)),
            pl.BlockSpec((1, D_MODEL), lambda i: (0, 0)),
            pl.BlockSpec((LANES, D_MODEL), lambda i: (COL_CB // LANES, 0)),
        ],
        out_specs=[
            pl.BlockSpec((LN_ROWS, D_MODEL), lambda i: (i, 0)),
            pl.BlockSpec((LN_ROWS, LANES), lambda i: (i, 0)),
            pl.BlockSpec((LN_ROWS, LANES), lambda i: (i, 0)),
            pl.BlockSpec((LN_ROWS, LANES), lambda i: (i, 0)),
            pl.BlockSpec((LN_ROWS, D_MODEL), lambda i: (jnp.maximum(i - t0, 0), 0)),
        ],
        out_shape=[jax.ShapeDtypeStruct((R_ALL, D_MODEL), bf16),
                   jax.ShapeDtypeStruct((R_ALL, LANES), f32),
                   jax.ShapeDtypeStruct((R_ALL, LANES), f32),
                   jax.ShapeDtypeStruct((R_ALL, LANES), f32),
                   jax.ShapeDtypeStruct((R_ALL - TAIL_ROW0, D_MODEL), f32)],
        scratch_shapes=[pltpu.VMEM((LANES, D_MODEL), bf16)],
        compiler_params=_params(1),
        name="ln0",
    )(xp, xt, g, b, w_in_t)


PROJ_TM, PROJ_TN = 1408, 512


LA_W = 2 * DK


def _proj_kernel(h_ref, w_ref, alr_ref, wa2_ref, ba_ref, p_ref, la_ref):
    j = pl.program_id(1)

    @pl.when(j < N_HEADS)
    def _():
        p_ref[...] = _dot(h_ref[...], w_ref[...].astype(bf16), _NT).astype(bf16)
        la_ref[:, 0:DK], la_ref[:, DK:LA_W] = _split2(
            _gate_log_decay(alr_ref[...], wa2_ref[...], ba_ref[...]))

    is_g = jnp.logical_and(j * PROJ_TN >= COL_G, j * PROJ_TN < COL_CB)

    @pl.when(is_g)
    def _():
        p = _dot(h_ref[...], w_ref[...].astype(bf16), _NT)
        p_ref[...] = (p * jax.nn.sigmoid(p)).astype(bf16)

    @pl.when(jnp.logical_and(j >= N_HEADS, jnp.logical_not(is_g)))
    def _():
        p_ref[...] = _dot(h_ref[...], w_ref[...].astype(bf16), _NT).astype(bf16)


def _proj(hb, w_in_t, alr, wa2p, ba):
    assert COL_G >= N_HEADS * PROJ_TN

    def w_row(j):
        return pl.multiple_of(j * PROJ_TN + jnp.where(j * PROJ_TN >= COL_CB, GATE_RANK, 0), GATE_RANK)

    jh = lambda j: jnp.minimum(j, N_HEADS - 1)
    return pl.pallas_call(
        _proj_kernel,
        grid=(R_ALL // PROJ_TM, D_PROJ // PROJ_TN),
        in_specs=[
            pl.BlockSpec((PROJ_TM, D_MODEL), lambda i, j: (i, 0)),
            pl.BlockSpec((pl.Element(PROJ_TN), pl.Element(D_MODEL)), lambda i, j: (w_row(j), 0)),
            pl.BlockSpec((PROJ_TM, LANES), lambda i, j: (i, 0)),
            pl.BlockSpec((LANES, DK), lambda i, j: (0, jh(j))),
            pl.BlockSpec((1, DK), lambda i, j: (0, jh(j))),
        ],
        out_specs=[
            pl.BlockSpec((PROJ_TM, PROJ_TN), lambda i, j: (i, j)),
            pl.BlockSpec((PROJ_TM, LA_W), lambda i, j: (i, jh(j))),
        ],
        out_shape=[jax.ShapeDtypeStruct((R_ALL, D_PROJ), bf16),
                   jax.ShapeDtypeStruct((R_ALL, N_HEADS * LA_W), bf16)],
        compiler_params=_params(2),
        name="proj",
    )(hb, w_in_t, alr, wa2p, ba)


GLA_BLK = 128
GLA_HPS = 4
GLA_ILV = 2
LVL_DIAG = _log2(GLA_BLK)


def _gla_levels(c):
    return [c >> (t + 1) for t in range(_log2(c))]


def _gla_tables(c):
    idx = np.arange(c)
    tri = (idx[None, :] <= idx[:, None]).astype(np.float32)
    mats = [tri]
    for m in _gla_levels(c):
        ref = ((idx // m) | 1) * m - 1
        mats.append(tri - tri[ref])
    diff = np.concatenate(mats, axis=0)
    diff = np.concatenate([diff, diff], axis=1)
    t = np.arange(min(c, GLA_BLK))
    level = np.full((t.size, t.size), -1, np.int32)
    for m in _gla_levels(t.size):
        same_pair = (t[:, None] // (2 * m)) == (t[None, :] // (2 * m))
        odd_even = (((t[:, None] // m) & 1) == 1) & (((t[None, :] // m) & 1) == 0)
        level[same_pair & odd_even] = _log2(m)
    level[t[:, None] == t[None, :]] = LVL_DIAG
    return jnp.asarray(diff, bf16), jnp.asarray(level)


def _gate_log_decay(alr, wa2, ba):
    return _log_sigmoid(_dot_f32(alr, wa2) + ba) * (1.0 / GATE_TAU)


def _decay_sums(la2, diff_ref, x_ref):
    x_ref[...] = _dot(diff_ref[...], jnp.concatenate([la2[:, 0:DK], la2[:, DK:LA_W]], axis=0))


GLA_STEPS = BATCH * (N_HEADS // GLA_HPS) * N_CHUNKS
SAMPLE_PER_STEP = DEC_BATCH // GLA_STEPS


def _gla_kernel(q_ref, k_ref, v_ref, g_ref, la_ref, ng_ref, s0_ref, diff_ref, lvl_ref,
                qs_ref, ks_ref, vs_ref, gs_ref, las_ref, s0s_ref,
                og_ref, sfin_ref, ogs_ref, ss_ref, ogacc_ref, *head_scratch):
    s = pl.program_id(2)
    step = (pl.program_id(0) * pl.num_programs(1) + pl.program_id(1)) * N_CHUNKS + s
    st_refs, x_refs = head_scratch[:GLA_HPS], head_scratch[GLA_HPS:]

    @pl.when(s == 0)
    def _():
        for hh in range(GLA_HPS):
            st_refs[hh][...] = s0_ref[hh]

    @pl.when(step == 0)
    def _():
        ogacc_ref[...] = jnp.zeros_like(ogacc_ref)

    for h0 in range(0, GLA_HPS, GLA_ILV):
        heads = [_gla_prompt_head(hh, q_ref, k_ref, v_ref, g_ref, la_ref, ng_ref,
                                  diff_ref, lvl_ref, og_ref, st_refs[hh], x_refs[hh])
                 for hh in range(h0, h0 + GLA_ILV)]
        while heads:
            heads = [h for h in heads if next(h, _DONE) is not _DONE]

    rows = lax.broadcasted_iota(jnp.int32, (DEC_BATCH, 1), 0)
    for hh in range(N_HEADS):
        vcols = slice(hh * DV, (hh + 1) * DV)
        acc = ogacc_ref[:, vcols]
        for t in range(SAMPLE_PER_STEP):
            r = step * SAMPLE_PER_STEP + t
            og_row = _gla_sample_token(r, t, hh, qs_ref, ks_ref, vs_ref, gs_ref, las_ref, ng_ref,
                                       s0s_ref, ss_ref)
            acc = jnp.where(rows == r, og_row, acc)
        ogacc_ref[:, vcols] = acc

    @pl.when(s == N_CHUNKS - 1)
    def _():
        for hh in range(GLA_HPS):
            sfin_ref[0, hh] = st_refs[hh][...].T

    @pl.when(step == GLA_STEPS - 1)
    def _():
        ogs_ref[...] = ogacc_ref[...].astype(bf16)


def _gla_sample_token(r, t, hh, qs_ref, ks_ref, vs_ref, gs_ref, las_ref, ng_ref, s0s_ref, ss_ref):
    kcols = slice(hh * DK, (hh + 1) * DK)
    vcols = slice(hh * DV, (hh + 1) * DV)
    as_col = (lax.broadcasted_iota(jnp.int32, (DEC_BATCH, LANES), 0) == r).astype(bf16)
    as_row = (lax.broadcasted_iota(jnp.int32, (SUBLANES, DEC_BATCH), 1) == r).astype(bf16)
    la_col = (_dot(las_ref[:, hh * LA_W:hh * LA_W + DK], as_col, _TN)
              + _dot(las_ref[:, hh * LA_W + DK:(hh + 1) * LA_W], as_col, _TN))
    a_col = jnp.exp(la_col)
    k_col = _dot(ks_ref[:, kcols], as_col, _TN)
    q_col = _dot(qs_ref[:, kcols], as_col, _TN) * (DK ** -0.5)
    v_row = _dot(as_row, vs_ref[:, vcols])[0:1]
    g_row = _dot(as_row, gs_ref[:, vcols])[0:1]
    wide = lambda c: jnp.concatenate([c] * (DV // LANES), axis=1)
    s_new = wide(a_col) * s0s_ref[t, hh] + wide(k_col) * v_row
    ss_ref[t, hh] = s_new
    o = jnp.sum(wide(q_col) * s_new, axis=0, keepdims=True)
    ms = jnp.mean(o * o, axis=-1, keepdims=True)
    return o * lax.rsqrt(ms + RMS_EPS) * ng_ref[...] * g_row


_DONE = object()


def _gla_prompt_head(hh, q_ref, k_ref, v_ref, g_ref, la_ref, ng_ref,
                     diff_ref, lvl_ref, og_ref, st_ref, x_ref):
    C = CHUNK
    nblk = C // GLA_BLK
    kcols = slice(hh * DK, (hh + 1) * DK)
    vcols = slice(hh * DV, (hh + 1) * DV)

    _decay_sums(la_ref[:, hh * LA_W:(hh + 1) * LA_W], diff_ref, x_ref)
    yield
    kb = k_ref[:, kcols]
    kf = kb.astype(f32)
    vb = v_ref[:, vcols]
    qf = q_ref[:, kcols].astype(f32) * (DK ** -0.5)
    qb = qf.astype(bf16)
    b = x_ref[0:C, :]
    b_last = x_ref[C - 1:C, :]

    lvl = lvl_ref[...]
    blk_rows = lambda a, i: a[i * GLA_BLK:(i + 1) * GLA_BLK]
    tiles = [jnp.where(lvl == LVL_DIAG, _dot(blk_rows(qb, i), blk_rows(kb, i), _NT), 0.0)
             for i in range(nblk)]
    cross = None
    row = lax.broadcasted_iota(jnp.int32, (C, 1), 0)
    for t, m in enumerate(_gla_levels(C)):
        yield
        e = jnp.exp(-jnp.abs(x_ref[(t + 1) * C:(t + 2) * C, :]))
        if m == GLA_BLK:
            qt = (blk_rows(qf, 1) * blk_rows(e, 1)).astype(bf16)
            kt = (blk_rows(kf, 0) * blk_rows(e, 0)).astype(bf16)
            cross = _dot(qt, kt, _NT)
            continue
        if m >= 8:
            n = C // (2 * m)
            e3, q3, k3 = (a.reshape(n, 2 * m, DK) for a in (e, qf, kf))
            zero = jnp.zeros((n, m, DK), f32)
            qt = jnp.concatenate([zero, q3[:, m:, :] * e3[:, m:, :]], axis=1).reshape(C, DK)
            kt = jnp.concatenate([k3[:, :m, :] * e3[:, :m, :], zero], axis=1).reshape(C, DK)
        else:
            odd = ((row >> _log2(m)) & 1) == 1
            qt = jnp.where(odd, qf * e, 0.0)
            kt = jnp.where(odd, 0.0, kf * e)
        qt, kt = qt.astype(bf16), kt.astype(bf16)
        tiles = [jnp.where(lvl == _log2(m), _dot(blk_rows(qt, i), blk_rows(kt, i), _NT), tiles[i])
                 for i in range(nblk)]

    yield
    st = st_ref[...]
    qe = (qf * jnp.exp(b)).astype(bf16)
    outs = []
    for i in range(nblk):
        lhs = tiles[i] if i == 0 else jnp.concatenate([cross, tiles[i]], axis=1)
        outs.append(_dot(lhs.astype(bf16), vb[0:(i + 1) * GLA_BLK]))
    o = _dot(qe, st.astype(bf16), _NT) + jnp.concatenate(outs, axis=0)
    ms = jnp.mean(o * o, axis=-1, keepdims=True)
    on = o * lax.rsqrt(ms + RMS_EPS) * ng_ref[...]
    og_ref[:, vcols] = (on * g_ref[:, vcols].astype(f32)).astype(bf16)

    yield
    kd = (kf * jnp.exp(b_last - b)).astype(bf16)
    st_new = st * jnp.exp(b_last) + _dot(vb, kd, _TN)
    st_ref[...] = st_new


def _gla(proj, la2, ng, s_meta, s0_sample):
    assert CHUNK in (GLA_BLK, 2 * GLA_BLK) and N_HEADS % GLA_HPS == 0
    assert DEC_BATCH % GLA_STEPS == 0
    diff, level = _gla_tables(CHUNK)
    hps, sps = GLA_HPS, SAMPLE_PER_STEP
    nh = N_HEADS // hps
    rb = lambda b, s: b * N_CHUNKS + s
    const = lambda b, h, s: (0, 0)
    step = lambda b, h, s: (b * nh + h) * N_CHUNKS + s
    srow = NP // DEC_BATCH
    scol = lambda col, width: (lambda b, h, s: (srow, col // width))
    return pl.pallas_call(
        _gla_kernel,
        grid=(BATCH, nh, N_CHUNKS),
        in_specs=[
            pl.BlockSpec((CHUNK, hps * DK), lambda b, h, s: (rb(b, s), COL_Q // (hps * DK) + h)),
            pl.BlockSpec((CHUNK, hps * DK), lambda b, h, s: (rb(b, s), COL_K // (hps * DK) + h)),
            pl.BlockSpec((CHUNK, hps * DV), lambda b, h, s: (rb(b, s), COL_V // (hps * DV) + h)),
            pl.BlockSpec((CHUNK, hps * DV), lambda b, h, s: (rb(b, s), COL_G // (hps * DV) + h)),
            pl.BlockSpec((CHUNK, hps * LA_W), lambda b, h, s: (rb(b, s), h)),
            pl.BlockSpec((1, DV), const),
            pl.BlockSpec((hps, DV, DK), lambda b, h, s: (h, 0, 0)),
            pl.BlockSpec(diff.shape, const),
            pl.BlockSpec(level.shape, const),
            pl.BlockSpec((DEC_BATCH, D_GLA_K), scol(COL_Q, D_GLA_K)),
            pl.BlockSpec((DEC_BATCH, D_GLA_K), scol(COL_K, D_GLA_K)),
            pl.BlockSpec((DEC_BATCH, D_GLA_V), scol(COL_V, D_GLA_V)),
            pl.BlockSpec((DEC_BATCH, D_GLA_V), scol(COL_G, D_GLA_V)),
            pl.BlockSpec((DEC_BATCH, N_HEADS * LA_W), lambda b, h, s: (srow, 0)),
            pl.BlockSpec((sps, N_HEADS, DK, DV), lambda b, h, s: (step(b, h, s), 0, 0, 0)),
        ],
        out_specs=[
            pl.BlockSpec((CHUNK, hps * DV), lambda b, h, s: (rb(b, s), h)),
            pl.BlockSpec((1, hps, DK, DV), lambda b, h, s: (b, h, 0, 0)),
            pl.BlockSpec((DEC_BATCH, D_GLA_V), const),
            pl.BlockSpec((sps, N_HEADS, DK, DV), lambda b, h, s: (step(b, h, s), 0, 0, 0)),
        ],
        out_shape=[jax.ShapeDtypeStruct((R_ALL, D_GLA_V), bf16),
                   jax.ShapeDtypeStruct((BATCH, N_HEADS, DK, DV), f32),
                   jax.ShapeDtypeStruct((DEC_BATCH, D_GLA_V), bf16),
                   jax.ShapeDtypeStruct((DEC_BATCH, N_HEADS, DK, DV), f32)],
        scratch_shapes=([pltpu.VMEM((DEC_BATCH, D_GLA_V), f32)]
                        + [pltpu.VMEM((DV, DK), f32)] * hps
                        + [pltpu.VMEM((diff.shape[0], DK), f32)] * hps),
        compiler_params=_params(3),
        name="gla",
    )(proj, proj, proj, proj, la2, ng, s_meta, diff, level,
      proj, proj, proj, proj, la2, s0_sample)


def _gla_meta_kernel(k_ref, v_ref, la_ref, diff_ref, s_ref, x_ref):
    _decay_sums(la_ref[...], diff_ref, x_ref)
    kd = (k_ref[...].astype(f32) * jnp.exp(x_ref[N_META - 1:N_META, :] - x_ref[...])).astype(bf16)
    s_ref[0] = _dot(v_ref[...], kd, _TN)


def _gla_meta(proj, la2):
    diff = _gla_tables(N_META)[0][:N_META]
    mb = R_MAIN // N_META
    return pl.pallas_call(
        _gla_meta_kernel,
        grid=(N_HEADS,),
        in_specs=[
            pl.BlockSpec((N_META, DK), lambda h: (mb, COL_K // DK + h)),
            pl.BlockSpec((N_META, DV), lambda h: (mb, COL_V // DV + h)),
            pl.BlockSpec((N_META, LA_W), lambda h: (mb, h)),
            pl.BlockSpec(diff.shape, lambda h: (0, 0)),
        ],
        out_specs=pl.BlockSpec((1, DV, DK), lambda h: (h, 0, 0)),
        out_shape=jax.ShapeDtypeStruct((N_HEADS, DV, DK), f32),
        scratch_shapes=[pltpu.VMEM((N_META, DK), f32)],
        compiler_params=_params(1),
        name="gla_meta",
    )(proj, proj, la2, diff)


CONV_TR, CONV_TC = 1024, 1024
CONV_PAD = SUBLANES


def _conv_prompt_kernel(cb_ref, cc_ref, ch_ref, mc_ref, mh_ref, w_ref, y_ref, nb_ref, u_ref):
    t = pl.program_id(2)
    TR = CONV_TR

    @pl.when(t == 0)
    def _():
        mu = mc_ref[...].astype(f32) * mh_ref[...].astype(f32)
        u_ref[CONV_PAD - 2:CONV_PAD, :] = mu[N_META - 2:N_META, :]

    u = cc_ref[...].astype(f32) * ch_ref[...].astype(f32)
    u_ref[CONV_PAD:CONV_PAD + TR, :] = u
    w = w_ref[...]
    zc = (w[0:1, :] * u_ref[CONV_PAD - 2:CONV_PAD - 2 + TR, :]
          + w[1:2, :] * u_ref[CONV_PAD - 1:CONV_PAD - 1 + TR, :]
          + w[2:3, :] * u)
    y_ref[...] = (cb_ref[...].astype(f32) * zc).astype(bf16)
    last = u[TR - 2:TR, :]
    u_ref[CONV_PAD - 2:CONV_PAD, :] = last

    @pl.when(t == SEQ // TR - 1)
    def _():
        nb_ref[0] = last


def _conv_prompt(proj, conv_w):
    TR, TC = CONV_TR, CONV_TC
    nt = SEQ // TR
    mrow = R_MAIN // N_META
    return pl.pallas_call(
        _conv_prompt_kernel,
        grid=(BATCH, D_CONV // TC, nt),
        in_specs=[
            pl.BlockSpec((TR, TC), lambda b, j, t: (b * nt + t, COL_CB // TC + j)),
            pl.BlockSpec((TR, TC), lambda b, j, t: (b * nt + t, COL_CC // TC + j)),
            pl.BlockSpec((TR, TC), lambda b, j, t: (b * nt + t, COL_CH // TC + j)),
            pl.BlockSpec((N_META, TC), lambda b, j, t: (mrow, COL_CC // TC + j)),
            pl.BlockSpec((N_META, TC), lambda b, j, t: (mrow, COL_CH // TC + j)),
            pl.BlockSpec((CONV_WIDTH, TC), lambda b, j, t: (0, j)),
        ],
        out_specs=[
            pl.BlockSpec((TR, TC), lambda b, j, t: (b * nt + t, j)),
            pl.BlockSpec((1, CONV_WIDTH - 1, TC), lambda b, j, t: (b, 0, j)),
        ],
        out_shape=[jax.ShapeDtypeStruct((R_ALL, D_CONV), bf16),
                   jax.ShapeDtypeStruct((BATCH, CONV_WIDTH - 1, D_CONV), f32)],
        scratch_shapes=[pltpu.VMEM((CONV_PAD + TR, TC), f32)],
        compiler_params=_params(3),
        name="conv_prompt",
    )(proj, proj, proj, proj, proj, conv_w)


def _conv_sample_kernel(cb_ref, cc_ref, ch_ref, buf_ref, w_ref, y_in_ref, y_ref, nb_ref):
    del y_in_ref
    u = cc_ref[...].astype(f32) * ch_ref[...].astype(f32)
    w = w_ref[...]
    b0 = buf_ref[:, 0:D_CONV]
    b1 = buf_ref[:, D_CONV:2 * D_CONV]
    zc = w[0:1, :] * b0 + w[1:2, :] * b1 + w[2:3, :] * u
    y_ref[...] = (cb_ref[...].astype(f32) * zc).astype(bf16)
    nb_ref[:, 0:D_CONV] = b1
    nb_ref[:, D_CONV:2 * D_CONV] = u


def _conv_sample(proj, buf, conv_w, yc):
    rb = NP // DEC_BATCH
    return pl.pallas_call(
        _conv_sample_kernel,
        grid=(1,),
        in_specs=[
            pl.BlockSpec((DEC_BATCH, D_CONV), lambda i: (rb, COL_CB // D_CONV)),
            pl.BlockSpec((DEC_BATCH, D_CONV), lambda i: (rb, COL_CC // D_CONV)),
            pl.BlockSpec((DEC_BATCH, D_CONV), lambda i: (rb, COL_CH // D_CONV)),
            pl.BlockSpec((DEC_BATCH, 2 * D_CONV), lambda i: (0, 0)),
            pl.BlockSpec((CONV_WIDTH, D_CONV), lambda i: (0, 0)),
            pl.BlockSpec(memory_space=pl.ANY),
        ],
        out_specs=[
            pl.BlockSpec((DEC_BATCH, D_CONV), lambda i: (rb, 0)),
            pl.BlockSpec((DEC_BATCH, 2 * D_CONV), lambda i: (0, 0)),
        ],
        out_shape=[jax.ShapeDtypeStruct((R_ALL, D_CONV), bf16),
                   jax.ShapeDtypeStruct((DEC_BATCH, 2 * D_CONV), f32)],
        input_output_aliases={5: 0},
        compiler_params=_params(1),
        name="conv_sample",
    )(proj, proj, proj, buf, conv_w, yc)


OUT_FULL = TAIL_ROW0 // OUT_TM


def _outproj_kernel(o_ref, y_ref, wo_ref, wy_ref, x_ref, mu_ref, rs_ref, g_ref, b_ref, ht_ref,
                    s_ref):
    i = pl.program_id(0)

    def mix():
        return (_dot(o_ref[...], wo_ref[...].astype(bf16))
                + _dot(y_ref[...], wy_ref[...].astype(bf16)))

    @pl.when(i < OUT_FULL)
    def _():
        s_ref[...] = _ln_from_stats(x_ref[...], mu_ref, rs_ref, g_ref[...], b_ref[...]) + mix()

    @pl.when(i >= OUT_FULL)
    def _():
        s_ref[...] = ALPHA * ht_ref[...] + mix()


def _outproj(og, yc, w_out, xp, mu, rs, g, b, h_tail):
    full = lambda i: i < OUT_FULL
    return pl.pallas_call(
        _outproj_kernel,
        grid=(R_MAIN // OUT_TM, D_MODEL // OUT_TN),
        in_specs=[
            pl.BlockSpec((OUT_TM, D_GLA_V), lambda i, j: (i, 0)),
            pl.BlockSpec((OUT_TM, D_CONV), lambda i, j: (i, 0)),
            pl.BlockSpec((D_GLA_V, OUT_TN), lambda i, j: (0, j)),
            pl.BlockSpec((D_CONV, OUT_TN), lambda i, j: (1, j)),
            pl.BlockSpec((OUT_TM, OUT_TN),
                         lambda i, j: (jnp.minimum(i, OUT_FULL - 1), jnp.where(full(i), j, 0))),
            pl.BlockSpec((OUT_TM, LANES), lambda i, j: (i, 0)),
            pl.BlockSpec((OUT_TM, LANES), lambda i, j: (i, 0)),
            pl.BlockSpec((1, OUT_TN), lambda i, j: (0, j)),
            pl.BlockSpec((1, OUT_TN), lambda i, j: (0, j)),
            pl.BlockSpec((OUT_TM, OUT_TN), lambda i, j: (0, jnp.where(full(i), 0, j))),
        ],
        out_specs=pl.BlockSpec((OUT_TM, OUT_TN), lambda i, j: (i, j)),
        out_shape=jax.ShapeDtypeStruct((R_MAIN, D_MODEL), f32),
        compiler_params=_params(2),
        name="outproj",
    )(og, yc, w_out, w_out, xp, mu, rs, g, b, h_tail)


LN1_ROWS = 320


def _ln1_kernel(s_ref, g_ref, b_ref, hb_ref, mu_ref, rs_ref):
    x = s_ref[...]
    mu, rs = _ln_stats(x)
    hb_ref[...] = ((x - mu) * rs * g_ref[...] + b_ref[...]).astype(bf16)
    _store_stats(mu, rs, mu_ref, rs_ref)


def _ln1(s1, g, b):
    return pl.pallas_call(
        _ln1_kernel,
        grid=(R_MAIN // LN1_ROWS,),
        in_specs=[
            pl.BlockSpec((LN1_ROWS, D_MODEL), lambda i: (i, 0)),
            pl.BlockSpec((1, D_MODEL), lambda i: (0, 0)),
            pl.BlockSpec((1, D_MODEL), lambda i: (0, 0)),
        ],
        out_specs=[
            pl.BlockSpec((LN1_ROWS, D_MODEL), lambda i: (i, 0)),
            pl.BlockSpec((LN1_ROWS, LANES), lambda i: (i, 0)),
            pl.BlockSpec((LN1_ROWS, LANES), lambda i: (i, 0)),
        ],
        out_shape=[jax.ShapeDtypeStruct((R_MAIN, D_MODEL), bf16),
                   jax.ShapeDtypeStruct((R_MAIN, LANES), f32),
                   jax.ShapeDtypeStruct((R_MAIN, LANES), f32)],
        compiler_params=_params(1),
        name="ln1",
    )(s1, g, b)


LN2_ROWS = 512


def _ln2_kernel(sp_ref, ss_ref, g_ref, b_ref, yp_ref, ys_ref):
    yp_ref[...] = _layer_norm(sp_ref[...], g_ref[...], b_ref[...])

    @pl.when(pl.program_id(0) == 0)
    def _():
        ys_ref[...] = _layer_norm(ss_ref[...], g_ref[...], b_ref[...])


def _ln2(s2, g, b):
    return pl.pallas_call(
        _ln2_kernel,
        grid=(NP // LN2_ROWS,),
        in_specs=[
            pl.BlockSpec((LN2_ROWS, D_MODEL), lambda i: (i, 0)),
            pl.BlockSpec((DEC_BATCH, D_MODEL), lambda i: (NP // DEC_BATCH, 0)),
            pl.BlockSpec((1, D_MODEL), lambda i: (0, 0)),
            pl.BlockSpec((1, D_MODEL), lambda i: (0, 0)),
        ],
        out_specs=[
            pl.BlockSpec((LN2_ROWS, D_MODEL), lambda i: (i, 0)),
            pl.BlockSpec((DEC_BATCH, D_MODEL), lambda i: (0, 0)),
        ],
        out_shape=[jax.ShapeDtypeStruct((NP, D_MODEL), f32),
                   jax.ShapeDtypeStruct((DEC_BATCH, D_MODEL), f32)],
        compiler_params=_params(1),
        name="ln2",
    )(s2, s2, g, b)


FFN_TM, FFN_TN = 2080, 256


FFN_NT = D_FF // FFN_TN


def _ffn_up_kernel(h_ref, wg_ref, wu_ref, wd_ref, act_ref, wdb_ref):
    i = pl.program_id(0)
    j = pl.program_id(1)

    @pl.when(j < FFN_NT)
    def _():
        h = h_ref[...]
        a = _dot(h, wg_ref[...].astype(bf16))
        u = _dot(h, wu_ref[...].astype(bf16))
        act_ref[...] = (a * jax.nn.sigmoid(a) * u).astype(bf16)

    @pl.when(j >= FFN_NT)
    def _():
        act_ref[...] = jnp.zeros_like(act_ref)

    @pl.when(jnp.logical_and(i == 0, j < FFN_NT))
    def _():
        wdb_ref[...] = wd_ref[...].astype(bf16)

    @pl.when(jnp.logical_and(i == 0, j >= FFN_NT))
    def _():
        wdb_ref[...] = jnp.zeros_like(wdb_ref)


def _ffn_up(hb, wg, wu, wd):
    nt_pad = D_FF_PAD // FFN_TN
    jw = lambda j: jnp.minimum(j, FFN_NT - 1)
    return pl.pallas_call(
        _ffn_up_kernel,
        grid=(R_MAIN // FFN_TM, nt_pad),
        in_specs=[
            pl.BlockSpec((FFN_TM, D_MODEL), lambda i, j: (i, 0), pipeline_mode=pl.Buffered(1)),
            pl.BlockSpec((D_MODEL, FFN_TN), lambda i, j: (0, jw(j))),
            pl.BlockSpec((D_MODEL, FFN_TN), lambda i, j: (0, jw(j))),
            pl.BlockSpec((FFN_TN, D_MODEL), lambda i, j: (jnp.where(i == 0, jw(j), FFN_NT - 1), 0)),
        ],
        out_specs=[
            pl.BlockSpec((FFN_TM, FFN_TN), lambda i, j: (i, j)),
            pl.BlockSpec((FFN_TN, D_MODEL), lambda i, j: (jnp.where(i == 0, j, nt_pad - 1), 0)),
        ],
        out_shape=[jax.ShapeDtypeStruct((R_MAIN, D_FF_PAD), bf16),
                   jax.ShapeDtypeStruct((D_FF_PAD, D_MODEL), bf16)],
        compiler_params=_params(2),
        name="ffn_up",
    )(hb, wg, wu, wd)


DOWN_TM, DOWN_TN, DOWN_TK = 2080, 1024, 1024


def _ffn_down_kernel(a_ref, w_ref, s1_ref, mu_ref, rs_ref, g_ref, b_ref, s_ref):
    @pl.when(pl.program_id(2) == 0)
    def _():
        s_ref[...] = (_ln_from_stats(s1_ref[...], mu_ref, rs_ref, g_ref[...], b_ref[...])
                      + _dot(a_ref[...], w_ref[...]))

    @pl.when(pl.program_id(2) > 0)
    def _():
        s_ref[...] += _dot(a_ref[...], w_ref[...])


def _ffn_down(act, wd, s1, mu, rs, g, b):
    return pl.pallas_call(
        _ffn_down_kernel,
        grid=(R_MAIN // DOWN_TM, D_MODEL // DOWN_TN, D_FF_PAD // DOWN_TK),
        in_specs=[
            pl.BlockSpec((DOWN_TM, DOWN_TK), lambda i, j, k: (i, k)),
            pl.BlockSpec((DOWN_TK, DOWN_TN), lambda i, j, k: (k, j)),
            pl.BlockSpec((DOWN_TM, DOWN_TN), lambda i, j, k: (i, j)),
            pl.BlockSpec((DOWN_TM, LANES), lambda i, j, k: (i, 0)),
            pl.BlockSpec((DOWN_TM, LANES), lambda i, j, k: (i, 0)),
            pl.BlockSpec((1, DOWN_TN), lambda i, j, k: (0, j)),
            pl.BlockSpec((1, DOWN_TN), lambda i, j, k: (0, j)),
        ],
        out_specs=pl.BlockSpec((DOWN_TM, DOWN_TN), lambda i, j, k: (i, j)),
        out_shape=jax.ShapeDtypeStruct((R_MAIN, D_MODEL), f32),
        compiler_params=_params(3),
        name="ffn_down",
    )(act, wd, s1, mu, rs, g, b)


def kernel(x_prompt, x_sample, state_gla, state_conv, meta_tokens, emb_ln_g, emb_ln_b,
           w_in, w_a2, b_a, gla_norm_g, conv_w, w_out, ln1_g, ln1_b,
           w_ffn_gate, w_ffn_up, w_ffn_down, ln2_g, ln2_b):
    assert x_prompt.shape == (BATCH, SEQ, D_MODEL) and x_sample.shape == (DEC_BATCH, 1, D_MODEL)
    assert w_in.shape[0] == 1, "single layer"
    row = lambda v: v.reshape(1, -1)

    w_in_t = jnp.transpose(w_in[0])
    wa2p = jnp.pad(w_a2[0], ((0, LANES - GATE_RANK), (0, 0)))

    xp = x_prompt.reshape(NP, D_MODEL)
    xt = jnp.concatenate([x_sample.reshape(DEC_BATCH, D_MODEL), meta_tokens.astype(f32),
                          jnp.zeros((LN_ROWS - DEC_BATCH - N_META, D_MODEL), f32)], axis=0)

    g0, b0 = row(emb_ln_g), row(emb_ln_b)
    g1, b1 = row(ln1_g[0]), row(ln1_b[0])
    hb, alr, mu0, rs0, h_tail = _ln0(xp, xt, g0, b0, w_in_t)
    proj, la2 = _proj(hb, w_in_t, alr, wa2p, row(b_a[0]))

    ng = row(gla_norm_g[0])
    s_meta = _gla_meta(proj, la2)
    og, s_p, og_s, s_s = _gla(proj, la2, ng, s_meta, state_gla[0])
    og = lax.dynamic_update_slice(og, og_s, (NP, 0))
    yc, nb_p = _conv_prompt(proj, conv_w[0])
    yc, nb_s = _conv_sample(proj, state_conv[0].reshape(DEC_BATCH, 2 * D_CONV), conv_w[0], yc)

    s1 = _outproj(og, yc, w_out[0], xp, mu0, rs0, ALPHA * g0, ALPHA * b0, h_tail)
    h1b, mu1, rs1 = _ln1(s1, g1, b1)
    act, wd = _ffn_up(h1b, w_ffn_gate[0], w_ffn_up[0], w_ffn_down[0])
    s2 = _ffn_down(act, wd, s1, mu1, rs1, ALPHA * g1, ALPHA * b1)
    y_p, y_s = _ln2(s2, row(ln2_g[0]), row(ln2_b[0]))

    return (y_p.reshape(BATCH, SEQ, D_MODEL),
            y_s.reshape(DEC_BATCH, 1, D_MODEL),
            s_p[None],
            nb_p[None],
            s_s[None],
            nb_s.reshape(1, DEC_BATCH, CONV_WIDTH - 1, D_CONV))
```

```python
import jax
import jax.numpy as jnp
import numpy as np
from jax import lax
from jax.experimental import pallas as pl
from jax.experimental.pallas import tpu as pltpu

f32 = jnp.float32
bf16 = jnp.bfloat16

D_MODEL = 4096
BATCH = 4
SEQ = 2048
DEC_BATCH = 128
N_META = 16
D_GLA_V = D_MODEL // 2
D_CONV = D_MODEL - D_GLA_V
D_GLA_K = D_GLA_V // 2
N_HEADS = 4
DK = D_GLA_K // N_HEADS
DV = D_GLA_V // N_HEADS
GATE_RANK = 16
GATE_TAU = 16.0
CONV_WIDTH = 3
D_FF = 11008
ALPHA = 2.0 ** 0.25
LN_EPS = 1e-5
RMS_EPS = 1e-6

LANES = 128
SUBLANES = 8
NP = BATCH * SEQ
R_MAIN = NP + DEC_BATCH
R_ALL = R_MAIN + LANES
D_PROJ = 2 * D_GLA_K + 2 * D_GLA_V + 3 * D_CONV
D_FF_PAD = 11264
CHUNK = 128
N_CHUNKS = SEQ // CHUNK
VMEM_LIMIT = 58 * 1024 * 1024

COL_Q, COL_K, COL_V, COL_G = 0, D_GLA_K, 2 * D_GLA_K, 2 * D_GLA_K + D_GLA_V
COL_CB = COL_G + D_GLA_V
COL_CC = COL_CB + D_CONV
COL_CH = COL_CC + D_CONV


def _params(n_axes):
    return pltpu.CompilerParams(
        dimension_semantics=("arbitrary",) * n_axes, vmem_limit_bytes=VMEM_LIMIT)


def _ln_stats(x):
    mu = jnp.mean(x, axis=-1, keepdims=True)
    xc = x - mu
    var = jnp.mean(xc * xc, axis=-1, keepdims=True)
    return mu, lax.rsqrt(var + LN_EPS)


def _layer_norm(x, g, b):
    mu, rs = _ln_stats(x)
    return (x - mu) * rs * g + b


def _ln_from_stats(x, mu_ref, rs_ref, g, b):
    reps = x.shape[-1] // LANES
    mu = jnp.concatenate([mu_ref[...]] * reps, axis=1)
    rs = jnp.concatenate([rs_ref[...]] * reps, axis=1)
    return (x - mu) * rs * g + b


def _store_stats(mu, rs, mu_ref, rs_ref):
    mu_ref[...] = jnp.broadcast_to(mu, mu_ref.shape)
    rs_ref[...] = jnp.broadcast_to(rs, rs_ref.shape)


def _split2(x):
    hi = x.astype(bf16)
    return hi, (x - hi.astype(f32)).astype(bf16)


def _dot(a, b, dims=(((1,), (0,)), ((), ()))):
    return lax.dot_general(a, b, dims, preferred_element_type=f32)


_NT = (((1,), (1,)), ((), ()))
_TN = (((0,), (0,)), ((), ()))


def _dot_f32(a, b):
    ah, am = _split2(a)
    bh, bm = _split2(b)
    return _dot(ah, bh) + _dot(ah, bm) + _dot(am, bh)


def _log2(n):
    assert n & (n - 1) == 0
    return n.bit_length() - 1


def _log_sigmoid(z):
    return jnp.minimum(z, 0.0) - jnp.log(1.0 + jnp.exp(-jnp.abs(z)))


LN_ROWS = 256


OUT_TM, OUT_TN = 1664, 256
TAIL_ROW0 = (R_MAIN // OUT_TM - 1) * OUT_TM
assert TAIL_ROW0 <= NP and TAIL_ROW0 % LN_ROWS == 0


def _ln0_kernel(xp_ref, xt_ref, g_ref, b_ref, wa_ref,
                hb_ref, alr_ref, mu_ref, rs_ref, ht_ref, wab_ref):
    i = pl.program_id(0)

    @pl.when(i == 0)
    def _():
        wab_ref[...] = wa_ref[...].astype(bf16)

    def emit(x):
        mu, rs = _ln_stats(x)
        h = (x - mu) * rs * g_ref[...] + b_ref[...]
        hb = h.astype(bf16)
        hb_ref[...] = hb
        alr_ref[...] = _dot(hb, wab_ref[...], _NT)
        _store_stats(mu, rs, mu_ref, rs_ref)
        ht_ref[...] = h

    @pl.when(i < NP // LN_ROWS)
    def _():
        emit(xp_ref[...])

    @pl.when(i >= NP // LN_ROWS)
    def _():
        emit(xt_ref[...])


def _ln0(xp, xt, g, b, w_in_t):
    n_p = NP // LN_ROWS
    t0 = TAIL_ROW0 // LN_ROWS
    return pl.pallas_call(
        _ln0_kernel,
        grid=(R_ALL // LN_ROWS,),
        in_specs=[
            pl.BlockSpec((LN_ROWS, D_MODEL), lambda i: (jnp.minimum(i, n_p - 1), 0)),
            pl.BlockSpec((LN_ROWS, D_MODEL), lambda i: (0, 0)),
            pl.BlockSpec((1, D_MODEL), lambda i: (0, 0)),
            pl.BlockSpec((1, D_MODEL), lambda i: (0, 0)),
            pl.BlockSpec((LANES, D_MODEL), lambda i: (COL_CB // LANES, 0)),
        ],
        out_specs=[
            pl.BlockSpec((LN_ROWS, D_MODEL), lambda i: (i, 0)),
            pl.BlockSpec((LN_ROWS, LANES), lambda i: (i, 0)),
            pl.BlockSpec((LN_ROWS, LANES), lambda i: (i, 0)),
            pl.BlockSpec((LN_ROWS, LANES), lambda i: (i, 0)),
            pl.BlockSpec((LN_ROWS, D_MODEL), lambda i: (jnp.maximum(i - t0, 0), 0)),
        ],
        out_shape=[jax.ShapeDtypeStruct((R_ALL, D_MODEL), bf16),
                   jax.ShapeDtypeStruct((R_ALL, LANES), f32),
                   jax.ShapeDtypeStruct((R_ALL, LANES), f32),
                   jax.ShapeDtypeStruct((R_ALL, LANES), f32),
                   jax.ShapeDtypeStruct((R_ALL - TAIL_ROW0, D_MODEL), f32)],
        scratch_shapes=[pltpu.VMEM((LANES, D_MODEL), bf16)],
        compiler_params=_params(1),
        name="ln0",
    )(xp, xt, g, b, w_in_t)


PROJ_TM, PROJ_TN = 1408, 512


LA_W = 2 * DK


def _proj_kernel(h_ref, w_ref, alr_ref, wa2_ref, ba_ref, p_ref, la_ref):
    j = pl.program_id(1)

    @pl.when(j < N_HEADS)
    def _():
        p_ref[...] = _dot(h_ref[...], w_ref[...].astype(bf16), _NT).astype(bf16)
        la_ref[:, 0:DK], la_ref[:, DK:LA_W] = _split2(
            _gate_log_decay(alr_ref[...], wa2_ref[...], ba_ref[...]))

    is_g = jnp.logical_and(j * PROJ_TN >= COL_G, j * PROJ_TN < COL_CB)

    @pl.when(is_g)
    def _():
        p = _dot(h_ref[...], w_ref[...].astype(bf16), _NT)
        p_ref[...] = (p * jax.nn.sigmoid(p)).astype(bf16)

    @pl.when(jnp.logical_and(j >= N_HEADS, jnp.logical_not(is_g)))
    def _():
        p_ref[...] = _dot(h_ref[...], w_ref[...].astype(bf16), _NT).astype(bf16)


def _proj(hb, w_in_t, alr, wa2p, ba):
    assert COL_G >= N_HEADS * PROJ_TN

    def w_row(j):
        return pl.multiple_of(j * PROJ_TN + jnp.where(j * PROJ_TN >= COL_CB, GATE_RANK, 0), GATE_RANK)

    jh = lambda j: jnp.minimum(j, N_HEADS - 1)
    return pl.pallas_call(
        _proj_kernel,
        grid=(R_ALL // PROJ_TM, D_PROJ // PROJ_TN),
        in_specs=[
            pl.BlockSpec((PROJ_TM, D_MODEL), lambda i, j: (i, 0)),
            pl.BlockSpec((pl.Element(PROJ_TN), pl.Element(D_MODEL)), lambda i, j: (w_row(j), 0)),
            pl.BlockSpec((PROJ_TM, LANES), lambda i, j: (i, 0)),
            pl.BlockSpec((LANES, DK), lambda i, j: (0, jh(j))),
            pl.BlockSpec((1, DK), lambda i, j: (0, jh(j))),
        ],
        out_specs=[
            pl.BlockSpec((PROJ_TM, PROJ_TN), lambda i, j: (i, j)),
            pl.BlockSpec((PROJ_TM, LA_W), lambda i, j: (i, jh(j))),
        ],
        out_shape=[jax.ShapeDtypeStruct((R_ALL, D_PROJ), bf16),
                   jax.ShapeDtypeStruct((R_ALL, N_HEADS * LA_W), bf16)],
        compiler_params=_params(2),
        name="proj",
    )(hb, w_in_t, alr, wa2p, ba)


GLA_BLK = 128
GLA_HPS = 4
GLA_ILV = 2
LVL_DIAG = _log2(GLA_BLK)


def _gla_levels(c):
    return [c >> (t + 1) for t in range(_log2(c))]


def _gla_tables(c):
    idx = np.arange(c)
    tri = (idx[None, :] <= idx[:, None]).astype(np.float32)
    mats = [tri]
    for m in _gla_levels(c):
        ref = ((idx // m) | 1) * m - 1
        mats.append(tri - tri[ref])
    diff = np.concatenate(mats, axis=0)
    diff = np.concatenate([diff, diff], axis=1)
    t = np.arange(min(c, GLA_BLK))
    level = np.full((t.size, t.size), -1, np.int32)
    for m in _gla_levels(t.size):
        same_pair = (t[:, None] // (2 * m)) == (t[None, :] // (2 * m))
        odd_even = (((t[:, None] // m) & 1) == 1) & (((t[None, :] // m) & 1) == 0)
        level[same_pair & odd_even] = _log2(m)
    level[t[:, None] == t[None, :]] = LVL_DIAG
    return jnp.asarray(diff, bf16), jnp.asarray(level)


def _gate_log_decay(alr, wa2, ba):
    return _log_sigmoid(_dot_f32(alr, wa2) + ba) * (1.0 / GATE_TAU)


def _decay_sums(la2, diff_ref, x_ref):
    x_ref[...] = _dot(diff_ref[...], jnp.concatenate([la2[:, 0:DK], la2[:, DK:LA_W]], axis=0))


GLA_STEPS = BATCH * N_CHUNKS
SAMPLE_PER_STEP = DEC_BATCH // GLA_STEPS


def _gla_kernel(q_ref, k_ref, v_ref, g_ref, la_ref, ng_ref, s0_ref, diff_ref, lvl_ref,
                qs_ref, ks_ref, vs_ref, gs_ref, las_ref, s0s_ref,
                og_ref, sfin_ref, ss_ref, ogacc_ref, *head_scratch):
    step = pl.program_id(0)

    @pl.when(step < GLA_STEPS)
    def _():
        _gla_step(step, q_ref, k_ref, v_ref, g_ref, la_ref, ng_ref, s0_ref, diff_ref, lvl_ref,
                  qs_ref, ks_ref, vs_ref, gs_ref, las_ref, s0s_ref,
                  og_ref, sfin_ref, ss_ref, ogacc_ref, *head_scratch)

    @pl.when(step == GLA_STEPS)
    def _():
        og_ref[...] = ogacc_ref[...].astype(bf16)


def _gla_step(step, q_ref, k_ref, v_ref, g_ref, la_ref, ng_ref, s0_ref, diff_ref, lvl_ref,
              qs_ref, ks_ref, vs_ref, gs_ref, las_ref, s0s_ref,
              og_ref, sfin_ref, ss_ref, ogacc_ref, *head_scratch):
    s = step % N_CHUNKS
    st_refs, x_refs = head_scratch[:GLA_HPS], head_scratch[GLA_HPS:]

    @pl.when(s == 0)
    def _():
        for hh in range(GLA_HPS):
            st_refs[hh][...] = s0_ref[hh]

    @pl.when(step == 0)
    def _():
        ogacc_ref[...] = jnp.zeros_like(ogacc_ref)

    for h0 in range(0, GLA_HPS, GLA_ILV):
        heads = [_gla_prompt_head(hh, q_ref, k_ref, v_ref, g_ref, la_ref, ng_ref,
                                  diff_ref, lvl_ref, og_ref, st_refs[hh], x_refs[hh])
                 for hh in range(h0, h0 + GLA_ILV)]
        while heads:
            heads = [h for h in heads if next(h, _DONE) is not _DONE]

    rows = lax.broadcasted_iota(jnp.int32, (DEC_BATCH, 1), 0)
    for hh in range(N_HEADS):
        vcols = slice(hh * DV, (hh + 1) * DV)
        acc = ogacc_ref[:, vcols]
        for t in range(SAMPLE_PER_STEP):
            r = step * SAMPLE_PER_STEP + t
            og_row = _gla_sample_token(r, t, hh, qs_ref, ks_ref, vs_ref, gs_ref, las_ref, ng_ref,
                                       s0s_ref, ss_ref)
            acc = jnp.where(rows == r, og_row, acc)
        ogacc_ref[:, vcols] = acc

    @pl.when(s == N_CHUNKS - 1)
    def _():
        for hh in range(GLA_HPS):
            sfin_ref[0, hh] = st_refs[hh][...].T


def _gla_sample_token(r, t, hh, qs_ref, ks_ref, vs_ref, gs_ref, las_ref, ng_ref, s0s_ref, ss_ref):
    kcols = slice(hh * DK, (hh + 1) * DK)
    vcols = slice(hh * DV, (hh + 1) * DV)
    as_col = (lax.broadcasted_iota(jnp.int32, (DEC_BATCH, LANES), 0) == r).astype(bf16)
    as_row = (lax.broadcasted_iota(jnp.int32, (SUBLANES, DEC_BATCH), 1) == r).astype(bf16)
    la_col = (_dot(las_ref[:, hh * LA_W:hh * LA_W + DK], as_col, _TN)
              + _dot(las_ref[:, hh * LA_W + DK:(hh + 1) * LA_W], as_col, _TN))
    a_col = jnp.exp(la_col)
    k_col = _dot(ks_ref[:, kcols], as_col, _TN)
    q_col = _dot(qs_ref[:, kcols], as_col, _TN) * (DK ** -0.5)
    v_row = _dot(as_row, vs_ref[:, vcols])[0:1]
    g_row = _dot(as_row, gs_ref[:, vcols])[0:1]
    wide = lambda c: jnp.concatenate([c] * (DV // LANES), axis=1)
    s_new = wide(a_col) * s0s_ref[t, hh] + wide(k_col) * v_row
    ss_ref[t, hh] = s_new
    o = jnp.sum(wide(q_col) * s_new, axis=0, keepdims=True)
    ms = jnp.mean(o * o, axis=-1, keepdims=True)
    return o * lax.rsqrt(ms + RMS_EPS) * ng_ref[...] * g_row


_DONE = object()


def _gla_prompt_head(hh, q_ref, k_ref, v_ref, g_ref, la_ref, ng_ref,
                     diff_ref, lvl_ref, og_ref, st_ref, x_ref):
    C = CHUNK
    nblk = C // GLA_BLK
    kcols = slice(hh * DK, (hh + 1) * DK)
    vcols = slice(hh * DV, (hh + 1) * DV)

    _decay_sums(la_ref[:, hh * LA_W:(hh + 1) * LA_W], diff_ref, x_ref)
    yield
    kb = k_ref[:, kcols]
    kf = kb.astype(f32)
    vb = v_ref[:, vcols]
    qf = q_ref[:, kcols].astype(f32) * (DK ** -0.5)
    qb = qf.astype(bf16)
    b = x_ref[0:C, :]
    b_last = x_ref[C - 1:C, :]

    lvl = lvl_ref[...]
    blk_rows = lambda a, i: a[i * GLA_BLK:(i + 1) * GLA_BLK]
    tiles = [jnp.where(lvl == LVL_DIAG, _dot(blk_rows(qb, i), blk_rows(kb, i), _NT), 0.0)
             for i in range(nblk)]
    cross = None
    row = lax.broadcasted_iota(jnp.int32, (C, 1), 0)
    for t, m in enumerate(_gla_levels(C)):
        yield
        e = jnp.exp(-jnp.abs(x_ref[(t + 1) * C:(t + 2) * C, :]))
        if m == GLA_BLK:
            qt = (blk_rows(qf, 1) * blk_rows(e, 1)).astype(bf16)
            kt = (blk_rows(kf, 0) * blk_rows(e, 0)).astype(bf16)
            cross = _dot(qt, kt, _NT)
            continue
        if m >= 8:
            n = C // (2 * m)
            e3, q3, k3 = (a.reshape(n, 2 * m, DK) for a in (e, qf, kf))
            zero = jnp.zeros((n, m, DK), f32)
            qt = jnp.concatenate([zero, q3[:, m:, :] * e3[:, m:, :]], axis=1).reshape(C, DK)
            kt = jnp.concatenate([k3[:, :m, :] * e3[:, :m, :], zero], axis=1).reshape(C, DK)
        else:
            odd = ((row >> _log2(m)) & 1) == 1
            qt = jnp.where(odd, qf * e, 0.0)
            kt = jnp.where(odd, 0.0, kf * e)
        qt, kt = qt.astype(bf16), kt.astype(bf16)
        tiles = [jnp.where(lvl == _log2(m), _dot(blk_rows(qt, i), blk_rows(kt, i), _NT), tiles[i])
                 for i in range(nblk)]

    yield
    st = st_ref[...]
    qe = (qf * jnp.exp(b)).astype(bf16)
    outs = []
    for i in range(nblk):
        lhs = tiles[i] if i == 0 else jnp.concatenate([cross, tiles[i]], axis=1)
        outs.append(_dot(lhs.astype(bf16), vb[0:(i + 1) * GLA_BLK]))
    o = _dot(qe, st.astype(bf16), _NT) + jnp.concatenate(outs, axis=0)
    ms = jnp.mean(o * o, axis=-1, keepdims=True)
    on = o * lax.rsqrt(ms + RMS_EPS) * ng_ref[...]
    og_ref[:, vcols] = (on * g_ref[:, vcols].astype(f32)).astype(bf16)

    yield
    kd = (kf * jnp.exp(b_last - b)).astype(bf16)
    st_new = st * jnp.exp(b_last) + _dot(vb, kd, _TN)
    st_ref[...] = st_new


def _gla(proj, la2, ng, s_meta, s0_sample):
    assert CHUNK == GLA_BLK == DEC_BATCH and GLA_HPS == N_HEADS
    assert DEC_BATCH % GLA_STEPS == 0
    diff, level = _gla_tables(CHUNK)
    sps = SAMPLE_PER_STEP
    const = lambda t: (0, 0)
    work = lambda t: jnp.minimum(t, GLA_STEPS - 1)
    srow = NP // DEC_BATCH
    pcol = lambda col, width: (lambda t: (work(t), col // width))
    scol = lambda col, width: (lambda t: (srow, col // width))
    return pl.pallas_call(
        _gla_kernel,
        grid=(GLA_STEPS + 1,),
        in_specs=[
            pl.BlockSpec((CHUNK, D_GLA_K), pcol(COL_Q, D_GLA_K)),
            pl.BlockSpec((CHUNK, D_GLA_K), pcol(COL_K, D_GLA_K)),
            pl.BlockSpec((CHUNK, D_GLA_V), pcol(COL_V, D_GLA_V)),
            pl.BlockSpec((CHUNK, D_GLA_V), pcol(COL_G, D_GLA_V)),
            pl.BlockSpec((CHUNK, N_HEADS * LA_W), lambda t: (work(t), 0)),
            pl.BlockSpec((1, DV), const),
            pl.BlockSpec((N_HEADS, DV, DK), lambda t: (0, 0, 0)),
            pl.BlockSpec(diff.shape, const),
            pl.BlockSpec(level.shape, const),
            pl.BlockSpec((DEC_BATCH, D_GLA_K), scol(COL_Q, D_GLA_K)),
            pl.BlockSpec((DEC_BATCH, D_GLA_K), scol(COL_K, D_GLA_K)),
            pl.BlockSpec((DEC_BATCH, D_GLA_V), scol(COL_V, D_GLA_V)),
            pl.BlockSpec((DEC_BATCH, D_GLA_V), scol(COL_G, D_GLA_V)),
            pl.BlockSpec((DEC_BATCH, N_HEADS * LA_W), lambda t: (srow, 0)),
            pl.BlockSpec((sps, N_HEADS, DK, DV), lambda t: (work(t), 0, 0, 0)),
        ],
        out_specs=[
            pl.BlockSpec((CHUNK, D_GLA_V), lambda t: (t, 0)),
            pl.BlockSpec((1, N_HEADS, DK, DV), lambda t: (work(t) // N_CHUNKS, 0, 0, 0)),
            pl.BlockSpec((sps, N_HEADS, DK, DV), lambda t: (work(t), 0, 0, 0)),
        ],
        out_shape=[jax.ShapeDtypeStruct((R_MAIN, D_GLA_V), bf16),
                   jax.ShapeDtypeStruct((BATCH, N_HEADS, DK, DV), f32),
                   jax.ShapeDtypeStruct((DEC_BATCH, N_HEADS, DK, DV), f32)],
        scratch_shapes=([pltpu.VMEM((DEC_BATCH, D_GLA_V), f32)]
                        + [pltpu.VMEM((DV, DK), f32)] * N_HEADS
                        + [pltpu.VMEM((diff.shape[0], DK), f32)] * N_HEADS),
        compiler_params=_params(1),
        name="gla",
    )(proj, proj, proj, proj, la2, ng, s_meta, diff, level,
      proj, proj, proj, proj, la2, s0_sample)


def _gla_meta_kernel(k_ref, v_ref, la_ref, diff_ref, s_ref, x_ref):
    _decay_sums(la_ref[...], diff_ref, x_ref)
    kd = (k_ref[...].astype(f32) * jnp.exp(x_ref[N_META - 1:N_META, :] - x_ref[...])).astype(bf16)
    s_ref[0] = _dot(v_ref[...], kd, _TN)


def _gla_meta(proj, la2):
    diff = _gla_tables(N_META)[0][:N_META]
    mb = R_MAIN // N_META
    return pl.pallas_call(
        _gla_meta_kernel,
        grid=(N_HEADS,),
        in_specs=[
            pl.BlockSpec((N_META, DK), lambda h: (mb, COL_K // DK + h)),
            pl.BlockSpec((N_META, DV), lambda h: (mb, COL_V // DV + h)),
            pl.BlockSpec((N_META, LA_W), lambda h: (mb, h)),
            pl.BlockSpec(diff.shape, lambda h: (0, 0)),
        ],
        out_specs=pl.BlockSpec((1, DV, DK), lambda h: (h, 0, 0)),
        out_shape=jax.ShapeDtypeStruct((N_HEADS, DV, DK), f32),
        scratch_shapes=[pltpu.VMEM((N_META, DK), f32)],
        compiler_params=_params(1),
        name="gla_meta",
    )(proj, proj, la2, diff)


CONV_ROWS = DEC_BATCH
CONV_PAD = SUBLANES
CONV_SEQ_BLOCKS = SEQ // CONV_ROWS
CONV_PROMPT_BLOCKS = NP // CONV_ROWS


def _conv_kernel(cb_ref, cc_ref, ch_ref, mc_ref, mh_ref, w_ref, buf_ref,
                 y_ref, nbp_ref, nbs_ref, u_ref):
    t = pl.program_id(0)
    TR = CONV_ROWS
    u = cc_ref[...].astype(f32) * ch_ref[...].astype(f32)
    cb = cb_ref[...].astype(f32)
    w = w_ref[...]

    @pl.when(t < CONV_PROMPT_BLOCKS)
    def _():
        @pl.when(t % CONV_SEQ_BLOCKS == 0)
        def _():
            mu = mc_ref[...].astype(f32) * mh_ref[...].astype(f32)
            u_ref[CONV_PAD - 2:CONV_PAD, :] = mu[N_META - 2:N_META, :]

        u_ref[CONV_PAD:CONV_PAD + TR, :] = u
        zc = (w[0:1, :] * u_ref[CONV_PAD - 2:CONV_PAD - 2 + TR, :]
              + w[1:2, :] * u_ref[CONV_PAD - 1:CONV_PAD - 1 + TR, :]
              + w[2:3, :] * u)
        y_ref[...] = (cb * zc).astype(bf16)
        last = u[TR - 2:TR, :]
        u_ref[CONV_PAD - 2:CONV_PAD, :] = last

        @pl.when(t % CONV_SEQ_BLOCKS == CONV_SEQ_BLOCKS - 1)
        def _():
            nbp_ref[0] = last

    @pl.when(t == CONV_PROMPT_BLOCKS)
    def _():
        b0 = buf_ref[:, 0:D_CONV]
        b1 = buf_ref[:, D_CONV:2 * D_CONV]
        y_ref[...] = (cb * (w[0:1, :] * b0 + w[1:2, :] * b1 + w[2:3, :] * u)).astype(bf16)
        nbs_ref[:, 0:D_CONV] = b1
        nbs_ref[:, D_CONV:2 * D_CONV] = u


def _conv(proj, conv_w, buf):
    TR = CONV_ROWS
    mrow = R_MAIN // N_META
    const = lambda t: (0, 0)
    pcol = lambda col: (lambda t: (t, col // D_CONV))
    return pl.pallas_call(
        _conv_kernel,
        grid=(R_MAIN // TR,),
        in_specs=[
            pl.BlockSpec((TR, D_CONV), pcol(COL_CB)),
            pl.BlockSpec((TR, D_CONV), pcol(COL_CC)),
            pl.BlockSpec((TR, D_CONV), pcol(COL_CH)),
            pl.BlockSpec((N_META, D_CONV), lambda t: (mrow, COL_CC // D_CONV)),
            pl.BlockSpec((N_META, D_CONV), lambda t: (mrow, COL_CH // D_CONV)),
            pl.BlockSpec((CONV_WIDTH, D_CONV), const),
            pl.BlockSpec((DEC_BATCH, 2 * D_CONV), const),
        ],
        out_specs=[
            pl.BlockSpec((TR, D_CONV), lambda t: (t, 0)),
            pl.BlockSpec((1, CONV_WIDTH - 1, D_CONV),
                         lambda t: (jnp.minimum(t, CONV_PROMPT_BLOCKS - 1) // CONV_SEQ_BLOCKS, 0, 0)),
            pl.BlockSpec((DEC_BATCH, 2 * D_CONV), const),
        ],
        out_shape=[jax.ShapeDtypeStruct((R_MAIN, D_CONV), bf16),
                   jax.ShapeDtypeStruct((BATCH, CONV_WIDTH - 1, D_CONV), f32),
                   jax.ShapeDtypeStruct((DEC_BATCH, 2 * D_CONV), f32)],
        scratch_shapes=[pltpu.VMEM((CONV_PAD + TR, D_CONV), f32)],
        compiler_params=_params(1),
        name="conv",
    )(proj, proj, proj, proj, proj, conv_w, buf)


OUT_FULL = TAIL_ROW0 // OUT_TM


def _outproj_kernel(o_ref, y_ref, wo_ref, wy_ref, x_ref, mu_ref, rs_ref, g_ref, b_ref, ht_ref,
                    s_ref):
    i = pl.program_id(0)

    def mix():
        return (_dot(o_ref[...], wo_ref[...].astype(bf16))
                + _dot(y_ref[...], wy_ref[...].astype(bf16)))

    @pl.when(i < OUT_FULL)
    def _():
        s_ref[...] = _ln_from_stats(x_ref[...], mu_ref, rs_ref, g_ref[...], b_ref[...]) + mix()

    @pl.when(i >= OUT_FULL)
    def _():
        s_ref[...] = ALPHA * ht_ref[...] + mix()


def _outproj(og, yc, w_out, xp, mu, rs, g, b, h_tail):
    full = lambda i: i < OUT_FULL
    return pl.pallas_call(
        _outproj_kernel,
        grid=(R_MAIN // OUT_TM, D_MODEL // OUT_TN),
        in_specs=[
            pl.BlockSpec((OUT_TM, D_GLA_V), lambda i, j: (i, 0)),
            pl.BlockSpec((OUT_TM, D_CONV), lambda i, j: (i, 0)),
            pl.BlockSpec((D_GLA_V, OUT_TN), lambda i, j: (0, j)),
            pl.BlockSpec((D_CONV, OUT_TN), lambda i, j: (1, j)),
            pl.BlockSpec((OUT_TM, OUT_TN),
                         lambda i, j: (jnp.minimum(i, OUT_FULL - 1), jnp.where(full(i), j, 0))),
            pl.BlockSpec((OUT_TM, LANES), lambda i, j: (i, 0)),
            pl.BlockSpec((OUT_TM, LANES), lambda i, j: (i, 0)),
            pl.BlockSpec((1, OUT_TN), lambda i, j: (0, j)),
            pl.BlockSpec((1, OUT_TN), lambda i, j: (0, j)),
            pl.BlockSpec((OUT_TM, OUT_TN), lambda i, j: (0, jnp.where(full(i), 0, j))),
        ],
        out_specs=pl.BlockSpec((OUT_TM, OUT_TN), lambda i, j: (i, j)),
        out_shape=jax.ShapeDtypeStruct((R_MAIN, D_MODEL), f32),
        compiler_params=_params(2),
        name="outproj",
    )(og, yc, w_out, w_out, xp, mu, rs, g, b, h_tail)


LN1_ROWS = 320


def _ln1_kernel(s_ref, g_ref, b_ref, hb_ref, mu_ref, rs_ref):
    x = s_ref[...]
    mu, rs = _ln_stats(x)
    hb_ref[...] = ((x - mu) * rs * g_ref[...] + b_ref[...]).astype(bf16)
    _store_stats(mu, rs, mu_ref, rs_ref)


def _ln1(s1, g, b):
    return pl.pallas_call(
        _ln1_kernel,
        grid=(R_MAIN // LN1_ROWS,),
        in_specs=[
            pl.BlockSpec((LN1_ROWS, D_MODEL), lambda i: (i, 0)),
            pl.BlockSpec((1, D_MODEL), lambda i: (0, 0)),
            pl.BlockSpec((1, D_MODEL), lambda i: (0, 0)),
        ],
        out_specs=[
            pl.BlockSpec((LN1_ROWS, D_MODEL), lambda i: (i, 0)),
            pl.BlockSpec((LN1_ROWS, LANES), lambda i: (i, 0)),
            pl.BlockSpec((LN1_ROWS, LANES), lambda i: (i, 0)),
        ],
        out_shape=[jax.ShapeDtypeStruct((R_MAIN, D_MODEL), bf16),
                   jax.ShapeDtypeStruct((R_MAIN, LANES), f32),
                   jax.ShapeDtypeStruct((R_MAIN, LANES), f32)],
        compiler_params=_params(1),
        name="ln1",
    )(s1, g, b)


LN2_ROWS = 512


def _ln2_kernel(sp_ref, ss_ref, g_ref, b_ref, yp_ref, ys_ref):
    yp_ref[...] = _layer_norm(sp_ref[...], g_ref[...], b_ref[...])

    @pl.when(pl.program_id(0) == 0)
    def _():
        ys_ref[...] = _layer_norm(ss_ref[...], g_ref[...], b_ref[...])


def _ln2(s2, g, b):
    return pl.pallas_call(
        _ln2_kernel,
        grid=(NP // LN2_ROWS,),
        in_specs=[
            pl.BlockSpec((LN2_ROWS, D_MODEL), lambda i: (i, 0)),
            pl.BlockSpec((DEC_BATCH, D_MODEL), lambda i: (NP // DEC_BATCH, 0)),
            pl.BlockSpec((1, D_MODEL), lambda i: (0, 0)),
            pl.BlockSpec((1, D_MODEL), lambda i: (0, 0)),
        ],
        out_specs=[
            pl.BlockSpec((LN2_ROWS, D_MODEL), lambda i: (i, 0)),
            pl.BlockSpec((DEC_BATCH, D_MODEL), lambda i: (0, 0)),
        ],
        out_shape=[jax.ShapeDtypeStruct((NP, D_MODEL), f32),
                   jax.ShapeDtypeStruct((DEC_BATCH, D_MODEL), f32)],
        compiler_params=_params(1),
        name="ln2",
    )(s2, s2, g, b)


FFN_TM, FFN_TN = 2080, 256


FFN_NT = D_FF // FFN_TN


def _ffn_up_kernel(h_ref, wg_ref, wu_ref, wd_ref, act_ref, wdb_ref):
    i = pl.program_id(0)
    j = pl.program_id(1)

    @pl.when(j < FFN_NT)
    def _():
        h = h_ref[...]
        a = _dot(h, wg_ref[...].astype(bf16))
        u = _dot(h, wu_ref[...].astype(bf16))
        act_ref[...] = (a * jax.nn.sigmoid(a) * u).astype(bf16)

    @pl.when(j >= FFN_NT)
    def _():
        act_ref[...] = jnp.zeros_like(act_ref)

    @pl.when(jnp.logical_and(i == 0, j < FFN_NT))
    def _():
        wdb_ref[...] = wd_ref[...].astype(bf16)

    @pl.when(jnp.logical_and(i == 0, j >= FFN_NT))
    def _():
        wdb_ref[...] = jnp.zeros_like(wdb_ref)


def _ffn_up(hb, wg, wu, wd):
    nt_pad = D_FF_PAD // FFN_TN
    jw = lambda j: jnp.minimum(j, FFN_NT - 1)
    return pl.pallas_call(
        _ffn_up_kernel,
        grid=(R_MAIN // FFN_TM, nt_pad),
        in_specs=[
            pl.BlockSpec((FFN_TM, D_MODEL), lambda i, j: (i, 0), pipeline_mode=pl.Buffered(1)),
            pl.BlockSpec((D_MODEL, FFN_TN), lambda i, j: (0, jw(j))),
            pl.BlockSpec((D_MODEL, FFN_TN), lambda i, j: (0, jw(j))),
            pl.BlockSpec((FFN_TN, D_MODEL), lambda i, j: (jnp.where(i == 0, jw(j), FFN_NT - 1), 0)),
        ],
        out_specs=[
            pl.BlockSpec((FFN_TM, FFN_TN), lambda i, j: (i, j)),
            pl.BlockSpec((FFN_TN, D_MODEL), lambda i, j: (jnp.where(i == 0, j, nt_pad - 1), 0)),
        ],
        out_shape=[jax.ShapeDtypeStruct((R_MAIN, D_FF_PAD), bf16),
                   jax.ShapeDtypeStruct((D_FF_PAD, D_MODEL), bf16)],
        compiler_params=_params(2),
        name="ffn_up",
    )(hb, wg, wu, wd)


DOWN_TM, DOWN_TN, DOWN_TK = 2080, 1024, 1024


def _ffn_down_kernel(a_ref, w_ref, s1_ref, mu_ref, rs_ref, g_ref, b_ref, s_ref):
    @pl.when(pl.program_id(2) == 0)
    def _():
        s_ref[...] = (_ln_from_stats(s1_ref[...], mu_ref, rs_ref, g_ref[...], b_ref[...])
                      + _dot(a_ref[...], w_ref[...]))

    @pl.when(pl.program_id(2) > 0)
    def _():
        s_ref[...] += _dot(a_ref[...], w_ref[...])


def _ffn_down(act, wd, s1, mu, rs, g, b):
    return pl.pallas_call(
        _ffn_down_kernel,
        grid=(R_MAIN // DOWN_TM, D_MODEL // DOWN_TN, D_FF_PAD // DOWN_TK),
        in_specs=[
            pl.BlockSpec((DOWN_TM, DOWN_TK), lambda i, j, k: (i, k)),
            pl.BlockSpec((DOWN_TK, DOWN_TN), lambda i, j, k: (k, j)),
            pl.BlockSpec((DOWN_TM, DOWN_TN), lambda i, j, k: (i, j)),
            pl.BlockSpec((DOWN_TM, LANES), lambda i, j, k: (i, 0)),
            pl.BlockSpec((DOWN_TM, LANES), lambda i, j, k: (i, 0)),
            pl.BlockSpec((1, DOWN_TN), lambda i, j, k: (0, j)),
            pl.BlockSpec((1, DOWN_TN), lambda i, j, k: (0, j)),
        ],
        out_specs=pl.BlockSpec((DOWN_TM, DOWN_TN), lambda i, j, k: (i, j)),
        out_shape=jax.ShapeDtypeStruct((R_MAIN, D_MODEL), f32),
        compiler_params=_params(3),
        name="ffn_down",
    )(act, wd, s1, mu, rs, g, b)


def kernel(x_prompt, x_sample, state_gla, state_conv, meta_tokens, emb_ln_g, emb_ln_b,
           w_in, w_a2, b_a, gla_norm_g, conv_w, w_out, ln1_g, ln1_b,
           w_ffn_gate, w_ffn_up, w_ffn_down, ln2_g, ln2_b):
    assert x_prompt.shape == (BATCH, SEQ, D_MODEL) and x_sample.shape == (DEC_BATCH, 1, D_MODEL)
    assert w_in.shape[0] == 1, "single layer"
    row = lambda v: v.reshape(1, -1)

    w_in_t = jnp.transpose(w_in[0])
    wa2p = jnp.pad(w_a2[0], ((0, LANES - GATE_RANK), (0, 0)))

    xp = x_prompt.reshape(NP, D_MODEL)
    xt = jnp.concatenate([x_sample.reshape(DEC_BATCH, D_MODEL), meta_tokens.astype(f32),
                          jnp.zeros((LN_ROWS - DEC_BATCH - N_META, D_MODEL), f32)], axis=0)

    g0, b0 = row(emb_ln_g), row(emb_ln_b)
    g1, b1 = row(ln1_g[0]), row(ln1_b[0])
    hb, alr, mu0, rs0, h_tail = _ln0(xp, xt, g0, b0, w_in_t)
    proj, la2 = _proj(hb, w_in_t, alr, wa2p, row(b_a[0]))

    ng = row(gla_norm_g[0])
    s_meta = _gla_meta(proj, la2)
    og, s_p, s_s = _gla(proj, la2, ng, s_meta, state_gla[0])
    yc, nb_p, nb_s = _conv(proj, conv_w[0], state_conv[0].reshape(DEC_BATCH, 2 * D_CONV))

    s1 = _outproj(og, yc, w_out[0], xp, mu0, rs0, ALPHA * g0, ALPHA * b0, h_tail)
    h1b, mu1, rs1 = _ln1(s1, g1, b1)
    act, wd = _ffn_up(h1b, w_ffn_gate[0], w_ffn_up[0], w_ffn_down[0])
    s2 = _ffn_down(act, wd, s1, mu1, rs1, ALPHA * g1, ALPHA * b1)
    y_p, y_s = _ln2(s2, row(ln2_g[0]), row(ln2_b[0]))

    return (y_p.reshape(BATCH, SEQ, D_MODEL),
            y_s.reshape(DEC_BATCH, 1, D_MODEL),
            s_p[None],
            nb_p[None],
            s_s[None],
            nb_s.reshape(1, DEC_BATCH, CONV_WIDTH - 1, D_CONV))
```

```python
import jax
import jax.numpy as jnp
import numpy as np
from jax import lax
from jax.experimental import pallas as pl
from jax.experimental.pallas import tpu as pltpu

f32 = jnp.float32
bf16 = jnp.bfloat16

D_MODEL = 4096
BATCH = 4
SEQ = 2048
DEC_BATCH = 128
N_META = 16
D_GLA_V = D_MODEL // 2
D_CONV = D_MODEL - D_GLA_V
D_GLA_K = D_GLA_V // 2
N_HEADS = 4
DK = D_GLA_K // N_HEADS
DV = D_GLA_V // N_HEADS
GATE_RANK = 16
GATE_TAU = 16.0
CONV_WIDTH = 3
D_FF = 11008
ALPHA = 2.0 ** 0.25
LN_EPS = 1e-5
RMS_EPS = 1e-6

LANES = 128
SUBLANES = 8
NP = BATCH * SEQ
R_MAIN = NP + DEC_BATCH
R_ALL = R_MAIN + LANES
D_PROJ = 2 * D_GLA_K + 2 * D_GLA_V + 3 * D_CONV
D_FF_PAD = 11264
CHUNK = 128
N_CHUNKS = SEQ // CHUNK
VMEM_LIMIT = 58 * 1024 * 1024

COL_Q, COL_K, COL_V, COL_G = 0, D_GLA_K, 2 * D_GLA_K, 2 * D_GLA_K + D_GLA_V
COL_CB = COL_G + D_GLA_V
COL_CC = COL_CB + D_CONV
COL_CH = COL_CC + D_CONV


def _params(n_axes):
    return pltpu.CompilerParams(
        dimension_semantics=("arbitrary",) * n_axes, vmem_limit_bytes=VMEM_LIMIT)


def _ln_stats(x):
    mu = jnp.mean(x, axis=-1, keepdims=True)
    xc = x - mu
    var = jnp.mean(xc * xc, axis=-1, keepdims=True)
    return mu, lax.rsqrt(var + LN_EPS)


def _layer_norm(x, g, b):
    mu, rs = _ln_stats(x)
    return (x - mu) * rs * g + b


def _ln_from_stats(x, mu_ref, rs_ref, g, b):
    reps = x.shape[-1] // LANES
    mu = jnp.concatenate([mu_ref[...]] * reps, axis=1)
    rs = jnp.concatenate([rs_ref[...]] * reps, axis=1)
    return (x - mu) * rs * g + b


def _store_stats(mu, rs, mu_ref, rs_ref):
    mu_ref[...] = jnp.broadcast_to(mu, mu_ref.shape)
    rs_ref[...] = jnp.broadcast_to(rs, rs_ref.shape)


def _split2(x):
    hi = x.astype(bf16)
    return hi, (x - hi.astype(f32)).astype(bf16)


def _dot(a, b, dims=(((1,), (0,)), ((), ()))):
    return lax.dot_general(a, b, dims, preferred_element_type=f32)


_NT = (((1,), (1,)), ((), ()))
_TN = (((0,), (0,)), ((), ()))


def _dot_f32(a, b):
    ah, am = _split2(a)
    bh, bm = _split2(b)
    return _dot(ah, bh) + _dot(ah, bm) + _dot(am, bh)


def _log2(n):
    assert n & (n - 1) == 0
    return n.bit_length() - 1


def _log_sigmoid(z):
    return jnp.minimum(z, 0.0) - jnp.log(1.0 + jnp.exp(-jnp.abs(z)))


LN_ROWS = 256


OUT_TM, OUT_TN = 1664, 256
TAIL_ROW0 = (R_MAIN // OUT_TM - 1) * OUT_TM
assert TAIL_ROW0 <= NP and TAIL_ROW0 % LN_ROWS == 0


def _ln0_kernel(xp_ref, xt_ref, g_ref, b_ref, wa_ref,
                hb_ref, alr_ref, mu_ref, rs_ref, ht_ref, wab_ref):
    i = pl.program_id(0)

    @pl.when(i == 0)
    def _():
        wab_ref[...] = wa_ref[...].astype(bf16)

    def emit(x):
        mu, rs = _ln_stats(x)
        h = (x - mu) * rs * g_ref[...] + b_ref[...]
        hb = h.astype(bf16)
        hb_ref[...] = hb
        alr_ref[...] = _dot(hb, wab_ref[...], _NT)
        _store_stats(mu, rs, mu_ref, rs_ref)
        ht_ref[...] = h

    @pl.when(i < NP // LN_ROWS)
    def _():
        emit(xp_ref[...])

    @pl.when(i >= NP // LN_ROWS)
    def _():
        emit(xt_ref[...])


def _ln0(xp, xt, g, b, w_in_t):
    n_p = NP // LN_ROWS
    t0 = TAIL_ROW0 // LN_ROWS
    return pl.pallas_call(
        _ln0_kernel,
        grid=(R_ALL // LN_ROWS,),
        in_specs=[
            pl.BlockSpec((LN_ROWS, D_MODEL), lambda i: (jnp.minimum(i, n_p - 1), 0)),
            pl.BlockSpec((LN_ROWS, D_MODEL), lambda i: (0, 0)),
            pl.BlockSpec((1, D_MODEL), lambda i: (0, 0)),
            pl.BlockSpec((1, D_MODEL), lambda i: (0, 0)),
            pl.BlockSpec((LANES, D_MODEL), lambda i: (COL_CB // LANES, 0)),
        ],
        out_specs=[
            pl.BlockSpec((LN_ROWS, D_MODEL), lambda i: (i, 0)),
            pl.BlockSpec((LN_ROWS, LANES), lambda i: (i, 0)),
            pl.BlockSpec((LN_ROWS, LANES), lambda i: (i, 0)),
            pl.BlockSpec((LN_ROWS, LANES), lambda i: (i, 0)),
            pl.BlockSpec((LN_ROWS, D_MODEL), lambda i: (jnp.maximum(i - t0, 0), 0)),
        ],
        out_shape=[jax.ShapeDtypeStruct((R_ALL, D_MODEL), bf16),
                   jax.ShapeDtypeStruct((R_ALL, LANES), f32),
                   jax.ShapeDtypeStruct((R_ALL, LANES), f32),
                   jax.ShapeDtypeStruct((R_ALL, LANES), f32),
                   jax.ShapeDtypeStruct((R_ALL - TAIL_ROW0, D_MODEL), f32)],
        scratch_shapes=[pltpu.VMEM((LANES, D_MODEL), bf16)],
        compiler_params=_params(1),
        name="ln0",
    )(xp, xt, g, b, w_in_t)


PROJ_TM, PROJ_TN = 1408, 512


LA_W = 2 * DK


def _proj_kernel(h_ref, w_ref, alr_ref, wa2_ref, ba_ref, p_ref, la_ref):
    j = pl.program_id(1)

    @pl.when(j < N_HEADS)
    def _():
        p_ref[...] = _dot(h_ref[...], w_ref[...].astype(bf16), _NT).astype(bf16)
        la_ref[:, 0:DK], la_ref[:, DK:LA_W] = _split2(
            _gate_log_decay(alr_ref[...], wa2_ref[...], ba_ref[...]))

    is_g = jnp.logical_and(j * PROJ_TN >= COL_G, j * PROJ_TN < COL_CB)

    @pl.when(is_g)
    def _():
        p = _dot(h_ref[...], w_ref[...].astype(bf16), _NT)
        p_ref[...] = (p * jax.nn.sigmoid(p)).astype(bf16)

    @pl.when(jnp.logical_and(j >= N_HEADS, jnp.logical_not(is_g)))
    def _():
        p_ref[...] = _dot(h_ref[...], w_ref[...].astype(bf16), _NT).astype(bf16)


def _proj(hb, w_in_t, alr, wa2p, ba):
    assert COL_G >= N_HEADS * PROJ_TN

    def w_row(j):
        return pl.multiple_of(j * PROJ_TN + jnp.where(j * PROJ_TN >= COL_CB, GATE_RANK, 0), GATE_RANK)

    jh = lambda j: jnp.minimum(j, N_HEADS - 1)
    return pl.pallas_call(
        _proj_kernel,
        grid=(R_ALL // PROJ_TM, D_PROJ // PROJ_TN),
        in_specs=[
            pl.BlockSpec((PROJ_TM, D_MODEL), lambda i, j: (i, 0)),
            pl.BlockSpec((pl.Element(PROJ_TN), pl.Element(D_MODEL)), lambda i, j: (w_row(j), 0)),
            pl.BlockSpec((PROJ_TM, LANES), lambda i, j: (i, 0)),
            pl.BlockSpec((LANES, DK), lambda i, j: (0, jh(j))),
            pl.BlockSpec((1, DK), lambda i, j: (0, jh(j))),
        ],
        out_specs=[
            pl.BlockSpec((PROJ_TM, PROJ_TN), lambda i, j: (i, j)),
            pl.BlockSpec((PROJ_TM, LA_W), lambda i, j: (i, jh(j))),
        ],
        out_shape=[jax.ShapeDtypeStruct((R_ALL, D_PROJ), bf16),
                   jax.ShapeDtypeStruct((R_ALL, N_HEADS * LA_W), bf16)],
        compiler_params=_params(2),
        name="proj",
    )(hb, w_in_t, alr, wa2p, ba)


GLA_BLK = 128
GLA_HPS = 4
GLA_ILV = 2
LVL_DIAG = _log2(GLA_BLK)


def _gla_levels(c):
    return [c >> (t + 1) for t in range(_log2(c))]


def _gla_tables(c):
    idx = np.arange(c)
    tri = (idx[None, :] <= idx[:, None]).astype(np.float32)
    mats = [tri]
    for m in _gla_levels(c):
        ref = ((idx // m) | 1) * m - 1
        mats.append(tri - tri[ref])
    diff = np.concatenate(mats, axis=0)
    diff = np.concatenate([diff, diff], axis=1)
    t = np.arange(min(c, GLA_BLK))
    level = np.full((t.size, t.size), -1, np.int32)
    for m in _gla_levels(t.size):
        same_pair = (t[:, None] // (2 * m)) == (t[None, :] // (2 * m))
        odd_even = (((t[:, None] // m) & 1) == 1) & (((t[None, :] // m) & 1) == 0)
        level[same_pair & odd_even] = _log2(m)
    level[t[:, None] == t[None, :]] = LVL_DIAG
    return jnp.asarray(diff, bf16), jnp.asarray(level)


def _gate_log_decay(alr, wa2, ba):
    return _log_sigmoid(_dot_f32(alr, wa2) + ba) * (1.0 / GATE_TAU)


def _decay_sums(la2, diff_ref, x_ref):
    x_ref[...] = _dot(diff_ref[...], jnp.concatenate([la2[:, 0:DK], la2[:, DK:LA_W]], axis=0))


GLA_STEPS = BATCH * N_CHUNKS
SAMPLE_PER_STEP = DEC_BATCH // GLA_STEPS


def _mixer_kernel(q_ref, k_ref, v_ref, g_ref, la_ref, ng_ref, s0_ref, diff_ref, lvl_ref,
                  qs_ref, ks_ref, vs_ref, gs_ref, las_ref, s0s_ref,
                  cb_ref, cc_ref, ch_ref, mc_ref, mh_ref, cw_ref, buf_ref,
                  og_ref, sfin_ref, ss_ref, yc_ref, nbp_ref, nbs_ref,
                  ogacc_ref, u_ref, *head_scratch):
    step = pl.program_id(0)
    conv_in = (cb_ref, cc_ref, ch_ref, cw_ref)

    @pl.when(step < GLA_STEPS)
    def _():
        _mixer_step(step, q_ref, k_ref, v_ref, g_ref, la_ref, ng_ref, s0_ref, diff_ref, lvl_ref,
                    qs_ref, ks_ref, vs_ref, gs_ref, las_ref, s0s_ref,
                    conv_in, mc_ref, mh_ref, og_ref, sfin_ref, ss_ref, yc_ref, nbp_ref,
                    ogacc_ref, u_ref, *head_scratch)

    @pl.when(step == GLA_STEPS)
    def _():
        og_ref[...] = ogacc_ref[...].astype(bf16)
        _conv_sample_block(*conv_in, buf_ref, yc_ref, nbs_ref)


def _mixer_step(step, q_ref, k_ref, v_ref, g_ref, la_ref, ng_ref, s0_ref, diff_ref, lvl_ref,
                qs_ref, ks_ref, vs_ref, gs_ref, las_ref, s0s_ref,
                conv_in, mc_ref, mh_ref, og_ref, sfin_ref, ss_ref, yc_ref, nbp_ref,
                ogacc_ref, u_ref, *head_scratch):
    s = step % N_CHUNKS
    st_refs, x_refs = head_scratch[:GLA_HPS], head_scratch[GLA_HPS:]

    @pl.when(s == 0)
    def _():
        for hh in range(GLA_HPS):
            st_refs[hh][...] = s0_ref[hh]
        _conv_carry_from_meta(mc_ref, mh_ref, u_ref)

    @pl.when(step == 0)
    def _():
        ogacc_ref[...] = jnp.zeros_like(ogacc_ref)

    conv_last = _conv_prompt_block(*conv_in, yc_ref, u_ref)

    for h0 in range(0, GLA_HPS, GLA_ILV):
        heads = [_gla_prompt_head(hh, q_ref, k_ref, v_ref, g_ref, la_ref, ng_ref,
                                  diff_ref, lvl_ref, og_ref, st_refs[hh], x_refs[hh])
                 for hh in range(h0, h0 + GLA_ILV)]
        while heads:
            heads = [h for h in heads if next(h, _DONE) is not _DONE]

    rows = lax.broadcasted_iota(jnp.int32, (DEC_BATCH, 1), 0)
    for hh in range(N_HEADS):
        vcols = slice(hh * DV, (hh + 1) * DV)
        acc = ogacc_ref[:, vcols]
        for t in range(SAMPLE_PER_STEP):
            r = step * SAMPLE_PER_STEP + t
            og_row = _gla_sample_token(r, t, hh, qs_ref, ks_ref, vs_ref, gs_ref, las_ref, ng_ref,
                                       s0s_ref, ss_ref)
            acc = jnp.where(rows == r, og_row, acc)
        ogacc_ref[:, vcols] = acc

    @pl.when(s == N_CHUNKS - 1)
    def _():
        for hh in range(GLA_HPS):
            sfin_ref[0, hh] = st_refs[hh][...].T
        nbp_ref[0] = conv_last


def _gla_sample_token(r, t, hh, qs_ref, ks_ref, vs_ref, gs_ref, las_ref, ng_ref, s0s_ref, ss_ref):
    kcols = slice(hh * DK, (hh + 1) * DK)
    vcols = slice(hh * DV, (hh + 1) * DV)
    as_col = (lax.broadcasted_iota(jnp.int32, (DEC_BATCH, LANES), 0) == r).astype(bf16)
    as_row = (lax.broadcasted_iota(jnp.int32, (SUBLANES, DEC_BATCH), 1) == r).astype(bf16)
    la_col = (_dot(las_ref[:, hh * LA_W:hh * LA_W + DK], as_col, _TN)
              + _dot(las_ref[:, hh * LA_W + DK:(hh + 1) * LA_W], as_col, _TN))
    a_col = jnp.exp(la_col)
    k_col = _dot(ks_ref[:, kcols], as_col, _TN)
    q_col = _dot(qs_ref[:, kcols], as_col, _TN) * (DK ** -0.5)
    v_row = _dot(as_row, vs_ref[:, vcols])[0:1]
    g_row = _dot(as_row, gs_ref[:, vcols])[0:1]
    wide = lambda c: jnp.concatenate([c] * (DV // LANES), axis=1)
    s_new = wide(a_col) * s0s_ref[t, hh] + wide(k_col) * v_row
    ss_ref[t, hh] = s_new
    o = jnp.sum(wide(q_col) * s_new, axis=0, keepdims=True)
    ms = jnp.mean(o * o, axis=-1, keepdims=True)
    return o * lax.rsqrt(ms + RMS_EPS) * ng_ref[...] * g_row


_DONE = object()


def _gla_prompt_head(hh, q_ref, k_ref, v_ref, g_ref, la_ref, ng_ref,
                     diff_ref, lvl_ref, og_ref, st_ref, x_ref):
    C = CHUNK
    nblk = C // GLA_BLK
    kcols = slice(hh * DK, (hh + 1) * DK)
    vcols = slice(hh * DV, (hh + 1) * DV)

    _decay_sums(la_ref[:, hh * LA_W:(hh + 1) * LA_W], diff_ref, x_ref)
    yield
    kb = k_ref[:, kcols]
    kf = kb.astype(f32)
    vb = v_ref[:, vcols]
    qf = q_ref[:, kcols].astype(f32) * (DK ** -0.5)
    qb = qf.astype(bf16)
    b = x_ref[0:C, :]
    b_last = x_ref[C - 1:C, :]

    lvl = lvl_ref[...]
    blk_rows = lambda a, i: a[i * GLA_BLK:(i + 1) * GLA_BLK]
    tiles = [jnp.where(lvl == LVL_DIAG, _dot(blk_rows(qb, i), blk_rows(kb, i), _NT), 0.0)
             for i in range(nblk)]
    cross = None
    row = lax.broadcasted_iota(jnp.int32, (C, 1), 0)
    for t, m in enumerate(_gla_levels(C)):
        yield
        e = jnp.exp(-jnp.abs(x_ref[(t + 1) * C:(t + 2) * C, :]))
        if m == GLA_BLK:
            qt = (blk_rows(qf, 1) * blk_rows(e, 1)).astype(bf16)
            kt = (blk_rows(kf, 0) * blk_rows(e, 0)).astype(bf16)
            cross = _dot(qt, kt, _NT)
            continue
        if m >= 8:
            n = C // (2 * m)
            e3, q3, k3 = (a.reshape(n, 2 * m, DK) for a in (e, qf, kf))
            zero = jnp.zeros((n, m, DK), f32)
            qt = jnp.concatenate([zero, q3[:, m:, :] * e3[:, m:, :]], axis=1).reshape(C, DK)
            kt = jnp.concatenate([k3[:, :m, :] * e3[:, :m, :], zero], axis=1).reshape(C, DK)
        else:
            odd = ((row >> _log2(m)) & 1) == 1
            qt = jnp.where(odd, qf * e, 0.0)
            kt = jnp.where(odd, 0.0, kf * e)
        qt, kt = qt.astype(bf16), kt.astype(bf16)
        tiles = [jnp.where(lvl == _log2(m), _dot(blk_rows(qt, i), blk_rows(kt, i), _NT), tiles[i])
                 for i in range(nblk)]

    yield
    st = st_ref[...]
    qe = (qf * jnp.exp(b)).astype(bf16)
    outs = []
    for i in range(nblk):
        lhs = tiles[i] if i == 0 else jnp.concatenate([cross, tiles[i]], axis=1)
        outs.append(_dot(lhs.astype(bf16), vb[0:(i + 1) * GLA_BLK]))
    o = _dot(qe, st.astype(bf16), _NT) + jnp.concatenate(outs, axis=0)
    ms = jnp.mean(o * o, axis=-1, keepdims=True)
    on = o * lax.rsqrt(ms + RMS_EPS) * ng_ref[...]
    og_ref[:, vcols] = (on * g_ref[:, vcols].astype(f32)).astype(bf16)

    yield
    kd = (kf * jnp.exp(b_last - b)).astype(bf16)
    st_new = st * jnp.exp(b_last) + _dot(vb, kd, _TN)
    st_ref[...] = st_new


def _mixer(proj, la2, ng, s_meta, s0_sample, conv_w, conv_buf):
    assert CHUNK == GLA_BLK == DEC_BATCH and GLA_HPS == N_HEADS
    assert DEC_BATCH % GLA_STEPS == 0
    diff, level = _gla_tables(CHUNK)
    sps = SAMPLE_PER_STEP
    const = lambda t: (0, 0)
    work = lambda t: jnp.minimum(t, GLA_STEPS - 1)
    srow = NP // DEC_BATCH
    mrow = R_MAIN // N_META
    pcol = lambda col, width: (lambda t: (work(t), col // width))
    scol = lambda col, width: (lambda t: (srow, col // width))
    ccol = lambda col: (lambda t: (t, col // D_CONV))
    return pl.pallas_call(
        _mixer_kernel,
        grid=(GLA_STEPS + 1,),
        in_specs=[
            pl.BlockSpec((CHUNK, D_GLA_K), pcol(COL_Q, D_GLA_K)),
            pl.BlockSpec((CHUNK, D_GLA_K), pcol(COL_K, D_GLA_K)),
            pl.BlockSpec((CHUNK, D_GLA_V), pcol(COL_V, D_GLA_V)),
            pl.BlockSpec((CHUNK, D_GLA_V), pcol(COL_G, D_GLA_V)),
            pl.BlockSpec((CHUNK, N_HEADS * LA_W), lambda t: (work(t), 0)),
            pl.BlockSpec((1, DV), const),
            pl.BlockSpec((N_HEADS, DV, DK), lambda t: (0, 0, 0)),
            pl.BlockSpec(diff.shape, const),
            pl.BlockSpec(level.shape, const),
            pl.BlockSpec((DEC_BATCH, D_GLA_K), scol(COL_Q, D_GLA_K)),
            pl.BlockSpec((DEC_BATCH, D_GLA_K), scol(COL_K, D_GLA_K)),
            pl.BlockSpec((DEC_BATCH, D_GLA_V), scol(COL_V, D_GLA_V)),
            pl.BlockSpec((DEC_BATCH, D_GLA_V), scol(COL_G, D_GLA_V)),
            pl.BlockSpec((DEC_BATCH, N_HEADS * LA_W), lambda t: (srow, 0)),
            pl.BlockSpec((sps, N_HEADS, DK, DV), lambda t: (work(t), 0, 0, 0)),
            pl.BlockSpec((CONV_ROWS, D_CONV), ccol(COL_CB)),
            pl.BlockSpec((CONV_ROWS, D_CONV), ccol(COL_CC)),
            pl.BlockSpec((CONV_ROWS, D_CONV), ccol(COL_CH)),
            pl.BlockSpec((N_META, D_CONV), lambda t: (mrow, COL_CC // D_CONV)),
            pl.BlockSpec((N_META, D_CONV), lambda t: (mrow, COL_CH // D_CONV)),
            pl.BlockSpec((CONV_WIDTH, D_CONV), const),
            pl.BlockSpec((DEC_BATCH, 2 * D_CONV), const),
        ],
        out_specs=[
            pl.BlockSpec((CHUNK, D_GLA_V), lambda t: (t, 0)),
            pl.BlockSpec((1, N_HEADS, DK, DV), lambda t: (work(t) // N_CHUNKS, 0, 0, 0)),
            pl.BlockSpec((sps, N_HEADS, DK, DV), lambda t: (work(t), 0, 0, 0)),
            pl.BlockSpec((CONV_ROWS, D_CONV), lambda t: (t, 0)),
            pl.BlockSpec((1, CONV_WIDTH - 1, D_CONV), lambda t: (work(t) // N_CHUNKS, 0, 0)),
            pl.BlockSpec((DEC_BATCH, 2 * D_CONV), const),
        ],
        out_shape=[jax.ShapeDtypeStruct((R_MAIN, D_GLA_V), bf16),
                   jax.ShapeDtypeStruct((BATCH, N_HEADS, DK, DV), f32),
                   jax.ShapeDtypeStruct((DEC_BATCH, N_HEADS, DK, DV), f32),
                   jax.ShapeDtypeStruct((R_MAIN, D_CONV), bf16),
                   jax.ShapeDtypeStruct((BATCH, CONV_WIDTH - 1, D_CONV), f32),
                   jax.ShapeDtypeStruct((DEC_BATCH, 2 * D_CONV), f32)],
        scratch_shapes=([pltpu.VMEM((DEC_BATCH, D_GLA_V), f32),
                         pltpu.VMEM((CONV_PAD + CONV_ROWS, D_CONV), f32)]
                        + [pltpu.VMEM((DV, DK), f32)] * N_HEADS
                        + [pltpu.VMEM((diff.shape[0], DK), f32)] * N_HEADS),
        compiler_params=_params(1),
        name="mixer",
    )(proj, proj, proj, proj, la2, ng, s_meta, diff, level,
      proj, proj, proj, proj, la2, s0_sample,
      proj, proj, proj, proj, proj, conv_w, conv_buf)


def _gla_meta_kernel(k_ref, v_ref, la_ref, diff_ref, s_ref, x_ref):
    _decay_sums(la_ref[...], diff_ref, x_ref)
    kd = (k_ref[...].astype(f32) * jnp.exp(x_ref[N_META - 1:N_META, :] - x_ref[...])).astype(bf16)
    s_ref[0] = _dot(v_ref[...], kd, _TN)


def _gla_meta(proj, la2):
    diff = _gla_tables(N_META)[0][:N_META]
    mb = R_MAIN // N_META
    return pl.pallas_call(
        _gla_meta_kernel,
        grid=(N_HEADS,),
        in_specs=[
            pl.BlockSpec((N_META, DK), lambda h: (mb, COL_K // DK + h)),
            pl.BlockSpec((N_META, DV), lambda h: (mb, COL_V // DV + h)),
            pl.BlockSpec((N_META, LA_W), lambda h: (mb, h)),
            pl.BlockSpec(diff.shape, lambda h: (0, 0)),
        ],
        out_specs=pl.BlockSpec((1, DV, DK), lambda h: (h, 0, 0)),
        out_shape=jax.ShapeDtypeStruct((N_HEADS, DV, DK), f32),
        scratch_shapes=[pltpu.VMEM((N_META, DK), f32)],
        compiler_params=_params(1),
        name="gla_meta",
    )(proj, proj, la2, diff)


CONV_ROWS = CHUNK
CONV_PAD = SUBLANES


def _conv_carry_from_meta(mc_ref, mh_ref, u_ref):
    mu = mc_ref[...].astype(f32) * mh_ref[...].astype(f32)
    u_ref[CONV_PAD - 2:CONV_PAD, :] = mu[N_META - 2:N_META, :]


def _conv_prompt_block(cb_ref, cc_ref, ch_ref, w_ref, y_ref, u_ref):
    TR = CONV_ROWS
    u = cc_ref[...].astype(f32) * ch_ref[...].astype(f32)
    w = w_ref[...]
    u_ref[CONV_PAD:CONV_PAD + TR, :] = u
    zc = (w[0:1, :] * u_ref[CONV_PAD - 2:CONV_PAD - 2 + TR, :]
          + w[1:2, :] * u_ref[CONV_PAD - 1:CONV_PAD - 1 + TR, :]
          + w[2:3, :] * u)
    y_ref[...] = (cb_ref[...].astype(f32) * zc).astype(bf16)
    last = u[TR - 2:TR, :]
    u_ref[CONV_PAD - 2:CONV_PAD, :] = last
    return last


def _conv_sample_block(cb_ref, cc_ref, ch_ref, w_ref, buf_ref, y_ref, nbs_ref):
    u = cc_ref[...].astype(f32) * ch_ref[...].astype(f32)
    w = w_ref[...]
    b0 = buf_ref[:, 0:D_CONV]
    b1 = buf_ref[:, D_CONV:2 * D_CONV]
    zc = w[0:1, :] * b0 + w[1:2, :] * b1 + w[2:3, :] * u
    y_ref[...] = (cb_ref[...].astype(f32) * zc).astype(bf16)
    nbs_ref[:, 0:D_CONV] = b1
    nbs_ref[:, D_CONV:2 * D_CONV] = u


OUT_FULL = TAIL_ROW0 // OUT_TM


def _outproj_kernel(o_ref, y_ref, wo_ref, wy_ref, x_ref, mu_ref, rs_ref, g_ref, b_ref, ht_ref,
                    s_ref):
    i = pl.program_id(0)

    def mix():
        return (_dot(o_ref[...], wo_ref[...].astype(bf16))
                + _dot(y_ref[...], wy_ref[...].astype(bf16)))

    @pl.when(i < OUT_FULL)
    def _():
        s_ref[...] = _ln_from_stats(x_ref[...], mu_ref, rs_ref, g_ref[...], b_ref[...]) + mix()

    @pl.when(i >= OUT_FULL)
    def _():
        s_ref[...] = ALPHA * ht_ref[...] + mix()


def _outproj(og, yc, w_out, xp, mu, rs, g, b, h_tail):
    full = lambda i: i < OUT_FULL
    return pl.pallas_call(
        _outproj_kernel,
        grid=(R_MAIN // OUT_TM, D_MODEL // OUT_TN),
        in_specs=[
            pl.BlockSpec((OUT_TM, D_GLA_V), lambda i, j: (i, 0)),
            pl.BlockSpec((OUT_TM, D_CONV), lambda i, j: (i, 0)),
            pl.BlockSpec((D_GLA_V, OUT_TN), lambda i, j: (0, j)),
            pl.BlockSpec((D_CONV, OUT_TN), lambda i, j: (1, j)),
            pl.BlockSpec((OUT_TM, OUT_TN),
                         lambda i, j: (jnp.minimum(i, OUT_FULL - 1), jnp.where(full(i), j, 0))),
            pl.BlockSpec((OUT_TM, LANES), lambda i, j: (i, 0)),
            pl.BlockSpec((OUT_TM, LANES), lambda i, j: (i, 0)),
            pl.BlockSpec((1, OUT_TN), lambda i, j: (0, j)),
            pl.BlockSpec((1, OUT_TN), lambda i, j: (0, j)),
            pl.BlockSpec((OUT_TM, OUT_TN), lambda i, j: (0, jnp.where(full(i), 0, j))),
        ],
        out_specs=pl.BlockSpec((OUT_TM, OUT_TN), lambda i, j: (i, j)),
        out_shape=jax.ShapeDtypeStruct((R_MAIN, D_MODEL), f32),
        compiler_params=_params(2),
        name="outproj",
    )(og, yc, w_out, w_out, xp, mu, rs, g, b, h_tail)


LN1_ROWS = 320


def _ln1_kernel(s_ref, g_ref, b_ref, hb_ref, mu_ref, rs_ref):
    x = s_ref[...]
    mu, rs = _ln_stats(x)
    hb_ref[...] = ((x - mu) * rs * g_ref[...] + b_ref[...]).astype(bf16)
    _store_stats(mu, rs, mu_ref, rs_ref)


def _ln1(s1, g, b):
    return pl.pallas_call(
        _ln1_kernel,
        grid=(R_MAIN // LN1_ROWS,),
        in_specs=[
            pl.BlockSpec((LN1_ROWS, D_MODEL), lambda i: (i, 0)),
            pl.BlockSpec((1, D_MODEL), lambda i: (0, 0)),
            pl.BlockSpec((1, D_MODEL), lambda i: (0, 0)),
        ],
        out_specs=[
            pl.BlockSpec((LN1_ROWS, D_MODEL), lambda i: (i, 0)),
            pl.BlockSpec((LN1_ROWS, LANES), lambda i: (i, 0)),
            pl.BlockSpec((LN1_ROWS, LANES), lambda i: (i, 0)),
        ],
        out_shape=[jax.ShapeDtypeStruct((R_MAIN, D_MODEL), bf16),
                   jax.ShapeDtypeStruct((R_MAIN, LANES), f32),
                   jax.ShapeDtypeStruct((R_MAIN, LANES), f32)],
        compiler_params=_params(1),
        name="ln1",
    )(s1, g, b)


LN2_ROWS = 512


def _ln2_kernel(sp_ref, ss_ref, g_ref, b_ref, yp_ref, ys_ref):
    yp_ref[...] = _layer_norm(sp_ref[...], g_ref[...], b_ref[...])

    @pl.when(pl.program_id(0) == 0)
    def _():
        ys_ref[...] = _layer_norm(ss_ref[...], g_ref[...], b_ref[...])


def _ln2(s2, g, b):
    return pl.pallas_call(
        _ln2_kernel,
        grid=(NP // LN2_ROWS,),
        in_specs=[
            pl.BlockSpec((LN2_ROWS, D_MODEL), lambda i: (i, 0)),
            pl.BlockSpec((DEC_BATCH, D_MODEL), lambda i: (NP // DEC_BATCH, 0)),
            pl.BlockSpec((1, D_MODEL), lambda i: (0, 0)),
            pl.BlockSpec((1, D_MODEL), lambda i: (0, 0)),
        ],
        out_specs=[
            pl.BlockSpec((LN2_ROWS, D_MODEL), lambda i: (i, 0)),
            pl.BlockSpec((DEC_BATCH, D_MODEL), lambda i: (0, 0)),
        ],
        out_shape=[jax.ShapeDtypeStruct((NP, D_MODEL), f32),
                   jax.ShapeDtypeStruct((DEC_BATCH, D_MODEL), f32)],
        compiler_params=_params(1),
        name="ln2",
    )(s2, s2, g, b)


FFN_TM, FFN_TN = 2080, 256


FFN_NT = D_FF // FFN_TN


def _ffn_up_kernel(h_ref, wg_ref, wu_ref, wd_ref, act_ref, wdb_ref):
    i = pl.program_id(0)
    j = pl.program_id(1)

    @pl.when(j < FFN_NT)
    def _():
        h = h_ref[...]
        a = _dot(h, wg_ref[...].astype(bf16))
        u = _dot(h, wu_ref[...].astype(bf16))
        act_ref[...] = (a * jax.nn.sigmoid(a) * u).astype(bf16)

    @pl.when(j >= FFN_NT)
    def _():
        act_ref[...] = jnp.zeros_like(act_ref)

    @pl.when(jnp.logical_and(i == 0, j < FFN_NT))
    def _():
        wdb_ref[...] = wd_ref[...].astype(bf16)

    @pl.when(jnp.logical_and(i == 0, j >= FFN_NT))
    def _():
        wdb_ref[...] = jnp.zeros_like(wdb_ref)


def _ffn_up(hb, wg, wu, wd):
    nt_pad = D_FF_PAD // FFN_TN
    jw = lambda j: jnp.minimum(j, FFN_NT - 1)
    return pl.pallas_call(
        _ffn_up_kernel,
        grid=(R_MAIN // FFN_TM, nt_pad),
        in_specs=[
            pl.BlockSpec((FFN_TM, D_MODEL), lambda i, j: (i, 0), pipeline_mode=pl.Buffered(1)),
            pl.BlockSpec((D_MODEL, FFN_TN), lambda i, j: (0, jw(j))),
            pl.BlockSpec((D_MODEL, FFN_TN), lambda i, j: (0, jw(j))),
            pl.BlockSpec((FFN_TN, D_MODEL), lambda i, j: (jnp.where(i == 0, jw(j), FFN_NT - 1), 0)),
        ],
        out_specs=[
            pl.BlockSpec((FFN_TM, FFN_TN), lambda i, j: (i, j)),
            pl.BlockSpec((FFN_TN, D_MODEL), lambda i, j: (jnp.where(i == 0, j, nt_pad - 1), 0)),
        ],
        out_shape=[jax.ShapeDtypeStruct((R_MAIN, D_FF_PAD), bf16),
                   jax.ShapeDtypeStruct((D_FF_PAD, D_MODEL), bf16)],
        compiler_params=_params(2),
        name="ffn_up",
    )(hb, wg, wu, wd)


DOWN_TM, DOWN_TN, DOWN_TK = 2080, 1024, 1024


def _ffn_down_kernel(a_ref, w_ref, s1_ref, mu_ref, rs_ref, g_ref, b_ref, s_ref):
    @pl.when(pl.program_id(2) == 0)
    def _():
        s_ref[...] = (_ln_from_stats(s1_ref[...], mu_ref, rs_ref, g_ref[...], b_ref[...])
                      + _dot(a_ref[...], w_ref[...]))

    @pl.when(pl.program_id(2) > 0)
    def _():
        s_ref[...] += _dot(a_ref[...], w_ref[...])


def _ffn_down(act, wd, s1, mu, rs, g, b):
    return pl.pallas_call(
        _ffn_down_kernel,
        grid=(R_MAIN // DOWN_TM, D_MODEL // DOWN_TN, D_FF_PAD // DOWN_TK),
        in_specs=[
            pl.BlockSpec((DOWN_TM, DOWN_TK), lambda i, j, k: (i, k)),
            pl.BlockSpec((DOWN_TK, DOWN_TN), lambda i, j, k: (k, j)),
            pl.BlockSpec((DOWN_TM, DOWN_TN), lambda i, j, k: (i, j)),
            pl.BlockSpec((DOWN_TM, LANES), lambda i, j, k: (i, 0)),
            pl.BlockSpec((DOWN_TM, LANES), lambda i, j, k: (i, 0)),
            pl.BlockSpec((1, DOWN_TN), lambda i, j, k: (0, j)),
            pl.BlockSpec((1, DOWN_TN), lambda i, j, k: (0, j)),
        ],
        out_specs=pl.BlockSpec((DOWN_TM, DOWN_TN), lambda i, j, k: (i, j)),
        out_shape=jax.ShapeDtypeStruct((R_MAIN, D_MODEL), f32),
        compiler_params=_params(3),
        name="ffn_down",
    )(act, wd, s1, mu, rs, g, b)


def kernel(x_prompt, x_sample, state_gla, state_conv, meta_tokens, emb_ln_g, emb_ln_b,
           w_in, w_a2, b_a, gla_norm_g, conv_w, w_out, ln1_g, ln1_b,
           w_ffn_gate, w_ffn_up, w_ffn_down, ln2_g, ln2_b):
    assert x_prompt.shape == (BATCH, SEQ, D_MODEL) and x_sample.shape == (DEC_BATCH, 1, D_MODEL)
    assert w_in.shape[0] == 1, "single layer"
    row = lambda v: v.reshape(1, -1)

    w_in_t = jnp.transpose(w_in[0])
    wa2p = jnp.pad(w_a2[0], ((0, LANES - GATE_RANK), (0, 0)))

    xp = x_prompt.reshape(NP, D_MODEL)
    xt = jnp.concatenate([x_sample.reshape(DEC_BATCH, D_MODEL), meta_tokens.astype(f32),
                          jnp.zeros((LN_ROWS - DEC_BATCH - N_META, D_MODEL), f32)], axis=0)

    g0, b0 = row(emb_ln_g), row(emb_ln_b)
    g1, b1 = row(ln1_g[0]), row(ln1_b[0])
    hb, alr, mu0, rs0, h_tail = _ln0(xp, xt, g0, b0, w_in_t)
    proj, la2 = _proj(hb, w_in_t, alr, wa2p, row(b_a[0]))

    ng = row(gla_norm_g[0])
    s_meta = _gla_meta(proj, la2)
    og, s_p, s_s, yc, nb_p, nb_s = _mixer(proj, la2, ng, s_meta, state_gla[0], conv_w[0],
                                          state_conv[0].reshape(DEC_BATCH, 2 * D_CONV))

    s1 = _outproj(og, yc, w_out[0], xp, mu0, rs0, ALPHA * g0, ALPHA * b0, h_tail)
    h1b, mu1, rs1 = _ln1(s1, g1, b1)
    act, wd = _ffn_up(h1b, w_ffn_gate[0], w_ffn_up[0], w_ffn_down[0])
    s2 = _ffn_down(act, wd, s1, mu1, rs1, ALPHA * g1, ALPHA * b1)
    y_p, y_s = _ln2(s2, row(ln2_g[0]), row(ln2_b[0]))

    return (y_p.reshape(BATCH, SEQ, D_MODEL),
            y_s.reshape(DEC_BATCH, 1, D_MODEL),
            s_p[None],
            nb_p[None],
            s_s[None],
            nb_s.reshape(1, DEC_BATCH, CONV_WIDTH - 1, D_CONV))
```

```python
import jax
import jax.numpy as jnp
import numpy as np
from jax import lax
from jax.experimental import pallas as pl
from jax.experimental.pallas import tpu as pltpu

f32 = jnp.float32
bf16 = jnp.bfloat16

D_MODEL = 4096
BATCH = 4
SEQ = 2048
DEC_BATCH = 128
N_META = 16
D_GLA_V = D_MODEL // 2
D_CONV = D_MODEL - D_GLA_V
D_GLA_K = D_GLA_V // 2
N_HEADS = 4
DK = D_GLA_K // N_HEADS
DV = D_GLA_V // N_HEADS
GATE_RANK = 16
GATE_TAU = 16.0
CONV_WIDTH = 3
D_FF = 11008
ALPHA = 2.0 ** 0.25
LN_EPS = 1e-5
RMS_EPS = 1e-6

LANES = 128
SUBLANES = 8
NP = BATCH * SEQ
R_MAIN = NP + DEC_BATCH
R_ALL = R_MAIN + LANES
D_PROJ = 2 * D_GLA_K + 2 * D_GLA_V + 3 * D_CONV
D_FF_PAD = 11264
CHUNK = 128
N_CHUNKS = SEQ // CHUNK
VMEM_LIMIT = 58 * 1024 * 1024

COL_Q, COL_K, COL_V, COL_G = 0, D_GLA_K, 2 * D_GLA_K, 2 * D_GLA_K + D_GLA_V
COL_CB = COL_G + D_GLA_V
COL_CC = COL_CB + D_CONV
COL_CH = COL_CC + D_CONV


def _params(n_axes):
    return pltpu.CompilerParams(
        dimension_semantics=("arbitrary",) * n_axes, vmem_limit_bytes=VMEM_LIMIT)


def _ln_stats(x):
    mu = jnp.mean(x, axis=-1, keepdims=True)
    xc = x - mu
    var = jnp.mean(xc * xc, axis=-1, keepdims=True)
    return mu, lax.rsqrt(var + LN_EPS)


def _layer_norm(x, g, b):
    mu, rs = _ln_stats(x)
    return (x - mu) * rs * g + b


def _ln_from_stats(x, mu_ref, rs_ref, g, b):
    reps = x.shape[-1] // LANES
    mu = jnp.concatenate([mu_ref[...]] * reps, axis=1)
    rs = jnp.concatenate([rs_ref[...]] * reps, axis=1)
    return (x - mu) * rs * g + b


def _store_stats(mu, rs, mu_ref, rs_ref):
    mu_ref[...] = jnp.broadcast_to(mu, mu_ref.shape)
    rs_ref[...] = jnp.broadcast_to(rs, rs_ref.shape)


def _split2(x):
    hi = x.astype(bf16)
    return hi, (x - hi.astype(f32)).astype(bf16)


def _dot(a, b, dims=(((1,), (0,)), ((), ()))):
    return lax.dot_general(a, b, dims, preferred_element_type=f32)


_NT = (((1,), (1,)), ((), ()))
_TN = (((0,), (0,)), ((), ()))


def _dot_f32(a, b):
    ah, am = _split2(a)
    bh, bm = _split2(b)
    return _dot(ah, bh) + _dot(ah, bm) + _dot(am, bh)


def _log2(n):
    assert n & (n - 1) == 0
    return n.bit_length() - 1


def _log_sigmoid(z):
    return jnp.minimum(z, 0.0) - jnp.log(1.0 + jnp.exp(-jnp.abs(z)))


LN_ROWS = 256


OUT_TM, OUT_TN = 1664, 256
TAIL_ROW0 = (R_MAIN // OUT_TM - 1) * OUT_TM
assert TAIL_ROW0 <= NP and TAIL_ROW0 % LN_ROWS == 0


def _ln0_kernel(xp_ref, xt_ref, g_ref, b_ref, wa_ref,
                hb_ref, alr_ref, mu_ref, rs_ref, ht_ref, wab_ref):
    i = pl.program_id(0)

    @pl.when(i == 0)
    def _():
        wab_ref[...] = wa_ref[...].astype(bf16)

    def emit(x):
        mu, rs = _ln_stats(x)
        h = (x - mu) * rs * g_ref[...] + b_ref[...]
        hb = h.astype(bf16)
        hb_ref[...] = hb
        alr_ref[...] = _dot(hb, wab_ref[...], _NT)
        _store_stats(mu, rs, mu_ref, rs_ref)
        ht_ref[...] = h

    @pl.when(i < NP // LN_ROWS)
    def _():
        emit(xp_ref[...])

    @pl.when(i >= NP // LN_ROWS)
    def _():
        emit(xt_ref[...])


def _ln0(xp, xt, g, b, w_in_t):
    n_p = NP // LN_ROWS
    t0 = TAIL_ROW0 // LN_ROWS
    return pl.pallas_call(
        _ln0_kernel,
        grid=(R_ALL // LN_ROWS,),
        in_specs=[
            pl.BlockSpec((LN_ROWS, D_MODEL), lambda i: (jnp.minimum(i, n_p - 1), 0)),
            pl.BlockSpec((LN_ROWS, D_MODEL), lambda i: (0, 0)),
            pl.BlockSpec((1, D_MODEL), lambda i: (0, 0)),
            pl.BlockSpec((1, D_MODEL), lambda i: (0, 0)),
            pl.BlockSpec((LANES, D_MODEL), lambda i: (COL_CB // LANES, 0)),
        ],
        out_specs=[
            pl.BlockSpec((LN_ROWS, D_MODEL), lambda i: (i, 0)),
            pl.BlockSpec((LN_ROWS, LANES), lambda i: (i, 0)),
            pl.BlockSpec((LN_ROWS, LANES), lambda i: (i, 0)),
            pl.BlockSpec((LN_ROWS, LANES), lambda i: (i, 0)),
            pl.BlockSpec((LN_ROWS, D_MODEL), lambda i: (jnp.maximum(i - t0, 0), 0)),
        ],
        out_shape=[jax.ShapeDtypeStruct((R_ALL, D_MODEL), bf16),
                   jax.ShapeDtypeStruct((R_ALL, LANES), f32),
                   jax.ShapeDtypeStruct((R_ALL, LANES), f32),
                   jax.ShapeDtypeStruct((R_ALL, LANES), f32),
                   jax.ShapeDtypeStruct((R_ALL - TAIL_ROW0, D_MODEL), f32)],
        scratch_shapes=[pltpu.VMEM((LANES, D_MODEL), bf16)],
        compiler_params=_params(1),
        name="ln0",
    )(xp, xt, g, b, w_in_t)


PROJ_TM, PROJ_TN = 1408, 512


LA_W = 2 * DK


def _proj_kernel(h_ref, w_ref, alr_ref, wa2_ref, ba_ref, p_ref, la_ref):
    j = pl.program_id(1)

    @pl.when(j < N_HEADS)
    def _():
        p_ref[...] = _dot(h_ref[...], w_ref[...].astype(bf16), _NT).astype(bf16)
        la_ref[:, 0:DK], la_ref[:, DK:LA_W] = _split2(
            _gate_log_decay(alr_ref[...], wa2_ref[...], ba_ref[...]))

    is_g = jnp.logical_and(j * PROJ_TN >= COL_G, j * PROJ_TN < COL_CB)

    @pl.when(is_g)
    def _():
        p = _dot(h_ref[...], w_ref[...].astype(bf16), _NT)
        p_ref[...] = (p * jax.nn.sigmoid(p)).astype(bf16)

    @pl.when(jnp.logical_and(j >= N_HEADS, jnp.logical_not(is_g)))
    def _():
        p_ref[...] = _dot(h_ref[...], w_ref[...].astype(bf16), _NT).astype(bf16)


def _proj(hb, w_in_t, alr, wa2p, ba):
    assert COL_G >= N_HEADS * PROJ_TN

    def w_row(j):
        return pl.multiple_of(j * PROJ_TN + jnp.where(j * PROJ_TN >= COL_CB, GATE_RANK, 0), GATE_RANK)

    jh = lambda j: jnp.minimum(j, N_HEADS - 1)
    return pl.pallas_call(
        _proj_kernel,
        grid=(R_ALL // PROJ_TM, D_PROJ // PROJ_TN),
        in_specs=[
            pl.BlockSpec((PROJ_TM, D_MODEL), lambda i, j: (i, 0)),
            pl.BlockSpec((pl.Element(PROJ_TN), pl.Element(D_MODEL)), lambda i, j: (w_row(j), 0)),
            pl.BlockSpec((PROJ_TM, LANES), lambda i, j: (i, 0)),
            pl.BlockSpec((LANES, DK), lambda i, j: (0, jh(j))),
            pl.BlockSpec((1, DK), lambda i, j: (0, jh(j))),
        ],
        out_specs=[
            pl.BlockSpec((PROJ_TM, PROJ_TN), lambda i, j: (i, j)),
            pl.BlockSpec((PROJ_TM, LA_W), lambda i, j: (i, jh(j))),
        ],
        out_shape=[jax.ShapeDtypeStruct((R_ALL, D_PROJ), bf16),
                   jax.ShapeDtypeStruct((R_ALL, N_HEADS * LA_W), bf16)],
        compiler_params=_params(2),
        name="proj",
    )(hb, w_in_t, alr, wa2p, ba)


GLA_BLK = 128
GLA_HPS = 4
GLA_ILV = 2
LVL_DIAG = _log2(GLA_BLK)


def _gla_levels(c):
    return [c >> (t + 1) for t in range(_log2(c))]


def _gla_tables(c):
    idx = np.arange(c)
    tri = (idx[None, :] <= idx[:, None]).astype(np.float32)
    mats = [tri]
    for m in _gla_levels(c):
        ref = ((idx // m) | 1) * m - 1
        mats.append(tri - tri[ref])
    diff = np.concatenate(mats, axis=0)
    diff = np.concatenate([diff, diff], axis=1)
    t = np.arange(min(c, GLA_BLK))
    level = np.full((t.size, t.size), -1, np.int32)
    for m in _gla_levels(t.size):
        same_pair = (t[:, None] // (2 * m)) == (t[None, :] // (2 * m))
        odd_even = (((t[:, None] // m) & 1) == 1) & (((t[None, :] // m) & 1) == 0)
        level[same_pair & odd_even] = _log2(m)
    level[t[:, None] == t[None, :]] = LVL_DIAG
    return jnp.asarray(diff, bf16), jnp.asarray(level)


def _gate_log_decay(alr, wa2, ba):
    return _log_sigmoid(_dot_f32(alr, wa2) + ba) * (1.0 / GATE_TAU)


def _decay_sums(la2, diff_ref, x_ref):
    x_ref[...] = _dot(diff_ref[...], jnp.concatenate([la2[:, 0:DK], la2[:, DK:LA_W]], axis=0))


GLA_STEPS = BATCH * N_CHUNKS
SAMPLE_PER_STEP = DEC_BATCH // GLA_STEPS


def _gla_kernel(q_ref, k_ref, v_ref, g_ref, la_ref, ng_ref, s0_ref, diff_ref, lvl_ref,
                qs_ref, ks_ref, vs_ref, gs_ref, las_ref, s0s_ref,
                og_ref, sfin_ref, ss_ref, ogacc_ref, *head_scratch):
    step = pl.program_id(0)

    @pl.when(step < GLA_STEPS)
    def _():
        _gla_step(step, q_ref, k_ref, v_ref, g_ref, la_ref, ng_ref, s0_ref, diff_ref, lvl_ref,
                  qs_ref, ks_ref, vs_ref, gs_ref, las_ref, s0s_ref,
                  og_ref, sfin_ref, ss_ref, ogacc_ref, *head_scratch)

    @pl.when(step == GLA_STEPS)
    def _():
        og_ref[...] = ogacc_ref[...].astype(bf16)


def _gla_step(step, q_ref, k_ref, v_ref, g_ref, la_ref, ng_ref, s0_ref, diff_ref, lvl_ref,
              qs_ref, ks_ref, vs_ref, gs_ref, las_ref, s0s_ref,
              og_ref, sfin_ref, ss_ref, ogacc_ref, *head_scratch):
    s = step % N_CHUNKS
    st_refs, x_refs = head_scratch[:GLA_HPS], head_scratch[GLA_HPS:]

    @pl.when(s == 0)
    def _():
        for hh in range(GLA_HPS):
            st_refs[hh][...] = s0_ref[hh]

    @pl.when(step == 0)
    def _():
        ogacc_ref[...] = jnp.zeros_like(ogacc_ref)

    for h0 in range(0, GLA_HPS, GLA_ILV):
        heads = [_gla_prompt_head(hh, q_ref, k_ref, v_ref, g_ref, la_ref, ng_ref,
                                  diff_ref, lvl_ref, og_ref, st_refs[hh], x_refs[hh])
                 for hh in range(h0, h0 + GLA_ILV)]
        while heads:
            heads = [h for h in heads if next(h, _DONE) is not _DONE]

    rows = lax.broadcasted_iota(jnp.int32, (DEC_BATCH, 1), 0)
    for hh in range(N_HEADS):
        vcols = slice(hh * DV, (hh + 1) * DV)
        acc = ogacc_ref[:, vcols]
        for t in range(SAMPLE_PER_STEP):
            r = step * SAMPLE_PER_STEP + t
            og_row = _gla_sample_token(r, t, hh, qs_ref, ks_ref, vs_ref, gs_ref, las_ref, ng_ref,
                                       s0s_ref, ss_ref)
            acc = jnp.where(rows == r, og_row, acc)
        ogacc_ref[:, vcols] = acc

    @pl.when(s == N_CHUNKS - 1)
    def _():
        for hh in range(GLA_HPS):
            sfin_ref[0, hh] = st_refs[hh][...].T


def _gla_sample_token(r, t, hh, qs_ref, ks_ref, vs_ref, gs_ref, las_ref, ng_ref, s0s_ref, ss_ref):
    kcols = slice(hh * DK, (hh + 1) * DK)
    vcols = slice(hh * DV, (hh + 1) * DV)
    as_col = (lax.broadcasted_iota(jnp.int32, (DEC_BATCH, LANES), 0) == r).astype(bf16)
    as_row = (lax.broadcasted_iota(jnp.int32, (SUBLANES, DEC_BATCH), 1) == r).astype(bf16)
    la_col = (_dot(las_ref[:, hh * LA_W:hh * LA_W + DK], as_col, _TN)
              + _dot(las_ref[:, hh * LA_W + DK:(hh + 1) * LA_W], as_col, _TN))
    a_col = jnp.exp(la_col)
    k_col = _dot(ks_ref[:, kcols], as_col, _TN)
    q_col = _dot(qs_ref[:, kcols], as_col, _TN) * (DK ** -0.5)
    v_row = _dot(as_row, vs_ref[:, vcols])[0:1]
    g_row = _dot(as_row, gs_ref[:, vcols])[0:1]
    wide = lambda c: jnp.concatenate([c] * (DV // LANES), axis=1)
    s_new = wide(a_col) * s0s_ref[t, hh] + wide(k_col) * v_row
    ss_ref[t, hh] = s_new
    o = jnp.sum(wide(q_col) * s_new, axis=0, keepdims=True)
    ms = jnp.mean(o * o, axis=-1, keepdims=True)
    return o * lax.rsqrt(ms + RMS_EPS) * ng_ref[...] * g_row


_DONE = object()


def _gla_prompt_head(hh, q_ref, k_ref, v_ref, g_ref, la_ref, ng_ref,
                     diff_ref, lvl_ref, og_ref, st_ref, x_ref):
    C = CHUNK
    nblk = C // GLA_BLK
    kcols = slice(hh * DK, (hh + 1) * DK)
    vcols = slice(hh * DV, (hh + 1) * DV)

    _decay_sums(la_ref[:, hh * LA_W:(hh + 1) * LA_W], diff_ref, x_ref)
    yield
    kb = k_ref[:, kcols]
    kf = kb.astype(f32)
    vb = v_ref[:, vcols]
    qf = q_ref[:, kcols].astype(f32) * (DK ** -0.5)
    qb = qf.astype(bf16)
    b = x_ref[0:C, :]
    b_last = x_ref[C - 1:C, :]

    lvl = lvl_ref[...]
    blk_rows = lambda a, i: a[i * GLA_BLK:(i + 1) * GLA_BLK]
    tiles = [jnp.where(lvl == LVL_DIAG, _dot(blk_rows(qb, i), blk_rows(kb, i), _NT), 0.0)
             for i in range(nblk)]
    cross = None
    row = lax.broadcasted_iota(jnp.int32, (C, 1), 0)
    for t, m in enumerate(_gla_levels(C)):
        yield
        e = jnp.exp(-jnp.abs(x_ref[(t + 1) * C:(t + 2) * C, :]))
        if m == GLA_BLK:
            qt = (blk_rows(qf, 1) * blk_rows(e, 1)).astype(bf16)
            kt = (blk_rows(kf, 0) * blk_rows(e, 0)).astype(bf16)
            cross = _dot(qt, kt, _NT)
            continue
        if m >= 8:
            n = C // (2 * m)
            e3, q3, k3 = (a.reshape(n, 2 * m, DK) for a in (e, qf, kf))
            zero = jnp.zeros((n, m, DK), f32)
            qt = jnp.concatenate([zero, q3[:, m:, :] * e3[:, m:, :]], axis=1).reshape(C, DK)
            kt = jnp.concatenate([k3[:, :m, :] * e3[:, :m, :], zero], axis=1).reshape(C, DK)
        else:
            odd = ((row >> _log2(m)) & 1) == 1
            qt = jnp.where(odd, qf * e, 0.0)
            kt = jnp.where(odd, 0.0, kf * e)
        qt, kt = qt.astype(bf16), kt.astype(bf16)
        tiles = [jnp.where(lvl == _log2(m), _dot(blk_rows(qt, i), blk_rows(kt, i), _NT), tiles[i])
                 for i in range(nblk)]

    yield
    st = st_ref[...]
    qe = (qf * jnp.exp(b)).astype(bf16)
    outs = []
    for i in range(nblk):
        lhs = tiles[i] if i == 0 else jnp.concatenate([cross, tiles[i]], axis=1)
        outs.append(_dot(lhs.astype(bf16), vb[0:(i + 1) * GLA_BLK]))
    o = _dot(qe, st.astype(bf16), _NT) + jnp.concatenate(outs, axis=0)
    ms = jnp.mean(o * o, axis=-1, keepdims=True)
    on = o * lax.rsqrt(ms + RMS_EPS) * ng_ref[...]
    og_ref[:, vcols] = (on * g_ref[:, vcols].astype(f32)).astype(bf16)

    yield
    kd = (kf * jnp.exp(b_last - b)).astype(bf16)
    st_new = st * jnp.exp(b_last) + _dot(vb, kd, _TN)
    st_ref[...] = st_new


def _gla(proj, la2, ng, s_meta, s0_sample):
    assert CHUNK == GLA_BLK == DEC_BATCH and GLA_HPS == N_HEADS
    assert DEC_BATCH % GLA_STEPS == 0
    diff, level = _gla_tables(CHUNK)
    sps = SAMPLE_PER_STEP
    const = lambda t: (0, 0)
    work = lambda t: jnp.minimum(t, GLA_STEPS - 1)
    srow = NP // DEC_BATCH
    pcol = lambda col, width: (lambda t: (work(t), col // width))
    scol = lambda col, width: (lambda t: (srow, col // width))
    return pl.pallas_call(
        _gla_kernel,
        grid=(GLA_STEPS + 1,),
        in_specs=[
            pl.BlockSpec((CHUNK, D_GLA_K), pcol(COL_Q, D_GLA_K)),
            pl.BlockSpec((CHUNK, D_GLA_K), pcol(COL_K, D_GLA_K)),
            pl.BlockSpec((CHUNK, D_GLA_V), pcol(COL_V, D_GLA_V)),
            pl.BlockSpec((CHUNK, D_GLA_V), pcol(COL_G, D_GLA_V)),
            pl.BlockSpec((CHUNK, N_HEADS * LA_W), lambda t: (work(t), 0)),
            pl.BlockSpec((1, DV), const),
            pl.BlockSpec((N_HEADS, DV, DK), lambda t: (0, 0, 0)),
            pl.BlockSpec(diff.shape, const),
            pl.BlockSpec(level.shape, const),
            pl.BlockSpec((DEC_BATCH, D_GLA_K), scol(COL_Q, D_GLA_K)),
            pl.BlockSpec((DEC_BATCH, D_GLA_K), scol(COL_K, D_GLA_K)),
            pl.BlockSpec((DEC_BATCH, D_GLA_V), scol(COL_V, D_GLA_V)),
            pl.BlockSpec((DEC_BATCH, D_GLA_V), scol(COL_G, D_GLA_V)),
            pl.BlockSpec((DEC_BATCH, N_HEADS * LA_W), lambda t: (srow, 0)),
            pl.BlockSpec((sps, N_HEADS, DK, DV), lambda t: (work(t), 0, 0, 0)),
        ],
        out_specs=[
            pl.BlockSpec((CHUNK, D_GLA_V), lambda t: (t, 0)),
            pl.BlockSpec((1, N_HEADS, DK, DV), lambda t: (work(t) // N_CHUNKS, 0, 0, 0)),
            pl.BlockSpec((sps, N_HEADS, DK, DV), lambda t: (work(t), 0, 0, 0)),
        ],
        out_shape=[jax.ShapeDtypeStruct((R_MAIN, D_GLA_V), bf16),
                   jax.ShapeDtypeStruct((BATCH, N_HEADS, DK, DV), f32),
                   jax.ShapeDtypeStruct((DEC_BATCH, N_HEADS, DK, DV), f32)],
        scratch_shapes=([pltpu.VMEM((DEC_BATCH, D_GLA_V), f32)]
                        + [pltpu.VMEM((DV, DK), f32)] * N_HEADS
                        + [pltpu.VMEM((diff.shape[0], DK), f32)] * N_HEADS),
        compiler_params=_params(1),
        name="gla",
    )(proj, proj, proj, proj, la2, ng, s_meta, diff, level,
      proj, proj, proj, proj, la2, s0_sample)


def _gla_meta_kernel(k_ref, v_ref, la_ref, diff_ref, s_ref, x_ref):
    _decay_sums(la_ref[...], diff_ref, x_ref)
    kd = (k_ref[...].astype(f32) * jnp.exp(x_ref[N_META - 1:N_META, :] - x_ref[...])).astype(bf16)
    s_ref[0] = _dot(v_ref[...], kd, _TN)


def _gla_meta(proj, la2):
    diff = _gla_tables(N_META)[0][:N_META]
    mb = R_MAIN // N_META
    return pl.pallas_call(
        _gla_meta_kernel,
        grid=(N_HEADS,),
        in_specs=[
            pl.BlockSpec((N_META, DK), lambda h: (mb, COL_K // DK + h)),
            pl.BlockSpec((N_META, DV), lambda h: (mb, COL_V // DV + h)),
            pl.BlockSpec((N_META, LA_W), lambda h: (mb, h)),
            pl.BlockSpec(diff.shape, lambda h: (0, 0)),
        ],
        out_specs=pl.BlockSpec((1, DV, DK), lambda h: (h, 0, 0)),
        out_shape=jax.ShapeDtypeStruct((N_HEADS, DV, DK), f32),
        scratch_shapes=[pltpu.VMEM((N_META, DK), f32)],
        compiler_params=_params(1),
        name="gla_meta",
    )(proj, proj, la2, diff)


CONV_ROWS = 2 * DEC_BATCH
CONV_PAD = SUBLANES
CONV_SEQ_BLOCKS = SEQ // CONV_ROWS
CONV_PROMPT_BLOCKS = NP // CONV_ROWS
assert R_ALL == (CONV_PROMPT_BLOCKS + 1) * CONV_ROWS


def _conv_carry_from_meta(mc_ref, mh_ref, u_ref):
    mu = mc_ref[...].astype(f32) * mh_ref[...].astype(f32)
    u_ref[CONV_PAD - 2:CONV_PAD, :] = mu[N_META - 2:N_META, :]


def _conv_prompt_block(cb_ref, cc_ref, ch_ref, w_ref, y_ref, u_ref):
    TR = CONV_ROWS
    u = cc_ref[...].astype(f32) * ch_ref[...].astype(f32)
    w = w_ref[...]
    u_ref[CONV_PAD:CONV_PAD + TR, :] = u
    zc = (w[0:1, :] * u_ref[CONV_PAD - 2:CONV_PAD - 2 + TR, :]
          + w[1:2, :] * u_ref[CONV_PAD - 1:CONV_PAD - 1 + TR, :]
          + w[2:3, :] * u)
    y_ref[...] = (cb_ref[...].astype(f32) * zc).astype(bf16)
    last = u[TR - 2:TR, :]
    u_ref[CONV_PAD - 2:CONV_PAD, :] = last
    return last


def _conv_sample_block(cb_ref, cc_ref, ch_ref, w_ref, buf_ref, y_ref, nbs_ref):
    rows = slice(0, DEC_BATCH)
    u = cc_ref[rows, :].astype(f32) * ch_ref[rows, :].astype(f32)
    w = w_ref[...]
    b0 = buf_ref[:, 0:D_CONV]
    b1 = buf_ref[:, D_CONV:2 * D_CONV]
    zc = w[0:1, :] * b0 + w[1:2, :] * b1 + w[2:3, :] * u
    y_ref[rows, :] = (cb_ref[rows, :].astype(f32) * zc).astype(bf16)
    y_ref[DEC_BATCH:CONV_ROWS, :] = jnp.zeros((CONV_ROWS - DEC_BATCH, D_CONV), bf16)
    nbs_ref[:, 0:D_CONV] = b1
    nbs_ref[:, D_CONV:2 * D_CONV] = u


def _conv_kernel(cb_ref, cc_ref, ch_ref, mc_ref, mh_ref, w_ref, buf_ref,
                 y_ref, nbp_ref, nbs_ref, u_ref):
    t = pl.program_id(0)

    @pl.when(t < CONV_PROMPT_BLOCKS)
    def _():
        @pl.when(t % CONV_SEQ_BLOCKS == 0)
        def _():
            _conv_carry_from_meta(mc_ref, mh_ref, u_ref)

        last = _conv_prompt_block(cb_ref, cc_ref, ch_ref, w_ref, y_ref, u_ref)

        @pl.when(t % CONV_SEQ_BLOCKS == CONV_SEQ_BLOCKS - 1)
        def _():
            nbp_ref[0] = last

    @pl.when(t == CONV_PROMPT_BLOCKS)
    def _():
        _conv_sample_block(cb_ref, cc_ref, ch_ref, w_ref, buf_ref, y_ref, nbs_ref)


def _conv(proj, conv_w, buf):
    TR = CONV_ROWS
    mrow = R_MAIN // N_META
    const = lambda t: (0, 0)
    pcol = lambda col: (lambda t: (t, col // D_CONV))
    return pl.pallas_call(
        _conv_kernel,
        grid=(R_ALL // TR,),
        in_specs=[
            pl.BlockSpec((TR, D_CONV), pcol(COL_CB)),
            pl.BlockSpec((TR, D_CONV), pcol(COL_CC)),
            pl.BlockSpec((TR, D_CONV), pcol(COL_CH)),
            pl.BlockSpec((N_META, D_CONV), lambda t: (mrow, COL_CC // D_CONV)),
            pl.BlockSpec((N_META, D_CONV), lambda t: (mrow, COL_CH // D_CONV)),
            pl.BlockSpec((CONV_WIDTH, D_CONV), const),
            pl.BlockSpec((DEC_BATCH, 2 * D_CONV), const),
        ],
        out_specs=[
            pl.BlockSpec((TR, D_CONV), lambda t: (t, 0)),
            pl.BlockSpec((1, CONV_WIDTH - 1, D_CONV),
                         lambda t: (jnp.minimum(t, CONV_PROMPT_BLOCKS - 1) // CONV_SEQ_BLOCKS, 0, 0)),
            pl.BlockSpec((DEC_BATCH, 2 * D_CONV), const),
        ],
        out_shape=[jax.ShapeDtypeStruct((R_ALL, D_CONV), bf16),
                   jax.ShapeDtypeStruct((BATCH, CONV_WIDTH - 1, D_CONV), f32),
                   jax.ShapeDtypeStruct((DEC_BATCH, 2 * D_CONV), f32)],
        scratch_shapes=[pltpu.VMEM((CONV_PAD + TR, D_CONV), f32)],
        compiler_params=_params(1),
        name="conv",
    )(proj, proj, proj, proj, proj, conv_w, buf)


OUT_FULL = TAIL_ROW0 // OUT_TM


def _outproj_kernel(o_ref, y_ref, wo_ref, wy_ref, x_ref, mu_ref, rs_ref, g_ref, b_ref, ht_ref,
                    s_ref):
    i = pl.program_id(0)

    def mix():
        return (_dot(o_ref[...], wo_ref[...].astype(bf16))
                + _dot(y_ref[...], wy_ref[...].astype(bf16)))

    @pl.when(i < OUT_FULL)
    def _():
        s_ref[...] = _ln_from_stats(x_ref[...], mu_ref, rs_ref, g_ref[...], b_ref[...]) + mix()

    @pl.when(i >= OUT_FULL)
    def _():
        s_ref[...] = ALPHA * ht_ref[...] + mix()


def _outproj(og, yc, w_out, xp, mu, rs, g, b, h_tail):
    full = lambda i: i < OUT_FULL
    return pl.pallas_call(
        _outproj_kernel,
        grid=(R_MAIN // OUT_TM, D_MODEL // OUT_TN),
        in_specs=[
            pl.BlockSpec((OUT_TM, D_GLA_V), lambda i, j: (i, 0)),
            pl.BlockSpec((OUT_TM, D_CONV), lambda i, j: (i, 0)),
            pl.BlockSpec((D_GLA_V, OUT_TN), lambda i, j: (0, j)),
            pl.BlockSpec((D_CONV, OUT_TN), lambda i, j: (1, j)),
            pl.BlockSpec((OUT_TM, OUT_TN),
                         lambda i, j: (jnp.minimum(i, OUT_FULL - 1), jnp.where(full(i), j, 0))),
            pl.BlockSpec((OUT_TM, LANES), lambda i, j: (i, 0)),
            pl.BlockSpec((OUT_TM, LANES), lambda i, j: (i, 0)),
            pl.BlockSpec((1, OUT_TN), lambda i, j: (0, j)),
            pl.BlockSpec((1, OUT_TN), lambda i, j: (0, j)),
            pl.BlockSpec((OUT_TM, OUT_TN), lambda i, j: (0, jnp.where(full(i), 0, j))),
        ],
        out_specs=pl.BlockSpec((OUT_TM, OUT_TN), lambda i, j: (i, j)),
        out_shape=jax.ShapeDtypeStruct((R_MAIN, D_MODEL), f32),
        compiler_params=_params(2),
        name="outproj",
    )(og, yc, w_out, w_out, xp, mu, rs, g, b, h_tail)


LN1_ROWS = 320


def _ln1_kernel(s_ref, g_ref, b_ref, hb_ref, mu_ref, rs_ref):
    x = s_ref[...]
    mu, rs = _ln_stats(x)
    hb_ref[...] = ((x - mu) * rs * g_ref[...] + b_ref[...]).astype(bf16)
    _store_stats(mu, rs, mu_ref, rs_ref)


def _ln1(s1, g, b):
    return pl.pallas_call(
        _ln1_kernel,
        grid=(R_MAIN // LN1_ROWS,),
        in_specs=[
            pl.BlockSpec((LN1_ROWS, D_MODEL), lambda i: (i, 0)),
            pl.BlockSpec((1, D_MODEL), lambda i: (0, 0)),
            pl.BlockSpec((1, D_MODEL), lambda i: (0, 0)),
        ],
        out_specs=[
            pl.BlockSpec((LN1_ROWS, D_MODEL), lambda i: (i, 0)),
            pl.BlockSpec((LN1_ROWS, LANES), lambda i: (i, 0)),
            pl.BlockSpec((LN1_ROWS, LANES), lambda i: (i, 0)),
        ],
        out_shape=[jax.ShapeDtypeStruct((R_MAIN, D_MODEL), bf16),
                   jax.ShapeDtypeStruct((R_MAIN, LANES), f32),
                   jax.ShapeDtypeStruct((R_MAIN, LANES), f32)],
        compiler_params=_params(1),
        name="ln1",
    )(s1, g, b)


LN2_ROWS = 512


def _ln2_kernel(sp_ref, ss_ref, g_ref, b_ref, yp_ref, ys_ref):
    yp_ref[...] = _layer_norm(sp_ref[...], g_ref[...], b_ref[...])

    @pl.when(pl.program_id(0) == 0)
    def _():
        ys_ref[...] = _layer_norm(ss_ref[...], g_ref[...], b_ref[...])


def _ln2(s2, g, b):
    return pl.pallas_call(
        _ln2_kernel,
        grid=(NP // LN2_ROWS,),
        in_specs=[
            pl.BlockSpec((LN2_ROWS, D_MODEL), lambda i: (i, 0)),
            pl.BlockSpec((DEC_BATCH, D_MODEL), lambda i: (NP // DEC_BATCH, 0)),
            pl.BlockSpec((1, D_MODEL), lambda i: (0, 0)),
            pl.BlockSpec((1, D_MODEL), lambda i: (0, 0)),
        ],
        out_specs=[
            pl.BlockSpec((LN2_ROWS, D_MODEL), lambda i: (i, 0)),
            pl.BlockSpec((DEC_BATCH, D_MODEL), lambda i: (0, 0)),
        ],
        out_shape=[jax.ShapeDtypeStruct((NP, D_MODEL), f32),
                   jax.ShapeDtypeStruct((DEC_BATCH, D_MODEL), f32)],
        compiler_params=_params(1),
        name="ln2",
    )(s2, s2, g, b)


FFN_TM, FFN_TN = 2080, 256


FFN_NT = D_FF // FFN_TN


def _ffn_up_kernel(h_ref, wg_ref, wu_ref, wd_ref, act_ref, wdb_ref):
    i = pl.program_id(0)
    j = pl.program_id(1)

    @pl.when(j < FFN_NT)
    def _():
        h = h_ref[...]
        a = _dot(h, wg_ref[...].astype(bf16))
        u = _dot(h, wu_ref[...].astype(bf16))
        act_ref[...] = (a * jax.nn.sigmoid(a) * u).astype(bf16)

    @pl.when(j >= FFN_NT)
    def _():
        act_ref[...] = jnp.zeros_like(act_ref)

    @pl.when(jnp.logical_and(i == 0, j < FFN_NT))
    def _():
        wdb_ref[...] = wd_ref[...].astype(bf16)

    @pl.when(jnp.logical_and(i == 0, j >= FFN_NT))
    def _():
        wdb_ref[...] = jnp.zeros_like(wdb_ref)


def _ffn_up(hb, wg, wu, wd):
    nt_pad = D_FF_PAD // FFN_TN
    jw = lambda j: jnp.minimum(j, FFN_NT - 1)
    return pl.pallas_call(
        _ffn_up_kernel,
        grid=(R_MAIN // FFN_TM, nt_pad),
        in_specs=[
            pl.BlockSpec((FFN_TM, D_MODEL), lambda i, j: (i, 0), pipeline_mode=pl.Buffered(1)),
            pl.BlockSpec((D_MODEL, FFN_TN), lambda i, j: (0, jw(j))),
            pl.BlockSpec((D_MODEL, FFN_TN), lambda i, j: (0, jw(j))),
            pl.BlockSpec((FFN_TN, D_MODEL), lambda i, j: (jnp.where(i == 0, jw(j), FFN_NT - 1), 0)),
        ],
        out_specs=[
            pl.BlockSpec((FFN_TM, FFN_TN), lambda i, j: (i, j)),
            pl.BlockSpec((FFN_TN, D_MODEL), lambda i, j: (jnp.where(i == 0, j, nt_pad - 1), 0)),
        ],
        out_shape=[jax.ShapeDtypeStruct((R_MAIN, D_FF_PAD), bf16),
                   jax.ShapeDtypeStruct((D_FF_PAD, D_MODEL), bf16)],
        compiler_params=_params(2),
        name="ffn_up",
    )(hb, wg, wu, wd)


DOWN_TM, DOWN_TN, DOWN_TK = 2080, 1024, 1024


def _ffn_down_kernel(a_ref, w_ref, s1_ref, mu_ref, rs_ref, g_ref, b_ref, s_ref):
    @pl.when(pl.program_id(2) == 0)
    def _():
        s_ref[...] = (_ln_from_stats(s1_ref[...], mu_ref, rs_ref, g_ref[...], b_ref[...])
                      + _dot(a_ref[...], w_ref[...]))

    @pl.when(pl.program_id(2) > 0)
    def _():
        s_ref[...] += _dot(a_ref[...], w_ref[...])


def _ffn_down(act, wd, s1, mu, rs, g, b):
    return pl.pallas_call(
        _ffn_down_kernel,
        grid=(R_MAIN // DOWN_TM, D_MODEL // DOWN_TN, D_FF_PAD // DOWN_TK),
        in_specs=[
            pl.BlockSpec((DOWN_TM, DOWN_TK), lambda i, j, k: (i, k)),
            pl.BlockSpec((DOWN_TK, DOWN_TN), lambda i, j, k: (k, j)),
            pl.BlockSpec((DOWN_TM, DOWN_TN), lambda i, j, k: (i, j)),
            pl.BlockSpec((DOWN_TM, LANES), lambda i, j, k: (i, 0)),
            pl.BlockSpec((DOWN_TM, LANES), lambda i, j, k: (i, 0)),
            pl.BlockSpec((1, DOWN_TN), lambda i, j, k: (0, j)),
            pl.BlockSpec((1, DOWN_TN), lambda i, j, k: (0, j)),
        ],
        out_specs=pl.BlockSpec((DOWN_TM, DOWN_TN), lambda i, j, k: (i, j)),
        out_shape=jax.ShapeDtypeStruct((R_MAIN, D_MODEL), f32),
        compiler_params=_params(3),
        name="ffn_down",
    )(act, wd, s1, mu, rs, g, b)


def kernel(x_prompt, x_sample, state_gla, state_conv, meta_tokens, emb_ln_g, emb_ln_b,
           w_in, w_a2, b_a, gla_norm_g, conv_w, w_out, ln1_g, ln1_b,
           w_ffn_gate, w_ffn_up, w_ffn_down, ln2_g, ln2_b):
    assert x_prompt.shape == (BATCH, SEQ, D_MODEL) and x_sample.shape == (DEC_BATCH, 1, D_MODEL)
    assert w_in.shape[0] == 1, "single layer"
    row = lambda v: v.reshape(1, -1)

    w_in_t = jnp.transpose(w_in[0])
    wa2p = jnp.pad(w_a2[0], ((0, LANES - GATE_RANK), (0, 0)))

    xp = x_prompt.reshape(NP, D_MODEL)
    xt = jnp.concatenate([x_sample.reshape(DEC_BATCH, D_MODEL), meta_tokens.astype(f32),
                          jnp.zeros((LN_ROWS - DEC_BATCH - N_META, D_MODEL), f32)], axis=0)

    g0, b0 = row(emb_ln_g), row(emb_ln_b)
    g1, b1 = row(ln1_g[0]), row(ln1_b[0])
    hb, alr, mu0, rs0, h_tail = _ln0(xp, xt, g0, b0, w_in_t)
    proj, la2 = _proj(hb, w_in_t, alr, wa2p, row(b_a[0]))

    ng = row(gla_norm_g[0])
    s_meta = _gla_meta(proj, la2)
    og, s_p, s_s = _gla(proj, la2, ng, s_meta, state_gla[0])
    yc, nb_p, nb_s = _conv(proj, conv_w[0], state_conv[0].reshape(DEC_BATCH, 2 * D_CONV))

    s1 = _outproj(og, yc, w_out[0], xp, mu0, rs0, ALPHA * g0, ALPHA * b0, h_tail)
    h1b, mu1, rs1 = _ln1(s1, g1, b1)
    act, wd = _ffn_up(h1b, w_ffn_gate[0], w_ffn_up[0], w_ffn_down[0])
    s2 = _ffn_down(act, wd, s1, mu1, rs1, ALPHA * g1, ALPHA * b1)
    y_p, y_s = _ln2(s2, row(ln2_g[0]), row(ln2_b[0]))

    return (y_p.reshape(BATCH, SEQ, D_MODEL),
            y_s.reshape(DEC_BATCH, 1, D_MODEL),
            s_p[None],
            nb_p[None],
            s_s[None],
            nb_s.reshape(1, DEC_BATCH, CONV_WIDTH - 1, D_CONV))
```

```python
import jax
import jax.numpy as jnp
import numpy as np
from jax import lax
from jax.experimental import pallas as pl
from jax.experimental.pallas import tpu as pltpu

f32 = jnp.float32
bf16 = jnp.bfloat16

D_MODEL = 4096
BATCH = 4
SEQ = 2048
DEC_BATCH = 128
N_META = 16
D_GLA_V = D_MODEL // 2
D_CONV = D_MODEL - D_GLA_V
D_GLA_K = D_GLA_V // 2
N_HEADS = 4
DK = D_GLA_K // N_HEADS
DV = D_GLA_V // N_HEADS
GATE_RANK = 16
GATE_TAU = 16.0
CONV_WIDTH = 3
D_FF = 11008
ALPHA = 2.0 ** 0.25
LN_EPS = 1e-5
RMS_EPS = 1e-6

LANES = 128
SUBLANES = 8
NP = BATCH * SEQ
R_MAIN = NP + DEC_BATCH
R_ALL = R_MAIN + LANES
D_PROJ = 2 * D_GLA_K + 2 * D_GLA_V + 3 * D_CONV
D_FF_PAD = 11264
CHUNK = 128
N_CHUNKS = SEQ // CHUNK
VMEM_LIMIT = 58 * 1024 * 1024

COL_Q, COL_K, COL_V, COL_G = 0, D_GLA_K, 2 * D_GLA_K, 2 * D_GLA_K + D_GLA_V
COL_CB = COL_G + D_GLA_V
COL_CC = COL_CB + D_CONV
COL_CH = COL_CC + D_CONV


def _params(n_axes):
    return pltpu.CompilerParams(
        dimension_semantics=("arbitrary",) * n_axes, vmem_limit_bytes=VMEM_LIMIT)


def _ln_stats(x):
    mu = jnp.mean(x, axis=-1, keepdims=True)
    xc = x - mu
    var = jnp.mean(xc * xc, axis=-1, keepdims=True)
    return mu, lax.rsqrt(var + LN_EPS)


def _layer_norm(x, g, b):
    mu, rs = _ln_stats(x)
    return (x - mu) * rs * g + b


def _ln_from_stats(x, mu_ref, rs_ref, g, b):
    reps = x.shape[-1] // LANES
    mu = jnp.concatenate([mu_ref[...]] * reps, axis=1)
    rs = jnp.concatenate([rs_ref[...]] * reps, axis=1)
    return (x - mu) * rs * g + b


def _store_stats(mu, rs, mu_ref, rs_ref):
    mu_ref[...] = jnp.broadcast_to(mu, mu_ref.shape)
    rs_ref[...] = jnp.broadcast_to(rs, rs_ref.shape)


def _split2(x):
    hi = x.astype(bf16)
    return hi, (x - hi.astype(f32)).astype(bf16)


def _dot(a, b, dims=(((1,), (0,)), ((), ()))):
    return lax.dot_general(a, b, dims, preferred_element_type=f32)


_NT = (((1,), (1,)), ((), ()))
_TN = (((0,), (0,)), ((), ()))


def _dot_f32(a, b):
    ah, am = _split2(a)
    bh, bm = _split2(b)
    return _dot(ah, bh) + _dot(ah, bm) + _dot(am, bh)


def _log2(n):
    assert n & (n - 1) == 0
    return n.bit_length() - 1


def _log_sigmoid(z):
    return jnp.minimum(z, 0.0) - jnp.log(1.0 + jnp.exp(-jnp.abs(z)))


LN_ROWS = 256


OUT_TM, OUT_TN = 1664, 256
TAIL_ROW0 = (R_MAIN // OUT_TM - 1) * OUT_TM
assert TAIL_ROW0 <= NP and TAIL_ROW0 % LN_ROWS == 0


def _ln0_kernel(xp_ref, xt_ref, g_ref, b_ref, wa_ref,
                hb_ref, alr_ref, mu_ref, rs_ref, ht_ref, wab_ref):
    i = pl.program_id(0)

    @pl.when(i == 0)
    def _():
        wab_ref[...] = wa_ref[...].astype(bf16)

    def emit(x):
        mu, rs = _ln_stats(x)
        h = (x - mu) * rs * g_ref[...] + b_ref[...]
        hb = h.astype(bf16)
        hb_ref[...] = hb
        alr_ref[...] = _dot(hb, wab_ref[...], _NT)
        _store_stats(mu, rs, mu_ref, rs_ref)
        ht_ref[...] = h

    @pl.when(i < NP // LN_ROWS)
    def _():
        emit(xp_ref[...])

    @pl.when(i >= NP // LN_ROWS)
    def _():
        emit(xt_ref[...])


def _ln0(xp, xt, g, b, w_in_t):
    n_p = NP // LN_ROWS
    t0 = TAIL_ROW0 // LN_ROWS
    return pl.pallas_call(
        _ln0_kernel,
        grid=(R_ALL // LN_ROWS,),
        in_specs=[
            pl.BlockSpec((LN_ROWS, D_MODEL), lambda i: (jnp.minimum(i, n_p - 1), 0)),
            pl.BlockSpec((LN_ROWS, D_MODEL), lambda i: (0, 0)),
            pl.BlockSpec((1, D_MODEL), lambda i: (0, 0)),
            pl.BlockSpec((1, D_MODEL), lambda i: (0, 0)),
            pl.BlockSpec((LANES, D_MODEL), lambda i: (COL_CB // LANES, 0)),
        ],
        out_specs=[
            pl.BlockSpec((LN_ROWS, D_MODEL), lambda i: (i, 0)),
            pl.BlockSpec((LN_ROWS, LANES), lambda i: (i, 0)),
            pl.BlockSpec((LN_ROWS, LANES), lambda i: (i, 0)),
            pl.BlockSpec((LN_ROWS, LANES), lambda i: (i, 0)),
            pl.BlockSpec((LN_ROWS, D_MODEL), lambda i: (jnp.maximum(i - t0, 0), 0)),
        ],
        out_shape=[jax.ShapeDtypeStruct((R_ALL, D_MODEL), bf16),
                   jax.ShapeDtypeStruct((R_ALL, LANES), f32),
                   jax.ShapeDtypeStruct((R_ALL, LANES), f32),
                   jax.ShapeDtypeStruct((R_ALL, LANES), f32),
                   jax.ShapeDtypeStruct((R_ALL - TAIL_ROW0, D_MODEL), f32)],
        scratch_shapes=[pltpu.VMEM((LANES, D_MODEL), bf16)],
        compiler_params=_params(1),
        name="ln0",
    )(xp, xt, g, b, w_in_t)


PROJ_TM, PROJ_TN = 1408, 512


LA_W = 2 * DK


def _proj_kernel(h_ref, w_ref, alr_ref, wa2_ref, ba_ref, p_ref, la_ref):
    j = pl.program_id(1)

    @pl.when(j < N_HEADS)
    def _():
        p_ref[...] = _dot(h_ref[...], w_ref[...].astype(bf16), _NT).astype(bf16)
        la_ref[:, 0:DK], la_ref[:, DK:LA_W] = _split2(
            _gate_log_decay(alr_ref[...], wa2_ref[...], ba_ref[...]))

    is_g = jnp.logical_and(j * PROJ_TN >= COL_G, j * PROJ_TN < COL_CB)

    @pl.when(is_g)
    def _():
        p = _dot(h_ref[...], w_ref[...].astype(bf16), _NT)
        p_ref[...] = (p * jax.nn.sigmoid(p)).astype(bf16)

    @pl.when(jnp.logical_and(j >= N_HEADS, jnp.logical_not(is_g)))
    def _():
        p_ref[...] = _dot(h_ref[...], w_ref[...].astype(bf16), _NT).astype(bf16)


def _proj(hb, w_in_t, alr, wa2p, ba):
    assert COL_G >= N_HEADS * PROJ_TN

    def w_row(j):
        return pl.multiple_of(j * PROJ_TN + jnp.where(j * PROJ_TN >= COL_CB, GATE_RANK, 0), GATE_RANK)

    jh = lambda j: jnp.minimum(j, N_HEADS - 1)
    return pl.pallas_call(
        _proj_kernel,
        grid=(R_ALL // PROJ_TM, D_PROJ // PROJ_TN),
        in_specs=[
            pl.BlockSpec((PROJ_TM, D_MODEL), lambda i, j: (i, 0)),
            pl.BlockSpec((pl.Element(PROJ_TN), pl.Element(D_MODEL)), lambda i, j: (w_row(j), 0)),
            pl.BlockSpec((PROJ_TM, LANES), lambda i, j: (i, 0)),
            pl.BlockSpec((LANES, DK), lambda i, j: (0, jh(j))),
            pl.BlockSpec((1, DK), lambda i, j: (0, jh(j))),
        ],
        out_specs=[
            pl.BlockSpec((PROJ_TM, PROJ_TN), lambda i, j: (i, j)),
            pl.BlockSpec((PROJ_TM, LA_W), lambda i, j: (i, jh(j))),
        ],
        out_shape=[jax.ShapeDtypeStruct((R_ALL, D_PROJ), bf16),
                   jax.ShapeDtypeStruct((R_ALL, N_HEADS * LA_W), bf16)],
        compiler_params=_params(2),
        name="proj",
    )(hb, w_in_t, alr, wa2p, ba)


GLA_BLK = 128
GLA_HPS = 4
GLA_ILV = 2
LVL_DIAG = _log2(GLA_BLK)


def _gla_levels(c):
    return [c >> (t + 1) for t in range(_log2(c))]


def _gla_tables(c):
    idx = np.arange(c)
    tri = (idx[None, :] <= idx[:, None]).astype(np.float32)
    mats = [tri]
    for m in _gla_levels(c):
        ref = ((idx // m) | 1) * m - 1
        mats.append(tri - tri[ref])
    diff = np.concatenate(mats, axis=0)
    diff = np.concatenate([diff, diff], axis=1)
    t = np.arange(min(c, GLA_BLK))
    level = np.full((t.size, t.size), -1, np.int32)
    for m in _gla_levels(t.size):
        same_pair = (t[:, None] // (2 * m)) == (t[None, :] // (2 * m))
        odd_even = (((t[:, None] // m) & 1) == 1) & (((t[None, :] // m) & 1) == 0)
        level[same_pair & odd_even] = _log2(m)
    level[t[:, None] == t[None, :]] = LVL_DIAG
    return jnp.asarray(diff, bf16), jnp.asarray(level)


def _gate_log_decay(alr, wa2, ba):
    return _log_sigmoid(_dot_f32(alr, wa2) + ba) * (1.0 / GATE_TAU)


def _decay_sums(la2, diff_ref, x_ref):
    x_ref[...] = _dot(diff_ref[...], jnp.concatenate([la2[:, 0:DK], la2[:, DK:LA_W]], axis=0))


GLA_STEPS = BATCH * N_CHUNKS
SAMPLE_PER_STEP = DEC_BATCH // GLA_STEPS


def _gla_kernel(q_ref, k_ref, v_ref, g_ref, la_ref, ng_ref, s0_ref, diff_ref, lvl_ref,
                qs_ref, ks_ref, vs_ref, gs_ref, las_ref, s0s_ref,
                og_ref, sfin_ref, ss_ref, ogacc_ref, *head_scratch):
    step = pl.program_id(0)

    @pl.when(step < GLA_STEPS)
    def _():
        _gla_step(step, q_ref, k_ref, v_ref, g_ref, la_ref, ng_ref, s0_ref, diff_ref, lvl_ref,
                  qs_ref, ks_ref, vs_ref, gs_ref, las_ref, s0s_ref,
                  og_ref, sfin_ref, ss_ref, ogacc_ref, *head_scratch)

    @pl.when(step == GLA_STEPS)
    def _():
        og_ref[...] = ogacc_ref[...].astype(bf16)


def _gla_step(step, q_ref, k_ref, v_ref, g_ref, la_ref, ng_ref, s0_ref, diff_ref, lvl_ref,
              qs_ref, ks_ref, vs_ref, gs_ref, las_ref, s0s_ref,
              og_ref, sfin_ref, ss_ref, ogacc_ref, *head_scratch):
    s = step % N_CHUNKS
    st_refs, x_refs = head_scratch[:GLA_HPS], head_scratch[GLA_HPS:]

    @pl.when(s == 0)
    def _():
        for hh in range(GLA_HPS):
            st_refs[hh][...] = s0_ref[hh]

    @pl.when(step == 0)
    def _():
        ogacc_ref[...] = jnp.zeros_like(ogacc_ref)

    for h0 in range(0, GLA_HPS, GLA_ILV):
        heads = [_gla_prompt_head(hh, q_ref, k_ref, v_ref, g_ref, la_ref, ng_ref,
                                  diff_ref, lvl_ref, og_ref, st_refs[hh], x_refs[hh])
                 for hh in range(h0, h0 + GLA_ILV)]
        while heads:
            heads = [h for h in heads if next(h, _DONE) is not _DONE]

    rows = lax.broadcasted_iota(jnp.int32, (DEC_BATCH, 1), 0)
    for hh in range(N_HEADS):
        vcols = slice(hh * DV, (hh + 1) * DV)
        acc = ogacc_ref[:, vcols]
        for t in range(SAMPLE_PER_STEP):
            r = step * SAMPLE_PER_STEP + t
            og_row = _gla_sample_token(r, t, hh, qs_ref, ks_ref, vs_ref, gs_ref, las_ref, ng_ref,
                                       s0s_ref, ss_ref)
            acc = jnp.where(rows == r, og_row, acc)
        ogacc_ref[:, vcols] = acc

    @pl.when(s == N_CHUNKS - 1)
    def _():
        for hh in range(GLA_HPS):
            sfin_ref[0, hh] = st_refs[hh][...].T


def _gla_sample_token(r, t, hh, qs_ref, ks_ref, vs_ref, gs_ref, las_ref, ng_ref, s0s_ref, ss_ref):
    kcols = slice(hh * DK, (hh + 1) * DK)
    vcols = slice(hh * DV, (hh + 1) * DV)
    as_col = (lax.broadcasted_iota(jnp.int32, (DEC_BATCH, LANES), 0) == r).astype(bf16)
    as_row = (lax.broadcasted_iota(jnp.int32, (SUBLANES, DEC_BATCH), 1) == r).astype(bf16)
    la_col = (_dot(las_ref[:, hh * LA_W:hh * LA_W + DK], as_col, _TN)
              + _dot(las_ref[:, hh * LA_W + DK:(hh + 1) * LA_W], as_col, _TN))
    a_col = jnp.exp(la_col)
    k_col = _dot(ks_ref[:, kcols], as_col, _TN)
    q_col = _dot(qs_ref[:, kcols], as_col, _TN) * (DK ** -0.5)
    v_row = _dot(as_row, vs_ref[:, vcols])[0:1]
    g_row = _dot(as_row, gs_ref[:, vcols])[0:1]
    wide = lambda c: jnp.concatenate([c] * (DV // LANES), axis=1)
    s_new = wide(a_col) * s0s_ref[t, hh] + wide(k_col) * v_row
    ss_ref[t, hh] = s_new
    o = jnp.sum(wide(q_col) * s_new, axis=0, keepdims=True)
    ms = jnp.mean(o * o, axis=-1, keepdims=True)
    return o * lax.rsqrt(ms + RMS_EPS) * ng_ref[...] * g_row


_DONE = object()


def _gla_prompt_head(hh, q_ref, k_ref, v_ref, g_ref, la_ref, ng_ref,
                     diff_ref, lvl_ref, og_ref, st_ref, x_ref):
    C = CHUNK
    nblk = C // GLA_BLK
    kcols = slice(hh * DK, (hh + 1) * DK)
    vcols = slice(hh * DV, (hh + 1) * DV)

    _decay_sums(la_ref[:, hh * LA_W:(hh + 1) * LA_W], diff_ref, x_ref)
    yield
    kb = k_ref[:, kcols]
    kf = kb.astype(f32)
    vb = v_ref[:, vcols]
    qf = q_ref[:, kcols].astype(f32) * (DK ** -0.5)
    qb = qf.astype(bf16)
    b = x_ref[0:C, :]
    b_last = x_ref[C - 1:C, :]

    lvl = lvl_ref[...]
    blk_rows = lambda a, i: a[i * GLA_BLK:(i + 1) * GLA_BLK]
    tiles = [jnp.where(lvl == LVL_DIAG, _dot(blk_rows(qb, i), blk_rows(kb, i), _NT), 0.0)
             for i in range(nblk)]
    cross = None
    row = lax.broadcasted_iota(jnp.int32, (C, 1), 0)
    for t, m in enumerate(_gla_levels(C)):
        yield
        e = jnp.exp(-jnp.abs(x_ref[(t + 1) * C:(t + 2) * C, :]))
        if m == GLA_BLK:
            qt = (blk_rows(qf, 1) * blk_rows(e, 1)).astype(bf16)
            kt = (blk_rows(kf, 0) * blk_rows(e, 0)).astype(bf16)
            cross = _dot(qt, kt, _NT)
            continue
        if m >= 8:
            n = C // (2 * m)
            e3, q3, k3 = (a.reshape(n, 2 * m, DK) for a in (e, qf, kf))
            zero = jnp.zeros((n, m, DK), f32)
            qt = jnp.concatenate([zero, q3[:, m:, :] * e3[:, m:, :]], axis=1).reshape(C, DK)
            kt = jnp.concatenate([k3[:, :m, :] * e3[:, :m, :], zero], axis=1).reshape(C, DK)
        else:
            odd = ((row >> _log2(m)) & 1) == 1
            qt = jnp.where(odd, qf * e, 0.0)
            kt = jnp.where(odd, 0.0, kf * e)
        qt, kt = qt.astype(bf16), kt.astype(bf16)
        tiles = [jnp.where(lvl == _log2(m), _dot(blk_rows(qt, i), blk_rows(kt, i), _NT), tiles[i])
                 for i in range(nblk)]

    yield
    st = st_ref[...]
    qe = (qf * jnp.exp(b)).astype(bf16)
    outs = []
    for i in range(nblk):
        lhs = tiles[i] if i == 0 else jnp.concatenate([cross, tiles[i]], axis=1)
        outs.append(_dot(lhs.astype(bf16), vb[0:(i + 1) * GLA_BLK]))
    o = _dot(qe, st.astype(bf16), _NT) + jnp.concatenate(outs, axis=0)
    ms = jnp.mean(o * o, axis=-1, keepdims=True)
    on = o * lax.rsqrt(ms + RMS_EPS) * ng_ref[...]
    og_ref[:, vcols] = (on * g_ref[:, vcols].astype(f32)).astype(bf16)

    yield
    kd = (kf * jnp.exp(b_last - b)).astype(bf16)
    st_new = st * jnp.exp(b_last) + _dot(vb, kd, _TN)
    st_ref[...] = st_new


def _gla(proj, la2, ng, s_meta, s0_sample):
    assert CHUNK == GLA_BLK == DEC_BATCH and GLA_HPS == N_HEADS
    assert DEC_BATCH % GLA_STEPS == 0
    diff, level = _gla_tables(CHUNK)
    sps = SAMPLE_PER_STEP
    const = lambda t: (0, 0)
    work = lambda t: jnp.minimum(t, GLA_STEPS - 1)
    srow = NP // DEC_BATCH
    pcol = lambda col, width: (lambda t: (work(t), col // width))
    scol = lambda col, width: (lambda t: (srow, col // width))
    return pl.pallas_call(
        _gla_kernel,
        grid=(GLA_STEPS + 1,),
        in_specs=[
            pl.BlockSpec((CHUNK, D_GLA_K), pcol(COL_Q, D_GLA_K)),
            pl.BlockSpec((CHUNK, D_GLA_K), pcol(COL_K, D_GLA_K)),
            pl.BlockSpec((CHUNK, D_GLA_V), pcol(COL_V, D_GLA_V)),
            pl.BlockSpec((CHUNK, D_GLA_V), pcol(COL_G, D_GLA_V)),
            pl.BlockSpec((CHUNK, N_HEADS * LA_W), lambda t: (work(t), 0)),
            pl.BlockSpec((1, DV), const),
            pl.BlockSpec((N_HEADS, DV, DK), lambda t: (0, 0, 0)),
            pl.BlockSpec(diff.shape, const),
            pl.BlockSpec(level.shape, const),
            pl.BlockSpec((DEC_BATCH, D_GLA_K), scol(COL_Q, D_GLA_K)),
            pl.BlockSpec((DEC_BATCH, D_GLA_K), scol(COL_K, D_GLA_K)),
            pl.BlockSpec((DEC_BATCH, D_GLA_V), scol(COL_V, D_GLA_V)),
            pl.BlockSpec((DEC_BATCH, D_GLA_V), scol(COL_G, D_GLA_V)),
            pl.BlockSpec((DEC_BATCH, N_HEADS * LA_W), lambda t: (srow, 0)),
            pl.BlockSpec((sps, N_HEADS, DK, DV), lambda t: (work(t), 0, 0, 0)),
        ],
        out_specs=[
            pl.BlockSpec((CHUNK, D_GLA_V), lambda t: (t, 0)),
            pl.BlockSpec((1, N_HEADS, DK, DV), lambda t: (work(t) // N_CHUNKS, 0, 0, 0)),
            pl.BlockSpec((sps, N_HEADS, DK, DV), lambda t: (work(t), 0, 0, 0)),
        ],
        out_shape=[jax.ShapeDtypeStruct((R_MAIN, D_GLA_V), bf16),
                   jax.ShapeDtypeStruct((BATCH, N_HEADS, DK, DV), f32),
                   jax.ShapeDtypeStruct((DEC_BATCH, N_HEADS, DK, DV), f32)],
        scratch_shapes=([pltpu.VMEM((DEC_BATCH, D_GLA_V), f32)]
                        + [pltpu.VMEM((DV, DK), f32)] * N_HEADS
                        + [pltpu.VMEM((diff.shape[0], DK), f32)] * N_HEADS),
        compiler_params=_params(1),
        name="gla",
    )(proj, proj, proj, proj, la2, ng, s_meta, diff, level,
      proj, proj, proj, proj, la2, s0_sample)


def _gla_meta_kernel(k_ref, v_ref, la_ref, diff_ref, s_ref, x_ref):
    _decay_sums(la_ref[...], diff_ref, x_ref)
    kd = (k_ref[...].astype(f32) * jnp.exp(x_ref[N_META - 1:N_META, :] - x_ref[...])).astype(bf16)
    s_ref[0] = _dot(v_ref[...], kd, _TN)


def _gla_meta(proj, la2):
    diff = _gla_tables(N_META)[0][:N_META]
    mb = R_MAIN // N_META
    return pl.pallas_call(
        _gla_meta_kernel,
        grid=(N_HEADS,),
        in_specs=[
            pl.BlockSpec((N_META, DK), lambda h: (mb, COL_K // DK + h)),
            pl.BlockSpec((N_META, DV), lambda h: (mb, COL_V // DV + h)),
            pl.BlockSpec((N_META, LA_W), lambda h: (mb, h)),
            pl.BlockSpec(diff.shape, lambda h: (0, 0)),
        ],
        out_specs=pl.BlockSpec((1, DV, DK), lambda h: (h, 0, 0)),
        out_shape=jax.ShapeDtypeStruct((N_HEADS, DV, DK), f32),
        scratch_shapes=[pltpu.VMEM((N_META, DK), f32)],
        compiler_params=_params(1),
        name="gla_meta",
    )(proj, proj, la2, diff)


CONV_ROWS = 2 * DEC_BATCH
CONV_PAD = SUBLANES
CONV_SEQ_BLOCKS = SEQ // CONV_ROWS
CONV_PROMPT_BLOCKS = NP // CONV_ROWS
assert R_ALL == (CONV_PROMPT_BLOCKS + 1) * CONV_ROWS


def _conv_carry_from_meta(mc_ref, mh_ref, u_ref):
    mu = mc_ref[...].astype(f32) * mh_ref[...].astype(f32)
    u_ref[CONV_PAD - 2:CONV_PAD, :] = mu[N_META - 2:N_META, :]


def _conv_prompt_block(cb_ref, cc_ref, ch_ref, w_ref, y_ref, u_ref):
    TR = CONV_ROWS
    u = cc_ref[...].astype(f32) * ch_ref[...].astype(f32)
    w = w_ref[...]
    u_ref[CONV_PAD:CONV_PAD + TR, :] = u
    zc = (w[0:1, :] * u_ref[CONV_PAD - 2:CONV_PAD - 2 + TR, :]
          + w[1:2, :] * u_ref[CONV_PAD - 1:CONV_PAD - 1 + TR, :]
          + w[2:3, :] * u)
    y_ref[...] = (cb_ref[...].astype(f32) * zc).astype(bf16)
    last = u[TR - 2:TR, :]
    u_ref[CONV_PAD - 2:CONV_PAD, :] = last
    return last


def _conv_sample_block(cb_ref, cc_ref, ch_ref, w_ref, buf_ref, y_ref, nbs_ref):
    rows = slice(0, DEC_BATCH)
    u = cc_ref[rows, :].astype(f32) * ch_ref[rows, :].astype(f32)
    w = w_ref[...]
    b0 = buf_ref[:, 0:D_CONV]
    b1 = buf_ref[:, D_CONV:2 * D_CONV]
    zc = w[0:1, :] * b0 + w[1:2, :] * b1 + w[2:3, :] * u
    y_ref[rows, :] = (cb_ref[rows, :].astype(f32) * zc).astype(bf16)
    y_ref[DEC_BATCH:CONV_ROWS, :] = jnp.zeros((CONV_ROWS - DEC_BATCH, D_CONV), bf16)
    nbs_ref[:, 0:D_CONV] = b1
    nbs_ref[:, D_CONV:2 * D_CONV] = u


def _conv_kernel(cb_ref, cc_ref, ch_ref, mc_ref, mh_ref, w_ref, buf_ref,
                 y_ref, nbp_ref, nbs_ref, u_ref):
    t = pl.program_id(0)

    @pl.when(t < CONV_PROMPT_BLOCKS)
    def _():
        @pl.when(t % CONV_SEQ_BLOCKS == 0)
        def _():
            _conv_carry_from_meta(mc_ref, mh_ref, u_ref)

        last = _conv_prompt_block(cb_ref, cc_ref, ch_ref, w_ref, y_ref, u_ref)

        @pl.when(t % CONV_SEQ_BLOCKS == CONV_SEQ_BLOCKS - 1)
        def _():
            nbp_ref[0] = last

    @pl.when(t == CONV_PROMPT_BLOCKS)
    def _():
        _conv_sample_block(cb_ref, cc_ref, ch_ref, w_ref, buf_ref, y_ref, nbs_ref)


def _conv(proj, conv_w, buf):
    TR = CONV_ROWS
    mrow = R_MAIN // N_META
    const = lambda t: (0, 0)
    pcol = lambda col: (lambda t: (t, col // D_CONV))
    return pl.pallas_call(
        _conv_kernel,
        grid=(R_ALL // TR,),
        in_specs=[
            pl.BlockSpec((TR, D_CONV), pcol(COL_CB)),
            pl.BlockSpec((TR, D_CONV), pcol(COL_CC)),
            pl.BlockSpec((TR, D_CONV), pcol(COL_CH)),
            pl.BlockSpec((N_META, D_CONV), lambda t: (mrow, COL_CC // D_CONV)),
            pl.BlockSpec((N_META, D_CONV), lambda t: (mrow, COL_CH // D_CONV)),
            pl.BlockSpec((CONV_WIDTH, D_CONV), const),
            pl.BlockSpec((DEC_BATCH, 2 * D_CONV), const),
        ],
        out_specs=[
            pl.BlockSpec((TR, D_CONV), lambda t: (t, 0)),
            pl.BlockSpec((1, CONV_WIDTH - 1, D_CONV),
                         lambda t: (jnp.minimum(t, CONV_PROMPT_BLOCKS - 1) // CONV_SEQ_BLOCKS, 0, 0)),
            pl.BlockSpec((DEC_BATCH, 2 * D_CONV), const),
        ],
        out_shape=[jax.ShapeDtypeStruct((R_ALL, D_CONV), bf16),
                   jax.ShapeDtypeStruct((BATCH, CONV_WIDTH - 1, D_CONV), f32),
                   jax.ShapeDtypeStruct((DEC_BATCH, 2 * D_CONV), f32)],
        scratch_shapes=[pltpu.VMEM((CONV_PAD + TR, D_CONV), f32)],
        compiler_params=_params(1),
        name="conv",
    )(proj, proj, proj, proj, proj, conv_w, buf)


OUT_FULL = TAIL_ROW0 // OUT_TM


def _outproj_kernel(o_ref, y_ref, wo_ref, wy_ref, x_ref, mu_ref, rs_ref, g_ref, b_ref, ht_ref,
                    s_ref):
    i = pl.program_id(0)

    def mix():
        return (_dot(o_ref[...], wo_ref[...].astype(bf16))
                + _dot(y_ref[...], wy_ref[...].astype(bf16)))

    @pl.when(i < OUT_FULL)
    def _():
        cols = pl.ds(pl.multiple_of(pl.program_id(1) * OUT_TN, OUT_TN), OUT_TN)
        s_ref[...] = _ln_from_stats(x_ref[...], mu_ref, rs_ref, g_ref[:, cols], b_ref[:, cols]) + mix()

    @pl.when(i >= OUT_FULL)
    def _():
        s_ref[...] = ALPHA * ht_ref[...] + mix()


def _outproj(og, yc, w_out, xp, mu, rs, g, b, h_tail):
    full = lambda i: i < OUT_FULL
    return pl.pallas_call(
        _outproj_kernel,
        grid=(R_MAIN // OUT_TM, D_MODEL // OUT_TN),
        in_specs=[
            pl.BlockSpec((OUT_TM, D_GLA_V), lambda i, j: (i, 0)),
            pl.BlockSpec((OUT_TM, D_CONV), lambda i, j: (i, 0)),
            pl.BlockSpec((D_GLA_V, OUT_TN), lambda i, j: (0, j)),
            pl.BlockSpec((D_CONV, OUT_TN), lambda i, j: (1, j)),
            pl.BlockSpec((OUT_TM, OUT_TN),
                         lambda i, j: (jnp.minimum(i, OUT_FULL - 1), jnp.where(full(i), j, 0))),
            pl.BlockSpec((OUT_TM, LANES), lambda i, j: (i, 0)),
            pl.BlockSpec((OUT_TM, LANES), lambda i, j: (i, 0)),
            pl.BlockSpec((1, D_MODEL), lambda i, j: (0, 0)),
            pl.BlockSpec((1, D_MODEL), lambda i, j: (0, 0)),
            pl.BlockSpec((OUT_TM, OUT_TN), lambda i, j: (0, jnp.where(full(i), 0, j))),
        ],
        out_specs=pl.BlockSpec((OUT_TM, OUT_TN), lambda i, j: (i, j)),
        out_shape=jax.ShapeDtypeStruct((R_MAIN, D_MODEL), f32),
        compiler_params=_params(2),
        name="outproj",
    )(og, yc, w_out, w_out, xp, mu, rs, g, b, h_tail)


LN1_ROWS = 320


def _ln1_kernel(s_ref, g_ref, b_ref, hb_ref, mu_ref, rs_ref):
    x = s_ref[...]
    mu, rs = _ln_stats(x)
    hb_ref[...] = ((x - mu) * rs * g_ref[...] + b_ref[...]).astype(bf16)
    _store_stats(mu, rs, mu_ref, rs_ref)


def _ln1(s1, g, b):
    return pl.pallas_call(
        _ln1_kernel,
        grid=(R_MAIN // LN1_ROWS,),
        in_specs=[
            pl.BlockSpec((LN1_ROWS, D_MODEL), lambda i: (i, 0)),
            pl.BlockSpec((1, D_MODEL), lambda i: (0, 0)),
            pl.BlockSpec((1, D_MODEL), lambda i: (0, 0)),
        ],
        out_specs=[
            pl.BlockSpec((LN1_ROWS, D_MODEL), lambda i: (i, 0)),
            pl.BlockSpec((LN1_ROWS, LANES), lambda i: (i, 0)),
            pl.BlockSpec((LN1_ROWS, LANES), lambda i: (i, 0)),
        ],
        out_shape=[jax.ShapeDtypeStruct((R_MAIN, D_MODEL), bf16),
                   jax.ShapeDtypeStruct((R_MAIN, LANES), f32),
                   jax.ShapeDtypeStruct((R_MAIN, LANES), f32)],
        compiler_params=_params(1),
        name="ln1",
    )(s1, g, b)


LN2_ROWS = 512


def _ln2_kernel(sp_ref, ss_ref, g_ref, b_ref, yp_ref, ys_ref):
    yp_ref[...] = _layer_norm(sp_ref[...], g_ref[...], b_ref[...])

    @pl.when(pl.program_id(0) == 0)
    def _():
        ys_ref[...] = _layer_norm(ss_ref[...], g_ref[...], b_ref[...])


def _ln2(s2, g, b):
    return pl.pallas_call(
        _ln2_kernel,
        grid=(NP // LN2_ROWS,),
        in_specs=[
            pl.BlockSpec((LN2_ROWS, D_MODEL), lambda i: (i, 0)),
            pl.BlockSpec((DEC_BATCH, D_MODEL), lambda i: (NP // DEC_BATCH, 0)),
            pl.BlockSpec((1, D_MODEL), lambda i: (0, 0)),
            pl.BlockSpec((1, D_MODEL), lambda i: (0, 0)),
        ],
        out_specs=[
            pl.BlockSpec((LN2_ROWS, D_MODEL), lambda i: (i, 0)),
            pl.BlockSpec((DEC_BATCH, D_MODEL), lambda i: (0, 0)),
        ],
        out_shape=[jax.ShapeDtypeStruct((NP, D_MODEL), f32),
                   jax.ShapeDtypeStruct((DEC_BATCH, D_MODEL), f32)],
        compiler_params=_params(1),
        name="ln2",
    )(s2, s2, g, b)


FFN_TM, FFN_TN = 2080, 256


FFN_NT = D_FF // FFN_TN


def _ffn_up_kernel(h_ref, wg_ref, wu_ref, wd_ref, act_ref, wdb_ref):
    i = pl.program_id(0)
    j = pl.program_id(1)

    @pl.when(j < FFN_NT)
    def _():
        h = h_ref[...]
        a = _dot(h, wg_ref[...].astype(bf16))
        u = _dot(h, wu_ref[...].astype(bf16))
        act_ref[...] = (a * jax.nn.sigmoid(a) * u).astype(bf16)

    @pl.when(j >= FFN_NT)
    def _():
        act_ref[...] = jnp.zeros_like(act_ref)

    @pl.when(jnp.logical_and(i == 0, j < FFN_NT))
    def _():
        wdb_ref[...] = wd_ref[...].astype(bf16)

    @pl.when(jnp.logical_and(i == 0, j >= FFN_NT))
    def _():
        wdb_ref[...] = jnp.zeros_like(wdb_ref)


def _ffn_up(hb, wg, wu, wd):
    nt_pad = D_FF_PAD // FFN_TN
    jw = lambda j: jnp.minimum(j, FFN_NT - 1)
    return pl.pallas_call(
        _ffn_up_kernel,
        grid=(R_MAIN // FFN_TM, nt_pad),
        in_specs=[
            pl.BlockSpec((FFN_TM, D_MODEL), lambda i, j: (i, 0), pipeline_mode=pl.Buffered(1)),
            pl.BlockSpec((D_MODEL, FFN_TN), lambda i, j: (0, jw(j))),
            pl.BlockSpec((D_MODEL, FFN_TN), lambda i, j: (0, jw(j))),
            pl.BlockSpec((FFN_TN, D_MODEL), lambda i, j: (jnp.where(i == 0, jw(j), FFN_NT - 1), 0)),
        ],
        out_specs=[
            pl.BlockSpec((FFN_TM, FFN_TN), lambda i, j: (i, j)),
            pl.BlockSpec((FFN_TN, D_MODEL), lambda i, j: (jnp.where(i == 0, j, nt_pad - 1), 0)),
        ],
        out_shape=[jax.ShapeDtypeStruct((R_MAIN, D_FF_PAD), bf16),
                   jax.ShapeDtypeStruct((D_FF_PAD, D_MODEL), bf16)],
        compiler_params=_params(2),
        name="ffn_up",
    )(hb, wg, wu, wd)


DOWN_TM, DOWN_TN, DOWN_TK = 2080, 1024, 1024


def _ffn_down_kernel(a_ref, w_ref, s1_ref, mu_ref, rs_ref, g_ref, b_ref, s_ref):
    @pl.when(pl.program_id(2) == 0)
    def _():
        s_ref[...] = (_ln_from_stats(s1_ref[...], mu_ref, rs_ref, g_ref[...], b_ref[...])
                      + _dot(a_ref[...], w_ref[...]))

    @pl.when(pl.program_id(2) > 0)
    def _():
        s_ref[...] += _dot(a_ref[...], w_ref[...])


def _ffn_down(act, wd, s1, mu, rs, g, b):
    return pl.pallas_call(
        _ffn_down_kernel,
        grid=(R_MAIN // DOWN_TM, D_MODEL // DOWN_TN, D_FF_PAD // DOWN_TK),
        in_specs=[
            pl.BlockSpec((DOWN_TM, DOWN_TK), lambda i, j, k: (i, k)),
            pl.BlockSpec((DOWN_TK, DOWN_TN), lambda i, j, k: (k, j)),
            pl.BlockSpec((DOWN_TM, DOWN_TN), lambda i, j, k: (i, j)),
            pl.BlockSpec((DOWN_TM, LANES), lambda i, j, k: (i, 0)),
            pl.BlockSpec((DOWN_TM, LANES), lambda i, j, k: (i, 0)),
            pl.BlockSpec((1, DOWN_TN), lambda i, j, k: (0, j)),
            pl.BlockSpec((1, DOWN_TN), lambda i, j, k: (0, j)),
        ],
        out_specs=pl.BlockSpec((DOWN_TM, DOWN_TN), lambda i, j, k: (i, j)),
        out_shape=jax.ShapeDtypeStruct((R_MAIN, D_MODEL), f32),
        compiler_params=_params(3),
        name="ffn_down",
    )(act, wd, s1, mu, rs, g, b)


def kernel(x_prompt, x_sample, state_gla, state_conv, meta_tokens, emb_ln_g, emb_ln_b,
           w_in, w_a2, b_a, gla_norm_g, conv_w, w_out, ln1_g, ln1_b,
           w_ffn_gate, w_ffn_up, w_ffn_down, ln2_g, ln2_b):
    assert x_prompt.shape == (BATCH, SEQ, D_MODEL) and x_sample.shape == (DEC_BATCH, 1, D_MODEL)
    assert w_in.shape[0] == 1, "single layer"
    row = lambda v: v.reshape(1, -1)

    w_in_t = jnp.transpose(w_in[0])
    wa2p = jnp.pad(w_a2[0], ((0, LANES - GATE_RANK), (0, 0)))

    xp = x_prompt.reshape(NP, D_MODEL)
    xt = jnp.concatenate([x_sample.reshape(DEC_BATCH, D_MODEL), meta_tokens.astype(f32),
                          jnp.zeros((LN_ROWS - DEC_BATCH - N_META, D_MODEL), f32)], axis=0)

    g0, b0 = row(emb_ln_g), row(emb_ln_b)
    g1, b1 = row(ln1_g[0]), row(ln1_b[0])
    hb, alr, mu0, rs0, h_tail = _ln0(xp, xt, g0, b0, w_in_t)
    proj, la2 = _proj(hb, w_in_t, alr, wa2p, row(b_a[0]))

    ng = row(gla_norm_g[0])
    s_meta = _gla_meta(proj, la2)
    og, s_p, s_s = _gla(proj, la2, ng, s_meta, state_gla[0])
    yc, nb_p, nb_s = _conv(proj, conv_w[0], state_conv[0].reshape(DEC_BATCH, 2 * D_CONV))

    s1 = _outproj(og, yc, w_out[0], xp, mu0, rs0, ALPHA * g0, ALPHA * b0, h_tail)
    h1b, mu1, rs1 = _ln1(s1, g1, b1)
    act, wd = _ffn_up(h1b, w_ffn_gate[0], w_ffn_up[0], w_ffn_down[0])
    s2 = _ffn_down(act, wd, s1, mu1, rs1, ALPHA * g1, ALPHA * b1)
    y_p, y_s = _ln2(s2, row(ln2_g[0]), row(ln2_b[0]))

    return (y_p.reshape(BATCH, SEQ, D_MODEL),
            y_s.reshape(DEC_BATCH, 1, D_MODEL),
            s_p[None],
            nb_p[None],
            s_s[None],
            nb_s.reshape(1, DEC_BATCH, CONV_WIDTH - 1, D_CONV))
```
